```python
import jax, jax.numpy as jnp
from jax import lax
import numpy as np

D_MODEL = 1024
BATCH = 4
SEQ = 4096
DEPTH = 1

N_META = 16
GRID_W = 64
M_HEADS = 4
M_DV = 256
M_DK = 128
M_INNER = M_HEADS * M_DV
M_CONV = 3
M_CHUNK = 64
NA_HEADS = 8
NA_DH = 64
NA_INNER = NA_HEADS * NA_DH
NA_WIN_H_MAX = 8
NA_WIN_W = 16
NA_SEG_W = NA_WIN_W
NA_REGION_W = 2 * NA_WIN_W
N_BRANCH = 2
D_FF = 4 * D_MODEL
EPS = 1e-6
NEG_LOG_GATE = -1e9
OFF_MX = 0
OFF_MO = OFF_MX + M_INNER
OFF_MG = OFF_MO + M_INNER
OFF_Q = OFF_MG + 4 * M_HEADS
OFF_K = OFF_Q + NA_INNER
OFF_V = OFF_K + NA_INNER
OFF_G = OFF_V + NA_INNER
D_IN_PROJ = OFF_G + N_BRANCH * D_MODEL

kernel_name = "hybrid_mlstm_natten_block"


def rms_norm(x, g):
    xf = x.astype(jnp.float32)
    y = xf * lax.rsqrt(jnp.mean(jnp.square(xf), axis=-1, keepdims=True) + EPS)
    return (y * g.astype(jnp.float32)).astype(x.dtype)


def centred_depthwise_conv(x, w):
    k = w.shape[0]
    return lax.conv_general_dilated(x, w.astype(x.dtype), (1,), [(k // 2, k // 2)],
                                    dimension_numbers=('NWC', 'WIO', 'NWC'),
                                    feature_group_count=x.shape[-1])


def mlstm_chunkwise(q, k, v, log_i, log_f):
    B, H, Lp, dk = q.shape
    dv = v.shape[-1]
    nc = Lp // M_CHUNK
    chunk = lambda a: jnp.moveaxis(a.reshape(B, H, nc, M_CHUNK, *a.shape[3:]), 2, 0)
    causal = jnp.tril(jnp.ones((M_CHUNK, M_CHUNK), dtype=bool))

    def step(carry, inp):
        C, n, m = carry
        qt, kt, vt, li, lf = inp
        b = jnp.cumsum(lf, axis=-1)
        d = jnp.where(causal, b[..., :, None] - b[..., None, :] + li[..., None, :], -jnp.inf)
        m_inter = b + m[..., None]
        m_t = jnp.maximum(m_inter, jnp.max(d, axis=-1))
        w_inter = jnp.exp(m_inter - m_t)
        s = jnp.einsum('bhtd,bhsd->bhts', qt, kt) * jnp.exp(d - m_t[..., None])
        num = w_inter[..., None] * jnp.einsum('bhtd,bhde->bhte', qt, C) + jnp.einsum('bhts,bhse->bhte', s, vt)
        den = w_inter * jnp.einsum('bhtd,bhd->bht', qt, n) + jnp.sum(s, axis=-1)
        h = num / jnp.maximum(jnp.abs(den), jnp.exp(-m_t))[..., None]
        b_end = b[..., -1]
        a = b_end[..., None] - b + li
        m_new = jnp.maximum(b_end + m, jnp.max(a, axis=-1))
        decay = jnp.exp(b_end + m - m_new)
        kw = kt * jnp.exp(a - m_new[..., None])[..., None]
        C_new = decay[..., None, None] * C + jnp.einsum('bhsd,bhse->bhde', kw, vt)
        n_new = decay[..., None] * n + jnp.sum(kw, axis=2)
        return (C_new, n_new, m_new), h

    init = (jnp.zeros((B, H, dk, dv), jnp.float32), jnp.zeros((B, H, dk), jnp.float32),
            jnp.zeros((B, H), jnp.float32))
    _, hs = lax.scan(step, init, (chunk(q), chunk(k), chunk(v), chunk(log_i), chunk(log_f)))
    return jnp.moveaxis(hs, 0, 2).reshape(B, H, Lp, dv)


def mlstm_bidirectional(q, k, v, gates):
    n_pad = M_CHUNK - N_META
    pad_t = lambda a: jnp.pad(a, ((0, 0), (0, 0), (n_pad, 0), (0, 0)))
    qp, kp, vp = pad_t(q), pad_t(k), pad_t(v)
    g = jnp.transpose(gates, (0, 2, 3, 1))
    pad_g = lambda a, c: jnp.pad(a, ((0, 0), (0, 0), (n_pad, 0)), constant_values=c)
    li_f = pad_g(g[:, 0], NEG_LOG_GATE)
    lf_f = pad_g(jax.nn.log_sigmoid(g[:, 1]), 0.0)
    li_b = pad_g(g[:, 2], NEG_LOG_GATE)
    lf_b = pad_g(jax.nn.log_sigmoid(g[:, 3]), 0.0)
    h_f = mlstm_chunkwise(qp, kp, vp, li_f, lf_f)
    ft = lambda a: jnp.flip(a, axis=2)
    h_b = ft(mlstm_chunkwise(ft(qp), ft(kp), ft(vp), jnp.flip(li_b, -1), jnp.flip(lf_b, -1)))
    return (h_f + h_b)[:, :, n_pad:]


def neighbourhood_attention(q, k, v, rpb, meta_bias, rows):
    B, H, _, dh = q.shape
    qm, km, vm = q[:, :, :N_META], k[:, :, :N_META], v[:, :, :N_META]
    qr, kr, vr = q[:, :, N_META:], k[:, :, N_META:], v[:, :, N_META:]
    wh = min(NA_WIN_H_MAX, rows)
    n_seg = GRID_W // NA_SEG_W
    nk = wh * NA_REGION_W
    seg = jnp.arange(n_seg)
    qcols = seg[:, None] * NA_SEG_W + jnp.arange(NA_SEG_W)[None, :]
    reg0 = jnp.clip(seg * NA_SEG_W - NA_WIN_W // 2, 0, GRID_W - NA_REGION_W)
    kcols = reg0[:, None] + jnp.arange(NA_REGION_W)[None, :]
    win0 = jnp.clip(qcols - NA_WIN_W // 2, 0, GRID_W - NA_WIN_W)
    col_ok = (kcols[:, None, :] >= win0[..., None]) & (kcols[:, None, :] < win0[..., None] + NA_WIN_W)
    mask = jnp.broadcast_to(col_ok[:, :, None, :], (n_seg, NA_SEG_W, wh, NA_REGION_W)).reshape(n_seg, NA_SEG_W, nk)
    dc = jnp.clip(kcols[:, None, :] - qcols[..., None], -(NA_WIN_W - 1), NA_WIN_W - 1) + NA_WIN_W - 1
    mb = meta_bias.astype(jnp.float32)

    def row_block(r):
        r0 = jnp.clip(r - wh // 2, 0, rows - wh)
        krows = r0 + jnp.arange(wh)
        dr = krows - r + NA_WIN_H_MAX - 1
        bias = rpb[:, dr[None, None, :, None], dc[:, :, None, :]].reshape(H, n_seg, NA_SEG_W, nk)
        kidx = (krows[None, :, None] * GRID_W + kcols[:, None, :]).reshape(-1)
        qidx = (r * GRID_W + qcols).reshape(-1)
        qb = jnp.take(qr, qidx, axis=2).reshape(B, H, n_seg, NA_SEG_W, dh)
        kb = jnp.take(kr, kidx, axis=2).reshape(B, H, n_seg, nk, dh)
        vb = jnp.take(vr, kidx, axis=2).reshape(B, H, n_seg, nk, dh)
        s_loc = jnp.einsum('bhnqd,bhnkd->bhnqk', qb, kb).astype(jnp.float32) + bias.astype(jnp.float32)
        s_loc = jnp.where(mask, s_loc, -jnp.inf)
        s_met = jnp.einsum('bhnqd,bhmd->bhnqm', qb, km).astype(jnp.float32) + mb[:, None, None, :]
        p = jax.nn.softmax(jnp.concatenate([s_loc, s_met], axis=-1), axis=-1).astype(v.dtype)
        return (jnp.einsum('bhnqk,bhnkd->bhnqd', p[..., :nk], vb)
                + jnp.einsum('bhnqm,bhmd->bhnqd', p[..., nk:], vm))

    o_r = lax.map(row_block, jnp.arange(rows))
    o_r = jnp.moveaxis(o_r, 0, 2).reshape(B, H, rows * GRID_W, dh)
    s_mm = jnp.einsum('bhqd,bhmd->bhqm', qm, km).astype(jnp.float32) + mb[:, None, :]
    o_m = jnp.einsum('bhqm,bhmd->bhqd', jax.nn.softmax(s_mm, axis=-1).astype(v.dtype), vm)
    return jnp.concatenate([o_m, o_r], axis=2)


def setup_inputs(seed: int = 0) -> dict:
    key = jax.random.key(seed)
    ks = jax.random.split(key, 24)
    nrm = lambda kk, shape, scale: jax.random.normal(kk, shape, jnp.float32) * scale
    f_bias = jnp.linspace(3.0, 6.0, M_HEADS, dtype=jnp.float32)
    gate_b = jnp.stack([nrm(ks[6], (M_HEADS,), 0.1),
                        f_bias + nrm(ks[7], (M_HEADS,), 0.1),
                        nrm(ks[8], (M_HEADS,), 0.1),
                        f_bias + nrm(ks[9], (M_HEADS,), 0.1)])
    return {
        'x': nrm(ks[0], (BATCH, SEQ, D_MODEL), 1.0),
        'meta_tokens': nrm(ks[1], (N_META, D_MODEL), 1.0),
        'norm1_g': 1.0 + nrm(ks[2], (D_MODEL,), 0.02),
        'w_in': nrm(ks[3], (D_MODEL, D_IN_PROJ), D_MODEL ** -0.5),
        'mlstm_conv_w': nrm(ks[4], (M_CONV, 1, M_INNER), M_CONV ** -0.5),
        'mlstm_conv_b': nrm(ks[5], (M_INNER,), 0.02),
        'mlstm_wq': nrm(ks[10], (M_HEADS, M_DV, M_DK), M_DV ** -0.5),
        'mlstm_wk': nrm(ks[11], (M_HEADS, M_DV, M_DK), M_DV ** -0.5),
        'mlstm_gate_b': gate_b,
        'mlstm_norm_g': 1.0 + nrm(ks[12], (M_HEADS, M_DV), 0.02),
        'mlstm_skip': 1.0 + nrm(ks[13], (M_INNER,), 0.02),
        'na_q_norm_g': 1.0 + nrm(ks[14], (NA_DH,), 0.02),
        'na_k_norm_g': 1.0 + nrm(ks[15], (NA_DH,), 0.02),
        'na_rpb': nrm(ks[16], (NA_HEADS, 2 * NA_WIN_H_MAX - 1, 2 * NA_WIN_W - 1), 0.02),
        'na_meta_bias': nrm(ks[17], (NA_HEADS, N_META), 0.02),
        'w_branch_a': nrm(ks[18], (M_INNER, D_MODEL), M_INNER ** -0.5),
        'w_branch_b': nrm(ks[19], (NA_INNER, D_MODEL), NA_INNER ** -0.5),
        'w_out': nrm(ks[20], (D_MODEL, D_MODEL), D_MODEL ** -0.5),
        'norm2_g': 1.0 + nrm(ks[21], (D_MODEL,), 0.02),
        'w_ff1': nrm(ks[22], (D_MODEL, D_FF), D_MODEL ** -0.5),
        'w_ff2': nrm(ks[23], (D_FF, D_MODEL), D_FF ** -0.5),
    }


def reference(x, meta_tokens, norm1_g, w_in, mlstm_conv_w, mlstm_conv_b, mlstm_wq, mlstm_wk,
              mlstm_gate_b, mlstm_norm_g, mlstm_skip, na_q_norm_g, na_k_norm_g, na_rpb, na_meta_bias,
              w_branch_a, w_branch_b, w_out, norm2_g, w_ff1, w_ff2):
    B, S, D = x.shape
    rows = S // GRID_W
    L = N_META + S
    h = jnp.concatenate([jnp.broadcast_to(meta_tokens[None].astype(x.dtype), (B, N_META, D)), x], axis=1)
    for _ in range(DEPTH):
        xn = rms_norm(h, norm1_g)
        u = xn @ w_in
        xm = u[..., OFF_MX:OFF_MO]
        xc = jax.nn.silu(centred_depthwise_conv(xm, mlstm_conv_w) + mlstm_conv_b)
        xch = xc.reshape(B, L, M_HEADS, M_DV)
        q_m = jnp.einsum('blhe,hed->bhld', xch, mlstm_wq).astype(jnp.float32)
        k_m = jnp.einsum('blhe,hed->bhld', xch, mlstm_wk).astype(jnp.float32) * (M_DK ** -0.5)
        v_m = xm.reshape(B, L, M_HEADS, M_DV).transpose(0, 2, 1, 3).astype(jnp.float32)
        gates = (u[..., OFF_MG:OFF_Q].reshape(B, L, 4, M_HEADS) + mlstm_gate_b).astype(jnp.float32)
        h_m = mlstm_bidirectional(q_m, k_m, v_m, gates)
        h_m = h_m * lax.rsqrt(jnp.mean(jnp.square(h_m), axis=-1, keepdims=True) + EPS)
        h_m = h_m * mlstm_norm_g.astype(jnp.float32)[None, :, None, :]
        h_m = h_m.transpose(0, 2, 1, 3).reshape(B, L, M_INNER).astype(x.dtype)
        y_a = jax.nn.sigmoid(u[..., OFF_MO:OFF_MG]) * (h_m + mlstm_skip * xc)
        q_n = rms_norm(u[..., OFF_Q:OFF_K].reshape(B, L, NA_HEADS, NA_DH), na_q_norm_g).transpose(0, 2, 1, 3) * (NA_DH ** -0.5)
        k_n = rms_norm(u[..., OFF_K:OFF_V].reshape(B, L, NA_HEADS, NA_DH), na_k_norm_g).transpose(0, 2, 1, 3)
        v_n = u[..., OFF_V:OFF_G].reshape(B, L, NA_HEADS, NA_DH).transpose(0, 2, 1, 3)
        o_n = neighbourhood_attention(q_n, k_n, v_n, na_rpb, na_meta_bias, rows)
        y_b = o_n.transpose(0, 2, 1, 3).reshape(B, L, NA_INNER)
        g_a = jax.nn.sigmoid(u[..., OFF_G:OFF_G + D_MODEL])
        g_b = jax.nn.sigmoid(u[..., OFF_G + D_MODEL:OFF_G + 2 * D_MODEL])
        mix = g_a * (y_a @ w_branch_a) + g_b * (y_b @ w_branch_b)
        h = h + mix @ w_out
        z = rms_norm(h, norm2_g) @ w_ff1
        h = h + jnp.square(jax.nn.relu(z)) @ w_ff2
    return h[:, N_META:]
```

```python
import functools

import jax
import jax.numpy as jnp
from jax import lax
from jax.experimental import pallas as pl
from jax.experimental.pallas import tpu as pltpu

D_MODEL = 1024
BATCH = 4
SEQ = 4096
N_META = 16
GRID_W = 64
ROWS = SEQ // GRID_W
M_HEADS = 4
M_DV = 256
M_DK = 128
M_INNER = M_HEADS * M_DV
NA_HEADS = 8
NA_DH = 64
NA_INNER = NA_HEADS * NA_DH
NA_WIN_H = 8
NA_WIN_W = 16
D_FF = 4 * D_MODEL
EPS = 1e-6
NEG_LOG_GATE = -1e9
MASKED = -1e30

TAIL = 128
NPAD = TAIL - N_META
CHUNK = 128
N_REAL = BATCH * SEQ
NT = N_REAL + BATCH * TAIL
TM = 512
N_TILES = NT // TM
NA_KEYS = NA_WIN_H * GRID_W + TAIL

VMEM_LIMIT = 56 * 1024 * 1024

F32 = jnp.float32
BF16 = jnp.bfloat16


def _dot(a, b):
    return jnp.dot(a, b, preferred_element_type=F32)


def _dot_nt(a, b):
    return lax.dot_general(a, b, (((1,), (1,)), ((), ())), preferred_element_type=F32)


def _dot_tn(a, b):
    return lax.dot_general(a, b, (((0,), (0,)), ((), ())), preferred_element_type=F32)


def _sigmoid(z):
    return 1.0 / (1.0 + jnp.exp(-z))


def _log_sigmoid(z):
    return jnp.minimum(z, 0.0) - jnp.log1p(jnp.exp(-jnp.abs(z)))


def _const_spec(shape):
    nd = len(shape)
    return pl.BlockSpec(shape, lambda *_: (0,) * nd, pipeline_mode=pl.Buffered(1))


def _params(*sem):
    return pltpu.CompilerParams(dimension_semantics=sem, vmem_limit_bytes=VMEM_LIMIT)


def _inproj_kernel(h_ref, g1_ref, wx_ref, wo_ref, wg_ref, wqkv_ref, wgab_ref, gb_ref,
                   qg_ref, kg_ref, bd_ref,
                   xm_ref, og_ref, gates_ref, q_ref, k_ref, v_ref, gab_ref):
    h = h_ref[...]
    ms = jnp.mean(h * h, axis=-1, keepdims=True)
    xn = (h * lax.rsqrt(ms + EPS) * g1_ref[...]).astype(BF16)
    xm_ref[...] = _dot(xn, wx_ref[...])
    og_ref[...] = _dot(xn, wo_ref[...])
    gates_ref[...] = _dot(xn, wg_ref[...])[:, :4 * M_HEADS] + gb_ref[...]
    gab_ref[...] = _dot(xn, wgab_ref[...])
    qkv = _dot(xn, wqkv_ref[...])
    uq = qkv[:, :NA_INNER]
    uk = qkv[:, NA_INNER:2 * NA_INNER]
    msq = _dot((uq * uq).astype(BF16), bd_ref[...])
    msk = _dot((uk * uk).astype(BF16), bd_ref[...])
    q_ref[...] = (uq * lax.rsqrt(msq + EPS) * qg_ref[...]).astype(BF16)
    k_ref[...] = (uk * lax.rsqrt(msk + EPS) * kg_ref[...]).astype(BF16)
    v_ref[...] = qkv[:, 2 * NA_INNER:].astype(BF16)


def _inproj(hp, g1, wx, wo, wg, wqkv, wgab, gb, qg, kg, bd):
    row = lambda w: pl.BlockSpec((TM, w), lambda i: (i, 0))
    return pl.pallas_call(
        _inproj_kernel,
        grid=(N_TILES,),
        in_specs=[row(D_MODEL)] + [_const_spec(a.shape) for a in (g1, wx, wo, wg, wqkv, wgab, gb, qg, kg, bd)],
        out_specs=[row(M_INNER), row(M_INNER), row(4 * M_HEADS), row(NA_INNER), row(NA_INNER),
                   row(NA_INNER), row(2 * D_MODEL)],
        out_shape=[jax.ShapeDtypeStruct((NT, M_INNER), F32),
                   jax.ShapeDtypeStruct((NT, M_INNER), F32),
                   jax.ShapeDtypeStruct((NT, 4 * M_HEADS), F32),
                   jax.ShapeDtypeStruct((NT, NA_INNER), BF16),
                   jax.ShapeDtypeStruct((NT, NA_INNER), BF16),
                   jax.ShapeDtypeStruct((NT, NA_INNER), BF16),
                   jax.ShapeDtypeStruct((NT, 2 * D_MODEL), F32)],
        compiler_params=_params("parallel"),
        name="inproj",
    )(hp, g1, wx, wo, wg, wqkv, wgab, gb, qg, kg, bd)


CT = 128
N_CT = SEQ // CT
HALO = 8


def _conv_kernel(x_ref, prev_ref, next_ref, cw_ref, cb_ref, wqk_ref, xc_ref, q_ref, k_ref):
    x = x_ref[...]
    rows = lax.broadcasted_iota(jnp.int32, (CT, M_INNER), 0)
    x_m1 = jnp.where(rows == 0, prev_ref[HALO - 1:HALO, :], pltpu.roll(x, 1, axis=0))
    x_p1 = jnp.where(rows == CT - 1, next_ref[0:1, :], pltpu.roll(x, CT - 1, axis=0))
    z = cw_ref[0:1, :] * x_m1 + cw_ref[1:2, :] * x + cw_ref[2:3, :] * x_p1 + cb_ref[...]
    xc = z * _sigmoid(z)
    xc_ref[...] = xc
    for hd in range(M_HEADS):
        qk = _dot(xc[:, hd * M_DV:(hd + 1) * M_DV].astype(BF16), wqk_ref[hd])
        q_ref[:, hd * M_DK:(hd + 1) * M_DK] = qk[:, :M_DK].astype(BF16)
        k_ref[:, hd * M_DK:(hd + 1) * M_DK] = (qk[:, M_DK:] * (M_DK ** -0.5)).astype(BF16)


def _conv(xm, cw, cb, wqk):
    real_blocks = SEQ // HALO
    tail_blocks = TAIL // HALO
    tail0 = N_REAL // HALO

    def main_map(b, j):
        return (jnp.where(j < N_CT, b * N_CT + j, N_REAL // CT + b), 0)

    def prev_map(b, j):
        inner = b * real_blocks + j * (CT // HALO) - 1
        first = tail0 + b * tail_blocks + tail_blocks - 1
        tail = b * real_blocks + real_blocks - 1
        return (jnp.where(j == 0, first, jnp.where(j == N_CT, tail, inner)), 0)

    def next_map(b, j):
        inner = b * real_blocks + (j + 1) * (CT // HALO)
        last = tail0 + b * tail_blocks
        tail = b * real_blocks
        return (jnp.where(j == N_CT - 1, last, jnp.where(j == N_CT, tail, inner)), 0)

    return pl.pallas_call(
        _conv_kernel,
        grid=(BATCH, N_CT + 1),
        in_specs=[pl.BlockSpec((CT, M_INNER), main_map),
                  pl.BlockSpec((HALO, M_INNER), prev_map),
                  pl.BlockSpec((HALO, M_INNER), next_map),
                  _const_spec(cw.shape), _const_spec(cb.shape), _const_spec(wqk.shape)],
        out_specs=[pl.BlockSpec((CT, M_INNER), main_map),
                   pl.BlockSpec((CT, M_HEADS * M_DK), main_map),
                   pl.BlockSpec((CT, M_HEADS * M_DK), main_map)],
        out_shape=[jax.ShapeDtypeStruct((NT, M_INNER), F32),
                   jax.ShapeDtypeStruct((NT, M_HEADS * M_DK), BF16),
                   jax.ShapeDtypeStruct((NT, M_HEADS * M_DK), BF16)],
        compiler_params=_params("parallel", "parallel"),
        name="conv_qk",
    )(xm, xm, xm, cw, cb, wqk)


N_CHUNK = SEQ // CHUNK


def _mlstm_step(dirn, q, k, v, gcol, grow, is_tail, c_ref, n_ref, m_ref):
    t = CHUNK
    li_c = gcol[:, 2 * dirn:2 * dirn + 1]
    lf_c = _log_sigmoid(gcol[:, 2 * dirn + 1:2 * dirn + 2])
    li_r = grow[2 * dirn:2 * dirn + 1, :]
    lf_r = _log_sigmoid(grow[2 * dirn + 1:2 * dirn + 2, :])
    if is_tail:
        pad_c = lax.broadcasted_iota(jnp.int32, (t, 1), 0) < NPAD
        pad_r = lax.broadcasted_iota(jnp.int32, (1, t), 1) < NPAD
        li_c = jnp.where(pad_c, NEG_LOG_GATE, li_c)
        lf_c = jnp.where(pad_c, 0.0, lf_c)
        li_r = jnp.where(pad_r, NEG_LOG_GATE, li_r)
        lf_r = jnp.where(pad_r, 0.0, lf_r)
    ri = lax.broadcasted_iota(jnp.int32, (t, t), 0)
    ci = lax.broadcasted_iota(jnp.int32, (t, t), 1)
    tri = (ci <= ri) if dirn == 0 else (ci >= ri)
    tri_t = (ri <= ci) if dirn == 0 else (ri >= ci)
    b_c = jnp.sum(jnp.where(tri, lf_r, 0.0), axis=1, keepdims=True)
    b_r = jnp.sum(jnp.where(tri_t, lf_c, 0.0), axis=0, keepdims=True)
    d = jnp.where(tri, b_c - b_r + li_r, MASKED)
    m_prev = m_ref[dirn]
    m_inter = b_c + m_prev
    m_t = jnp.maximum(m_inter, jnp.max(d, axis=1, keepdims=True))
    w_inter = jnp.exp(m_inter - m_t)
    p = _dot_nt(q, k) * jnp.exp(d - m_t)
    c_old = c_ref[dirn]
    n_old = n_ref[dirn]
    vb = v.astype(BF16)
    num = w_inter * _dot(q, c_old.astype(BF16)) + _dot(p.astype(BF16), vb)
    den = (w_inter * jnp.sum(q.astype(F32) * n_old, axis=1, keepdims=True)
           + jnp.sum(p, axis=1, keepdims=True))
    h = num / jnp.maximum(jnp.abs(den), jnp.exp(-m_t))
    b_end = b_c[t - 1:t, :] if dirn == 0 else b_c[0:1, :]
    a_c = b_end - b_c + li_c
    m_new = jnp.maximum(b_end + m_prev, jnp.max(a_c, axis=0, keepdims=True))
    decay = jnp.exp(b_end + m_prev - m_new)
    kw = k.astype(F32) * jnp.exp(a_c - m_new)
    c_ref[dirn] = decay * c_old + _dot_tn(kw.astype(BF16), vb)
    n_ref[dirn] = decay * n_old + jnp.sum(kw, axis=0, keepdims=True)
    m_ref[dirn] = m_new
    return h


def _mlstm_kernel(q_ref, qt_ref, k_ref, kt_ref, v_ref, vt_ref, gc_ref, gct_ref, gr_ref, grt_ref,
                  o_ref, ot_ref, c_ref, n_ref, m_ref):
    c_ref[...] = jnp.zeros_like(c_ref)
    n_ref[...] = jnp.zeros_like(n_ref)
    m_ref[...] = jnp.zeros_like(m_ref)
    state = (c_ref, n_ref, m_ref)

    def real_step(dirn, c):
        r0 = pl.multiple_of(c * CHUNK, CHUNK)
        rows = pl.ds(r0, CHUNK)
        return _mlstm_step(dirn, q_ref[rows, :], k_ref[rows, :], v_ref[rows, :],
                           gc_ref[rows, :], gr_ref[:, rows], False, *state), rows

    def tail_step(dirn):
        return _mlstm_step(dirn, qt_ref[...], kt_ref[...], vt_ref[...],
                           gct_ref[...], grt_ref[...], True, *state)

    ot_ref[...] = tail_step(0)

    def first_half(i, carry):
        hf, rows_f = real_step(0, i)
        o_ref[rows_f, :] = hf
        hb, rows_b = real_step(1, N_CHUNK - 1 - i)
        o_ref[rows_b, :] = hb
        return carry

    def second_half(i, carry):
        hf, rows_f = real_step(0, i)
        o_ref[rows_f, :] += hf
        hb, rows_b = real_step(1, N_CHUNK - 1 - i)
        o_ref[rows_b, :] += hb
        return carry

    lax.fori_loop(0, N_CHUNK // 2, first_half, 0)
    lax.fori_loop(N_CHUNK // 2, N_CHUNK, second_half, 0)
    ot_ref[...] += tail_step(1)


def _mlstm(q, k, xm, gcol, grow):
    tail_blk = N_REAL // TAIL
    real = lambda w: pl.BlockSpec((SEQ, w), lambda b, h: (b, h))
    tail = lambda w: pl.BlockSpec((TAIL, w), lambda b, h: (tail_blk + b, h))
    return pl.pallas_call(
        _mlstm_kernel,
        grid=(BATCH, M_HEADS),
        in_specs=[real(M_DK), tail(M_DK), real(M_DK), tail(M_DK), real(M_DV), tail(M_DV),
                  pl.BlockSpec((None, SEQ, 4), lambda b, h: (h, b, 0)),
                  pl.BlockSpec((None, TAIL, 4), lambda b, h: (h, tail_blk + b, 0)),
                  pl.BlockSpec((None, 4, SEQ), lambda b, h: (h, 0, b)),
                  pl.BlockSpec((None, 4, TAIL), lambda b, h: (h, 0, tail_blk + b))],
        out_specs=[pl.BlockSpec((SEQ, M_DV), lambda b, h: (b, h)),
                   pl.BlockSpec((TAIL, M_DV), lambda b, h: (b, h))],
        out_shape=[jax.ShapeDtypeStruct((N_REAL, M_INNER), F32),
                   jax.ShapeDtypeStruct((BATCH * TAIL, M_INNER), F32)],
        scratch_shapes=[pltpu.VMEM((2, M_DK, M_DV), F32),
                        pltpu.VMEM((2, 1, M_DK), F32),
                        pltpu.VMEM((2, 1, 1), F32)],
        compiler_params=_params("parallel", "parallel"),
        name="mlstm",
    )(q, q, k, k, xm, xm, gcol, gcol, grow, grow)


PAIR = 2 * NA_DH


def _natten_kernel(q_ref, qt_ref, k_ref, kt_ref, v_ref, vt_ref, bias_ref, o_ref, ot_ref):
    lane = lax.broadcasted_iota(jnp.int32, (1, PAIR), 1)
    first = lane < NA_DH
    k_tail = kt_ref[...]
    v_tail = vt_ref[...]

    def attend(q, keys, vals, bias):
        outs = []
        for hd in range(2):
            qh = jnp.where(first if hd == 0 else ~first, q, jnp.zeros_like(q))
            s = _dot_nt(qh, keys) + bias(hd)
            e = jnp.exp(s - jnp.max(s, axis=1, keepdims=True))
            outs.append(_dot(e.astype(BF16), vals) / jnp.sum(e, axis=1, keepdims=True))
        return jnp.where(first, outs[0], outs[1])

    def row_block(r, carry):
        r0 = jnp.clip(r - NA_WIN_H // 2, 0, ROWS - NA_WIN_H)
        delta = r - r0
        qrows = pl.ds(pl.multiple_of(r * GRID_W, GRID_W), GRID_W)
        krows = pl.ds(pl.multiple_of(r0 * GRID_W, GRID_W), NA_WIN_H * GRID_W)
        keys = jnp.concatenate([k_ref[krows, :], k_tail], axis=0)
        vals = jnp.concatenate([v_ref[krows, :], v_tail], axis=0)
        o = attend(q_ref[qrows, :], keys, vals, lambda hd: bias_ref[hd, delta])
        o_ref[qrows, :] = o.astype(o_ref.dtype)
        return carry

    lax.fori_loop(0, ROWS, row_block, 0)
    tail_bias = lambda hd: bias_ref[hd, 0, 0:1, NA_WIN_H * GRID_W:]
    ot_ref[...] = attend(qt_ref[...], k_tail, v_tail, tail_bias).astype(ot_ref.dtype)


def _natten(qn, kn, vn, bias):
    tail_blk = N_REAL // TAIL
    real = pl.BlockSpec((SEQ, PAIR), lambda b, p: (b, p))
    tail = pl.BlockSpec((TAIL, PAIR), lambda b, p: (tail_blk + b, p))
    return pl.pallas_call(
        _natten_kernel,
        grid=(BATCH, NA_HEADS // 2),
        in_specs=[real, tail, real, tail, real, tail,
                  pl.BlockSpec((2, NA_WIN_H, GRID_W, NA_KEYS), lambda b, p: (p, 0, 0, 0))],
        out_specs=[pl.BlockSpec((SEQ, PAIR), lambda b, p: (b, p)),
                   pl.BlockSpec((TAIL, PAIR), lambda b, p: (b, p))],
        out_shape=[jax.ShapeDtypeStruct((N_REAL, NA_INNER), BF16),
                   jax.ShapeDtypeStruct((BATCH * TAIL, NA_INNER), BF16)],
        compiler_params=_params("parallel", "parallel"),
        name="natten",
    )(qn, qn, kn, kn, vn, vn, bias)


def _natten_bias(rpb, meta_bias):
    qc = jnp.arange(GRID_W)
    kc = jnp.arange(GRID_W)
    win0 = jnp.clip(qc - NA_WIN_W // 2, 0, GRID_W - NA_WIN_W)
    ok = (kc[None, :] >= win0[:, None]) & (kc[None, :] < win0[:, None] + NA_WIN_W)
    dc = jnp.clip(kc[None, :] - qc[:, None], -(NA_WIN_W - 1), NA_WIN_W - 1) + NA_WIN_W - 1
    delta = jnp.arange(NA_WIN_H)
    ki = jnp.arange(NA_WIN_H)
    dr = ki[None, :] - delta[:, None] + NA_WIN_H - 1
    loc = rpb.astype(F32)[:, dr[:, None, :, None], dc[None, :, None, :]]
    loc = jnp.where(ok[None, None, :, None, :], loc, MASKED)
    loc = loc.reshape(NA_HEADS, NA_WIN_H, GRID_W, NA_WIN_H * GRID_W)
    met = jnp.concatenate([jnp.full((NA_HEADS, NPAD), MASKED, F32), meta_bias.astype(F32)], axis=1)
    met = jnp.broadcast_to(met[:, None, None, :], (NA_HEADS, NA_WIN_H, GRID_W, TAIL))
    return jnp.concatenate([loc, met], axis=-1)


def _merge_kernel(hs_ref, xc_ref, og_ref, yb_ref, gab_ref, h_ref, ng_ref, sk_ref,
                  wa_ref, wb_ref, wo_ref, o_ref):
    hs = hs_ref[...]
    parts = []
    for hd in range(M_HEADS):
        sl = hs[:, hd * M_DV:(hd + 1) * M_DV]
        parts.append(sl * lax.rsqrt(jnp.mean(sl * sl, axis=-1, keepdims=True) + EPS))
    hn = jnp.concatenate(parts, axis=1) * ng_ref[...]
    y_a = _sigmoid(og_ref[...]) * (hn + sk_ref[...] * xc_ref[...])
    gab = gab_ref[...]
    mix = (_sigmoid(gab[:, :D_MODEL]) * _dot(y_a.astype(BF16), wa_ref[...])
           + _sigmoid(gab[:, D_MODEL:]) * _dot(yb_ref[...], wb_ref[...]))
    o_ref[...] = h_ref[...] + _dot(mix.astype(BF16), wo_ref[...])


def _merge(hs, xc, og, yb, gab, hp, ng, sk, wa, wb, wo):
    row = lambda w: pl.BlockSpec((TM, w), lambda i: (i, 0))
    return pl.pallas_call(
        _merge_kernel,
        grid=(N_TILES,),
        in_specs=[row(M_INNER), row(M_INNER), row(M_INNER), row(NA_INNER), row(2 * D_MODEL), row(D_MODEL)]
                 + [_const_spec(a.shape) for a in (ng, sk, wa, wb, wo)],
        out_specs=row(D_MODEL),
        out_shape=jax.ShapeDtypeStruct((NT, D_MODEL), F32),
        compiler_params=_params("parallel"),
        name="merge",
    )(hs, xc, og, yb, gab, hp, ng, sk, wa, wb, wo)


def _ffn_kernel(h_ref, g2_ref, w1_ref, w2_ref, o_ref):
    h = h_ref[...]
    xn = (h * lax.rsqrt(jnp.mean(h * h, axis=-1, keepdims=True) + EPS) * g2_ref[...]).astype(BF16)
    z = jnp.maximum(_dot(xn, w1_ref[...]), 0.0)
    o_ref[...] = h + _dot((z * z).astype(BF16), w2_ref[...])


def _ffn(h1, g2, w1, w2):
    row = pl.BlockSpec((TM, D_MODEL), lambda i: (i, 0))
    return pl.pallas_call(
        _ffn_kernel,
        grid=(N_TILES,),
        in_specs=[row, _const_spec(g2.shape), _const_spec(w1.shape), _const_spec(w2.shape)],
        out_specs=row,
        out_shape=jax.ShapeDtypeStruct((NT, D_MODEL), F32),
        compiler_params=_params("parallel"),
        name="ffn",
    )(h1, g2, w1, w2)


def kernel(x, meta_tokens, norm1_g, w_in, mlstm_conv_w, mlstm_conv_b, mlstm_wq, mlstm_wk, mlstm_gate_b, mlstm_norm_g, mlstm_skip, na_q_norm_g, na_k_norm_g, na_rpb, na_meta_bias, w_branch_a, w_branch_b, w_out, norm2_g, w_ff1, w_ff2):
    off_mo = M_INNER
    off_mg = off_mo + M_INNER
    off_q = off_mg + 4 * M_HEADS
    off_v = off_q + 2 * NA_INNER
    off_g = off_v + NA_INNER

    tail = jnp.concatenate([jnp.zeros((NPAD, D_MODEL), F32), meta_tokens.astype(F32)], axis=0)
    hp = jnp.concatenate([x.reshape(N_REAL, D_MODEL), jnp.tile(tail, (BATCH, 1))], axis=0)

    wb16 = w_in.astype(BF16)
    wg = jnp.pad(wb16[:, off_mg:off_q], ((0, 0), (0, 128 - 4 * M_HEADS)))
    bd = jnp.kron(jnp.eye(NA_HEADS, dtype=F32), jnp.full((NA_DH, NA_DH), 1.0 / NA_DH, F32)).astype(BF16)
    qg = jnp.tile(na_q_norm_g.astype(F32), NA_HEADS)[None, :] * (NA_DH ** -0.5)
    kg = jnp.tile(na_k_norm_g.astype(F32), NA_HEADS)[None, :]

    xm, og, gates, qn, kn, vn, gab = _inproj(
        hp, norm1_g.astype(F32)[None, :], wb16[:, :off_mo], wb16[:, off_mo:off_mg], wg,
        wb16[:, off_q:off_g], wb16[:, off_g:], mlstm_gate_b.astype(F32).reshape(1, 4 * M_HEADS), qg, kg, bd)

    wqk = jnp.concatenate([mlstm_wq, mlstm_wk], axis=-1).astype(BF16)
    xc, qm, km = _conv(xm, mlstm_conv_w.astype(F32).reshape(3, M_INNER),
                       mlstm_conv_b.astype(F32)[None, :], wqk)

    g3 = gates.reshape(NT, 4, M_HEADS)
    hs_real, hs_tail = _mlstm(qm, km, xm, jnp.transpose(g3, (2, 0, 1)), jnp.transpose(g3, (2, 1, 0)))

    yb_real, yb_tail = _natten(qn, kn, vn, _natten_bias(na_rpb, na_meta_bias))

    h1 = _merge(jnp.concatenate([hs_real, hs_tail], axis=0), xc, og,
                jnp.concatenate([yb_real, yb_tail], axis=0), gab, hp,
                mlstm_norm_g.astype(F32).reshape(1, M_INNER), mlstm_skip.astype(F32)[None, :],
                w_branch_a.astype(BF16), w_branch_b.astype(BF16), w_out.astype(BF16))
    out = _ffn(h1, norm2_g.astype(F32)[None, :], w_ff1.astype(BF16), w_ff2.astype(BF16))
    return out[:N_REAL].reshape(BATCH, SEQ, D_MODEL)
```

```python
import functools

import jax
import jax.numpy as jnp
from jax import lax
from jax.experimental import pallas as pl
from jax.experimental.pallas import tpu as pltpu

D_MODEL = 1024
BATCH = 4
SEQ = 4096
N_META = 16
GRID_W = 64
ROWS = SEQ // GRID_W
M_HEADS = 4
M_DV = 256
M_DK = 128
M_INNER = M_HEADS * M_DV
NA_HEADS = 8
NA_DH = 64
NA_INNER = NA_HEADS * NA_DH
NA_WIN_H = 8
NA_WIN_W = 16
D_FF = 4 * D_MODEL
EPS = 1e-6
NEG_LOG_GATE = -1e9
MASKED = -1e30

TAIL = 128
NPAD = TAIL - N_META
CHUNK = 128
N_REAL = BATCH * SEQ
NT = N_REAL + BATCH * TAIL
TM = 512
N_TILES = NT // TM
NA_KEYS = NA_WIN_H * GRID_W + TAIL

VMEM_LIMIT = 56 * 1024 * 1024

F32 = jnp.float32
BF16 = jnp.bfloat16


def _dot(a, b):
    return jnp.dot(a, b, preferred_element_type=F32)


def _dot_nt(a, b):
    return lax.dot_general(a, b, (((1,), (1,)), ((), ())), preferred_element_type=F32)


def _dot_tn(a, b):
    return lax.dot_general(a, b, (((0,), (0,)), ((), ())), preferred_element_type=F32)


def _sigmoid(z):
    return 1.0 / (1.0 + jnp.exp(-z))


def _log_sigmoid(z):
    return jnp.minimum(z, 0.0) - jnp.log1p(jnp.exp(-jnp.abs(z)))


def _const_spec(shape):
    nd = len(shape)
    return pl.BlockSpec(shape, lambda *_: (0,) * nd, pipeline_mode=pl.Buffered(1))


def _params(*sem):
    return pltpu.CompilerParams(dimension_semantics=sem, vmem_limit_bytes=VMEM_LIMIT)


def _inproj_kernel(h_ref, g1_ref, wx_ref, wo_ref, wg_ref, wqkv_ref, wgab_ref, gb_ref,
                   qg_ref, kg_ref, bd_ref,
                   xm_ref, og_ref, gates_ref, q_ref, k_ref, v_ref, gab_ref):
    h = h_ref[...]
    ms = jnp.mean(h * h, axis=-1, keepdims=True)
    xn = (h * lax.rsqrt(ms + EPS) * g1_ref[...]).astype(BF16)
    xm_ref[...] = _dot(xn, wx_ref[...])
    og_ref[...] = _dot(xn, wo_ref[...])
    gates_ref[...] = _dot(xn, wg_ref[...])[:, :4 * M_HEADS] + gb_ref[...]
    gab_ref[...] = _dot(xn, wgab_ref[...])
    qkv = _dot(xn, wqkv_ref[...])
    uq = qkv[:, :NA_INNER]
    uk = qkv[:, NA_INNER:2 * NA_INNER]
    msq = _dot((uq * uq).astype(BF16), bd_ref[...])
    msk = _dot((uk * uk).astype(BF16), bd_ref[...])
    q_ref[...] = (uq * lax.rsqrt(msq + EPS) * qg_ref[...]).astype(BF16)
    k_ref[...] = (uk * lax.rsqrt(msk + EPS) * kg_ref[...]).astype(BF16)
    v_ref[...] = qkv[:, 2 * NA_INNER:].astype(BF16)


def _inproj(hp, g1, wx, wo, wg, wqkv, wgab, gb, qg, kg, bd):
    row = lambda w: pl.BlockSpec((TM, w), lambda i: (i, 0))
    return pl.pallas_call(
        _inproj_kernel,
        grid=(N_TILES,),
        in_specs=[row(D_MODEL)] + [_const_spec(a.shape) for a in (g1, wx, wo, wg, wqkv, wgab, gb, qg, kg, bd)],
        out_specs=[row(M_INNER), row(M_INNER), row(4 * M_HEADS), row(NA_INNER), row(NA_INNER),
                   row(NA_INNER), row(2 * D_MODEL)],
        out_shape=[jax.ShapeDtypeStruct((NT, M_INNER), F32),
                   jax.ShapeDtypeStruct((NT, M_INNER), F32),
                   jax.ShapeDtypeStruct((NT, 4 * M_HEADS), F32),
                   jax.ShapeDtypeStruct((NT, NA_INNER), BF16),
                   jax.ShapeDtypeStruct((NT, NA_INNER), BF16),
                   jax.ShapeDtypeStruct((NT, NA_INNER), BF16),
                   jax.ShapeDtypeStruct((NT, 2 * D_MODEL), F32)],
        compiler_params=_params("parallel"),
        name="inproj",
    )(hp, g1, wx, wo, wg, wqkv, wgab, gb, qg, kg, bd)


CT = 128
N_CT = SEQ // CT
HALO = 8


def _conv_kernel(x_ref, prev_ref, next_ref, cw_ref, cb_ref, wqk_ref, xc_ref, q_ref, k_ref):
    x = x_ref[...]
    rows = lax.broadcasted_iota(jnp.int32, (CT, M_INNER), 0)
    x_m1 = jnp.where(rows == 0, prev_ref[HALO - 1:HALO, :], pltpu.roll(x, 1, axis=0))
    x_p1 = jnp.where(rows == CT - 1, next_ref[0:1, :], pltpu.roll(x, CT - 1, axis=0))
    z = cw_ref[0:1, :] * x_m1 + cw_ref[1:2, :] * x + cw_ref[2:3, :] * x_p1 + cb_ref[...]
    xc = z * _sigmoid(z)
    xc_ref[...] = xc
    for hd in range(M_HEADS):
        qk = _dot(xc[:, hd * M_DV:(hd + 1) * M_DV].astype(BF16), wqk_ref[hd])
        q_ref[:, hd * M_DK:(hd + 1) * M_DK] = qk[:, :M_DK].astype(BF16)
        k_ref[:, hd * M_DK:(hd + 1) * M_DK] = (qk[:, M_DK:] * (M_DK ** -0.5)).astype(BF16)


def _conv(xm, cw, cb, wqk):
    real_blocks = SEQ // HALO
    tail_blocks = TAIL // HALO
    tail0 = N_REAL // HALO

    def main_map(b, j):
        return (jnp.where(j < N_CT, b * N_CT + j, N_REAL // CT + b), 0)

    def prev_map(b, j):
        inner = b * real_blocks + j * (CT // HALO) - 1
        first = tail0 + b * tail_blocks + tail_blocks - 1
        tail = b * real_blocks + real_blocks - 1
        return (jnp.where(j == 0, first, jnp.where(j == N_CT, tail, inner)), 0)

    def next_map(b, j):
        inner = b * real_blocks + (j + 1) * (CT // HALO)
        last = tail0 + b * tail_blocks
        tail = b * real_blocks
        return (jnp.where(j == N_CT - 1, last, jnp.where(j == N_CT, tail, inner)), 0)

    return pl.pallas_call(
        _conv_kernel,
        grid=(BATCH, N_CT + 1),
        in_specs=[pl.BlockSpec((CT, M_INNER), main_map),
                  pl.BlockSpec((HALO, M_INNER), prev_map),
                  pl.BlockSpec((HALO, M_INNER), next_map),
                  _const_spec(cw.shape), _const_spec(cb.shape), _const_spec(wqk.shape)],
        out_specs=[pl.BlockSpec((CT, M_INNER), main_map),
                   pl.BlockSpec((CT, M_HEADS * M_DK), main_map),
                   pl.BlockSpec((CT, M_HEADS * M_DK), main_map)],
        out_shape=[jax.ShapeDtypeStruct((NT, M_INNER), F32),
                   jax.ShapeDtypeStruct((NT, M_HEADS * M_DK), BF16),
                   jax.ShapeDtypeStruct((NT, M_HEADS * M_DK), BF16)],
        compiler_params=_params("parallel", "parallel"),
        name="conv_qk",
    )(xm, xm, xm, cw, cb, wqk)


N_CHUNK = SEQ // CHUNK


def _mlstm_step(dirn, q, k, v, gcol, grow, is_tail, c_ref, n_ref, m_ref):
    t = CHUNK
    li_c = gcol[:, 2 * dirn:2 * dirn + 1]
    lf_c = _log_sigmoid(gcol[:, 2 * dirn + 1:2 * dirn + 2])
    li_r = grow[2 * dirn:2 * dirn + 1, :]
    lf_r = _log_sigmoid(grow[2 * dirn + 1:2 * dirn + 2, :])
    if is_tail:
        pad_c = lax.broadcasted_iota(jnp.int32, (t, 1), 0) < NPAD
        pad_r = lax.broadcasted_iota(jnp.int32, (1, t), 1) < NPAD
        li_c = jnp.where(pad_c, NEG_LOG_GATE, li_c)
        lf_c = jnp.where(pad_c, 0.0, lf_c)
        li_r = jnp.where(pad_r, NEG_LOG_GATE, li_r)
        lf_r = jnp.where(pad_r, 0.0, lf_r)
    ri = lax.broadcasted_iota(jnp.int32, (t, t), 0)
    ci = lax.broadcasted_iota(jnp.int32, (t, t), 1)
    tri = (ci <= ri) if dirn == 0 else (ci >= ri)
    tri_t = (ri <= ci) if dirn == 0 else (ri >= ci)
    b_c = jnp.sum(jnp.where(tri, lf_r, 0.0), axis=1, keepdims=True)
    b_r = jnp.sum(jnp.where(tri_t, lf_c, 0.0), axis=0, keepdims=True)
    d = jnp.where(tri, b_c - b_r + li_r, MASKED)
    m_prev = m_ref[dirn]
    m_inter = b_c + m_prev
    m_t = jnp.maximum(m_inter, jnp.max(d, axis=1, keepdims=True))
    w_inter = jnp.exp(m_inter - m_t)
    p = _dot_nt(q, k) * jnp.exp(d - m_t)
    c_old = c_ref[dirn]
    n_old = n_ref[dirn]
    vb = v.astype(BF16)
    num = w_inter * _dot(q, c_old.astype(BF16)) + _dot(p.astype(BF16), vb)
    den = (w_inter * jnp.sum(q.astype(F32) * n_old, axis=1, keepdims=True)
           + jnp.sum(p, axis=1, keepdims=True))
    h = num / jnp.maximum(jnp.abs(den), jnp.exp(-m_t))
    b_end = b_c[t - 1:t, :] if dirn == 0 else b_c[0:1, :]
    a_c = b_end - b_c + li_c
    m_new = jnp.maximum(b_end + m_prev, jnp.max(a_c, axis=0, keepdims=True))
    decay = jnp.exp(b_end + m_prev - m_new)
    kw = k.astype(F32) * jnp.exp(a_c - m_new)
    c_ref[dirn] = decay * c_old + _dot_tn(kw.astype(BF16), vb)
    n_ref[dirn] = decay * n_old + jnp.sum(kw, axis=0, keepdims=True)
    m_ref[dirn] = m_new
    return h


def _mlstm_kernel(q_ref, qt_ref, k_ref, kt_ref, v_ref, vt_ref, gc_ref, gct_ref, gr_ref, grt_ref,
                  o_ref, ot_ref, c_ref, n_ref, m_ref):
    c_ref[...] = jnp.zeros_like(c_ref)
    n_ref[...] = jnp.zeros_like(n_ref)
    m_ref[...] = jnp.zeros_like(m_ref)
    state = (c_ref, n_ref, m_ref)

    def real_step(dirn, c):
        r0 = pl.multiple_of(c * CHUNK, CHUNK)
        rows = pl.ds(r0, CHUNK)
        return _mlstm_step(dirn, q_ref[rows, :], k_ref[rows, :], v_ref[rows, :],
                           gc_ref[rows, :], gr_ref[:, rows], False, *state), rows

    def tail_step(dirn):
        return _mlstm_step(dirn, qt_ref[...], kt_ref[...], vt_ref[...],
                           gct_ref[...], grt_ref[...], True, *state)

    ot_ref[...] = tail_step(0)

    def first_half(i, carry):
        hf, rows_f = real_step(0, i)
        o_ref[rows_f, :] = hf
        hb, rows_b = real_step(1, N_CHUNK - 1 - i)
        o_ref[rows_b, :] = hb
        return carry

    def second_half(i, carry):
        hf, rows_f = real_step(0, i)
        o_ref[rows_f, :] += hf
        hb, rows_b = real_step(1, N_CHUNK - 1 - i)
        o_ref[rows_b, :] += hb
        return carry

    lax.fori_loop(0, N_CHUNK // 2, first_half, 0)
    lax.fori_loop(N_CHUNK // 2, N_CHUNK, second_half, 0)
    ot_ref[...] += tail_step(1)


def _mlstm(q, k, xm, gcol, grow):
    tail_blk = N_REAL // TAIL
    real = lambda w: pl.BlockSpec((SEQ, w), lambda b, h: (b, h))
    tail = lambda w: pl.BlockSpec((TAIL, w), lambda b, h: (tail_blk + b, h))
    return pl.pallas_call(
        _mlstm_kernel,
        grid=(BATCH, M_HEADS),
        in_specs=[real(M_DK), tail(M_DK), real(M_DK), tail(M_DK), real(M_DV), tail(M_DV),
                  pl.BlockSpec((None, SEQ, 4), lambda b, h: (h, b, 0)),
                  pl.BlockSpec((None, TAIL, 4), lambda b, h: (h, tail_blk + b, 0)),
                  pl.BlockSpec((None, 4, SEQ), lambda b, h: (h, 0, b)),
                  pl.BlockSpec((None, 4, TAIL), lambda b, h: (h, 0, tail_blk + b))],
        out_specs=[pl.BlockSpec((SEQ, M_DV), lambda b, h: (b, h)),
                   pl.BlockSpec((TAIL, M_DV), lambda b, h: (b, h))],
        out_shape=[jax.ShapeDtypeStruct((N_REAL, M_INNER), F32),
                   jax.ShapeDtypeStruct((BATCH * TAIL, M_INNER), F32)],
        scratch_shapes=[pltpu.VMEM((2, M_DK, M_DV), F32),
                        pltpu.VMEM((2, 1, M_DK), F32),
                        pltpu.VMEM((2, 1, 1), F32)],
        compiler_params=_params("parallel", "parallel"),
        name="mlstm",
    )(q, q, k, k, xm, xm, gcol, gcol, grow, grow)


PAIR = 2 * NA_DH


def _natten_kernel(q_ref, qt_ref, k_ref, kt_ref, v_ref, vt_ref, bias_ref, o_ref, ot_ref):
    lane = lax.broadcasted_iota(jnp.int32, (1, PAIR), 1)
    first = lane < NA_DH
    k_tail = kt_ref[...]
    v_tail = vt_ref[...]

    def attend(q, keys, vals, bias_t):
        n = q.shape[0]
        zero = jnp.zeros_like(q)
        qs = jnp.concatenate([jnp.where(first, q, zero), jnp.where(first, zero, q)], axis=0)
        s = _dot_nt(keys, qs) + bias_t
        e = jnp.exp(s - jnp.max(s, axis=0, keepdims=True))
        p = e * (1.0 / jnp.sum(e, axis=0, keepdims=True))
        o = _dot(jnp.transpose(p.astype(BF16)), vals)
        return jnp.where(first, o[:n], o[n:])

    def row_block(r, carry):
        r0 = jnp.clip(r - NA_WIN_H // 2, 0, ROWS - NA_WIN_H)
        qrows = pl.ds(pl.multiple_of(r * GRID_W, GRID_W), GRID_W)
        krows = pl.ds(pl.multiple_of(r0 * GRID_W, GRID_W), NA_WIN_H * GRID_W)
        keys = jnp.concatenate([k_ref[krows, :], k_tail], axis=0)
        vals = jnp.concatenate([v_ref[krows, :], v_tail], axis=0)
        o = attend(q_ref[qrows, :], keys, vals, bias_ref[r - r0])
        o_ref[qrows, :] = o.astype(o_ref.dtype)
        return carry

    lax.fori_loop(0, ROWS, row_block, 0, unroll=8)
    tb = bias_ref[0, NA_WIN_H * GRID_W:, :]
    tail_bias = jnp.concatenate([jnp.broadcast_to(tb[:, 0:1], (TAIL, TAIL)),
                                 jnp.broadcast_to(tb[:, NA_DH:NA_DH + 1], (TAIL, TAIL))], axis=1)
    ot_ref[...] = attend(qt_ref[...], k_tail, v_tail, tail_bias).astype(ot_ref.dtype)


def _natten(qn, kn, vn, bias):
    tail_blk = N_REAL // TAIL
    real = pl.BlockSpec((SEQ, PAIR), lambda b, p: (b, p))
    tail = pl.BlockSpec((TAIL, PAIR), lambda b, p: (tail_blk + b, p))
    return pl.pallas_call(
        _natten_kernel,
        grid=(BATCH, NA_HEADS // 2),
        in_specs=[real, tail, real, tail, real, tail,
                  pl.BlockSpec((None, NA_WIN_H, NA_KEYS, PAIR), lambda b, p: (p, 0, 0, 0))],
        out_specs=[pl.BlockSpec((SEQ, PAIR), lambda b, p: (b, p)),
                   pl.BlockSpec((TAIL, PAIR), lambda b, p: (b, p))],
        out_shape=[jax.ShapeDtypeStruct((N_REAL, NA_INNER), BF16),
                   jax.ShapeDtypeStruct((BATCH * TAIL, NA_INNER), BF16)],
        compiler_params=_params("parallel", "parallel"),
        name="natten",
    )(qn, qn, kn, kn, vn, vn, bias)


def _natten_bias(rpb, meta_bias):
    qc = jnp.arange(GRID_W)
    kc = jnp.arange(GRID_W)
    win0 = jnp.clip(qc - NA_WIN_W // 2, 0, GRID_W - NA_WIN_W)
    ok = (kc[None, :] >= win0[:, None]) & (kc[None, :] < win0[:, None] + NA_WIN_W)
    dc = jnp.clip(kc[None, :] - qc[:, None], -(NA_WIN_W - 1), NA_WIN_W - 1) + NA_WIN_W - 1
    onehot = (dc[None] == jnp.arange(2 * NA_WIN_W - 1)[:, None, None]).astype(F32)
    t1 = jnp.einsum('hdj,jqk->hdqk', rpb.astype(F32), onehot, precision=lax.Precision.HIGHEST)
    t1 = jnp.where(ok[None, None], t1, MASKED)
    loc = jnp.stack([t1[:, NA_WIN_H - 1 - dl:2 * NA_WIN_H - 1 - dl] for dl in range(NA_WIN_H)], axis=1)
    loc = jnp.transpose(loc, (0, 1, 3, 2, 4))
    loc = loc.reshape(NA_HEADS, NA_WIN_H, GRID_W, NA_WIN_H * GRID_W)
    met = jnp.concatenate([jnp.full((NA_HEADS, NPAD), MASKED, F32), meta_bias.astype(F32)], axis=1)
    met = jnp.broadcast_to(met[:, None, None, :], (NA_HEADS, NA_WIN_H, GRID_W, TAIL))
    bias = jnp.concatenate([loc, met], axis=-1)
    bias = bias.reshape(NA_HEADS // 2, 2, NA_WIN_H, GRID_W, NA_KEYS)
    return jnp.transpose(bias, (0, 2, 4, 1, 3)).reshape(NA_HEADS // 2, NA_WIN_H, NA_KEYS, PAIR)


def _merge_kernel(hs_ref, xc_ref, og_ref, yb_ref, gab_ref, h_ref, ng_ref, sk_ref,
                  wa_ref, wb_ref, wo_ref, o_ref):
    hs = hs_ref[...]
    parts = []
    for hd in range(M_HEADS):
        sl = hs[:, hd * M_DV:(hd + 1) * M_DV]
        parts.append(sl * lax.rsqrt(jnp.mean(sl * sl, axis=-1, keepdims=True) + EPS))
    hn = jnp.concatenate(parts, axis=1) * ng_ref[...]
    y_a = _sigmoid(og_ref[...]) * (hn + sk_ref[...] * xc_ref[...])
    gab = gab_ref[...]
    mix = (_sigmoid(gab[:, :D_MODEL]) * _dot(y_a.astype(BF16), wa_ref[...])
           + _sigmoid(gab[:, D_MODEL:]) * _dot(yb_ref[...], wb_ref[...]))
    o_ref[...] = h_ref[...] + _dot(mix.astype(BF16), wo_ref[...])


def _merge(hs, xc, og, yb, gab, hp, ng, sk, wa, wb, wo):
    row = lambda w: pl.BlockSpec((TM, w), lambda i: (i, 0))
    return pl.pallas_call(
        _merge_kernel,
        grid=(N_TILES,),
        in_specs=[row(M_INNER), row(M_INNER), row(M_INNER), row(NA_INNER), row(2 * D_MODEL), row(D_MODEL)]
                 + [_const_spec(a.shape) for a in (ng, sk, wa, wb, wo)],
        out_specs=row(D_MODEL),
        out_shape=jax.ShapeDtypeStruct((NT, D_MODEL), F32),
        compiler_params=_params("parallel"),
        name="merge",
    )(hs, xc, og, yb, gab, hp, ng, sk, wa, wb, wo)


def _ffn_kernel(h_ref, g2_ref, w1_ref, w2_ref, o_ref):
    h = h_ref[...]
    xn = (h * lax.rsqrt(jnp.mean(h * h, axis=-1, keepdims=True) + EPS) * g2_ref[...]).astype(BF16)
    z = jnp.maximum(_dot(xn, w1_ref[...]), 0.0)
    o_ref[...] = h + _dot((z * z).astype(BF16), w2_ref[...])


def _ffn(h1, g2, w1, w2):
    row = pl.BlockSpec((TM, D_MODEL), lambda i: (i, 0))
    return pl.pallas_call(
        _ffn_kernel,
        grid=(N_TILES,),
        in_specs=[row, _const_spec(g2.shape), _const_spec(w1.shape), _const_spec(w2.shape)],
        out_specs=row,
        out_shape=jax.ShapeDtypeStruct((NT, D_MODEL), F32),
        compiler_params=_params("parallel"),
        name="ffn",
    )(h1, g2, w1, w2)


def kernel(x, meta_tokens, norm1_g, w_in, mlstm_conv_w, mlstm_conv_b, mlstm_wq, mlstm_wk, mlstm_gate_b, mlstm_norm_g, mlstm_skip, na_q_norm_g, na_k_norm_g, na_rpb, na_meta_bias, w_branch_a, w_branch_b, w_out, norm2_g, w_ff1, w_ff2):
    off_mo = M_INNER
    off_mg = off_mo + M_INNER
    off_q = off_mg + 4 * M_HEADS
    off_v = off_q + 2 * NA_INNER
    off_g = off_v + NA_INNER

    tail = jnp.concatenate([jnp.zeros((NPAD, D_MODEL), F32), meta_tokens.astype(F32)], axis=0)
    hp = jnp.concatenate([x.reshape(N_REAL, D_MODEL), jnp.tile(tail, (BATCH, 1))], axis=0)

    wb16 = w_in.astype(BF16)
    wg = jnp.pad(wb16[:, off_mg:off_q], ((0, 0), (0, 128 - 4 * M_HEADS)))
    bd = jnp.kron(jnp.eye(NA_HEADS, dtype=F32), jnp.full((NA_DH, NA_DH), 1.0 / NA_DH, F32)).astype(BF16)
    qg = jnp.tile(na_q_norm_g.astype(F32), NA_HEADS)[None, :] * (NA_DH ** -0.5)
    kg = jnp.tile(na_k_norm_g.astype(F32), NA_HEADS)[None, :]

    xm, og, gates, qn, kn, vn, gab = _inproj(
        hp, norm1_g.astype(F32)[None, :], wb16[:, :off_mo], wb16[:, off_mo:off_mg], wg,
        wb16[:, off_q:off_g], wb16[:, off_g:], mlstm_gate_b.astype(F32).reshape(1, 4 * M_HEADS), qg, kg, bd)

    wqk = jnp.concatenate([mlstm_wq, mlstm_wk], axis=-1).astype(BF16)
    xc, qm, km = _conv(xm, mlstm_conv_w.astype(F32).reshape(3, M_INNER),
                       mlstm_conv_b.astype(F32)[None, :], wqk)

    g3 = gates.reshape(NT, 4, M_HEADS)
    hs_real, hs_tail = _mlstm(qm, km, xm, jnp.transpose(g3, (2, 0, 1)), jnp.transpose(g3, (2, 1, 0)))

    yb_real, yb_tail = _natten(qn, kn, vn, _natten_bias(na_rpb, na_meta_bias))

    h1 = _merge(jnp.concatenate([hs_real, hs_tail], axis=0), xc, og,
                jnp.concatenate([yb_real, yb_tail], axis=0), gab, hp,
                mlstm_norm_g.astype(F32).reshape(1, M_INNER), mlstm_skip.astype(F32)[None, :],
                w_branch_a.astype(BF16), w_branch_b.astype(BF16), w_out.astype(BF16))
    out = _ffn(h1, norm2_g.astype(F32)[None, :], w_ff1.astype(BF16), w_ff2.astype(BF16))
    return out[:N_REAL].reshape(BATCH, SEQ, D_MODEL)
```

```python
import functools

import jax
import jax.numpy as jnp
from jax import lax
from jax.experimental import pallas as pl
from jax.experimental.pallas import tpu as pltpu

D_MODEL = 1024
BATCH = 4
SEQ = 4096
N_META = 16
GRID_W = 64
ROWS = SEQ // GRID_W
M_HEADS = 4
M_DV = 256
M_DK = 128
M_INNER = M_HEADS * M_DV
NA_HEADS = 8
NA_DH = 64
NA_INNER = NA_HEADS * NA_DH
NA_WIN_H = 8
NA_WIN_W = 16
D_FF = 4 * D_MODEL
EPS = 1e-6
NEG_LOG_GATE = -1e9
MASKED = -1e30

TAIL = 128
NPAD = TAIL - N_META
CHUNK = 128
N_REAL = BATCH * SEQ
NT = N_REAL + BATCH * TAIL
TM = 512
N_TILES = NT // TM
N_REAL_TILES = N_REAL // TM
NA_KEYS = NA_WIN_H * GRID_W + TAIL

VMEM_LIMIT = 56 * 1024 * 1024

F32 = jnp.float32
BF16 = jnp.bfloat16


def _dot(a, b):
    return jnp.dot(a, b, preferred_element_type=F32)


def _dot_nt(a, b):
    return lax.dot_general(a, b, (((1,), (1,)), ((), ())), preferred_element_type=F32)


def _dot_tn(a, b):
    return lax.dot_general(a, b, (((0,), (0,)), ((), ())), preferred_element_type=F32)


def _sigmoid(z):
    return 1.0 / (1.0 + jnp.exp(-z))


def _log_sigmoid(z):
    return jnp.minimum(z, 0.0) - jnp.log1p(jnp.exp(-jnp.abs(z)))


def _const_spec(shape):
    nd = len(shape)
    return pl.BlockSpec(shape, lambda *_: (0,) * nd, pipeline_mode=pl.Buffered(1))


def _params(*sem):
    return pltpu.CompilerParams(dimension_semantics=sem, vmem_limit_bytes=VMEM_LIMIT)


GATE_PAD = 8


def _inproj_kernel(x_ref, tail_ref, g1_ref, wx_ref, wo_ref, wg_ref, wqkv_ref, wgab_ref, gb_ref,
                   qg_ref, kg_ref, bd_ref,
                   xm_ref, og_ref, gcol_ref, grow_ref, q_ref, k_ref, v_ref, gab_ref):
    def body(h_ref):
        h = h_ref[...]
        ms = jnp.mean(h * h, axis=-1, keepdims=True)
        xn = (h * lax.rsqrt(ms + EPS) * g1_ref[...]).astype(BF16)
        xm_ref[...] = _dot(xn, wx_ref[...])
        og_ref[...] = _dot(xn, wo_ref[...]).astype(BF16)
        gates = _dot(xn, wg_ref[...]) + gb_ref[...]
        for hd in range(M_HEADS):
            gcol_ref[hd] = gates[:, hd * GATE_PAD:(hd + 1) * GATE_PAD]
        grow_ref[...] = jnp.transpose(gates)[:M_HEADS * GATE_PAD, :]
        gab_ref[...] = _dot(xn, wgab_ref[...]).astype(BF16)
        qkv = _dot(xn, wqkv_ref[...])
        uq = qkv[:, :NA_INNER]
        uk = qkv[:, NA_INNER:2 * NA_INNER]
        msq = _dot((uq * uq).astype(BF16), bd_ref[...])
        msk = _dot((uk * uk).astype(BF16), bd_ref[...])
        q_ref[...] = (uq * lax.rsqrt(msq + EPS) * qg_ref[...]).astype(BF16)
        k_ref[...] = (uk * lax.rsqrt(msk + EPS) * kg_ref[...]).astype(BF16)
        v_ref[...] = qkv[:, 2 * NA_INNER:].astype(BF16)

    _real_or_tail(body, x_ref, tail_ref)


def _real_or_tail(body, *ref_pairs):
    i = pl.program_id(0)
    pl.when(i < N_REAL_TILES)(lambda: body(*ref_pairs[0::2]))
    pl.when(i == N_REAL_TILES)(lambda: body(*ref_pairs[1::2]))


def _real_spec(width):
    return pl.BlockSpec((TM, width), lambda i: (jnp.minimum(i, N_REAL_TILES - 1), 0))


def _tail_spec(width):
    return pl.BlockSpec((TM, width), lambda i: (0, 0))


def _row_spec(width):
    return pl.BlockSpec((TM, width), lambda i: (i, 0))


def _inproj(x2d, tailh, g1, wx, wo, wg, wqkv, wgab, gb, qg, kg, bd):
    return pl.pallas_call(
        _inproj_kernel,
        grid=(N_TILES,),
        in_specs=[_real_spec(D_MODEL), _tail_spec(D_MODEL)]
                 + [_const_spec(a.shape) for a in (g1, wx, wo, wg, wqkv, wgab, gb, qg, kg, bd)],
        out_specs=[_row_spec(M_INNER), _row_spec(M_INNER),
                   pl.BlockSpec((M_HEADS, TM, GATE_PAD), lambda i: (0, i, 0)),
                   pl.BlockSpec((M_HEADS * GATE_PAD, TM), lambda i: (0, i)),
                   _row_spec(NA_INNER), _row_spec(NA_INNER), _row_spec(NA_INNER), _row_spec(2 * D_MODEL)],
        out_shape=[jax.ShapeDtypeStruct((NT, M_INNER), F32),
                   jax.ShapeDtypeStruct((NT, M_INNER), BF16),
                   jax.ShapeDtypeStruct((M_HEADS, NT, GATE_PAD), F32),
                   jax.ShapeDtypeStruct((M_HEADS * GATE_PAD, NT), F32),
                   jax.ShapeDtypeStruct((NT, NA_INNER), BF16),
                   jax.ShapeDtypeStruct((NT, NA_INNER), BF16),
                   jax.ShapeDtypeStruct((NT, NA_INNER), BF16),
                   jax.ShapeDtypeStruct((NT, 2 * D_MODEL), BF16)],
        compiler_params=_params("parallel"),
        name="inproj",
    )(x2d, tailh, g1, wx, wo, wg, wqkv, wgab, gb, qg, kg, bd)


CT = 128
N_CT = SEQ // CT
HALO = 8


def _conv_kernel(x_ref, prev_ref, next_ref, cw_ref, cb_ref, wqk_ref, xc_ref, q_ref, k_ref):
    x = x_ref[...]
    rows = lax.broadcasted_iota(jnp.int32, (CT, M_INNER), 0)
    x_m1 = jnp.where(rows == 0, prev_ref[HALO - 1:HALO, :], pltpu.roll(x, 1, axis=0))
    x_p1 = jnp.where(rows == CT - 1, next_ref[0:1, :], pltpu.roll(x, CT - 1, axis=0))
    z = cw_ref[0:1, :] * x_m1 + cw_ref[1:2, :] * x + cw_ref[2:3, :] * x_p1 + cb_ref[...]
    xc = z * _sigmoid(z)
    xc_ref[...] = xc.astype(xc_ref.dtype)
    for hd in range(M_HEADS):
        qk = _dot(xc[:, hd * M_DV:(hd + 1) * M_DV].astype(BF16), wqk_ref[hd])
        q_ref[:, hd * M_DK:(hd + 1) * M_DK] = qk[:, :M_DK].astype(BF16)
        k_ref[:, hd * M_DK:(hd + 1) * M_DK] = (qk[:, M_DK:] * (M_DK ** -0.5)).astype(BF16)


def _conv(xm, cw, cb, wqk):
    real_blocks = SEQ // HALO
    tail_blocks = TAIL // HALO
    tail0 = N_REAL // HALO

    def main_map(b, j):
        return (jnp.where(j < N_CT, b * N_CT + j, N_REAL // CT + b), 0)

    def prev_map(b, j):
        inner = b * real_blocks + j * (CT // HALO) - 1
        first = tail0 + b * tail_blocks + tail_blocks - 1
        tail = b * real_blocks + real_blocks - 1
        return (jnp.where(j == 0, first, jnp.where(j == N_CT, tail, inner)), 0)

    def next_map(b, j):
        inner = b * real_blocks + (j + 1) * (CT // HALO)
        last = tail0 + b * tail_blocks
        tail = b * real_blocks
        return (jnp.where(j == N_CT - 1, last, jnp.where(j == N_CT, tail, inner)), 0)

    return pl.pallas_call(
        _conv_kernel,
        grid=(BATCH, N_CT + 1),
        in_specs=[pl.BlockSpec((CT, M_INNER), main_map),
                  pl.BlockSpec((HALO, M_INNER), prev_map),
                  pl.BlockSpec((HALO, M_INNER), next_map),
                  _const_spec(cw.shape), _const_spec(cb.shape), _const_spec(wqk.shape)],
        out_specs=[pl.BlockSpec((CT, M_INNER), main_map),
                   pl.BlockSpec((CT, M_HEADS * M_DK), main_map),
                   pl.BlockSpec((CT, M_HEADS * M_DK), main_map)],
        out_shape=[jax.ShapeDtypeStruct((NT, M_INNER), BF16),
                   jax.ShapeDtypeStruct((NT, M_HEADS * M_DK), BF16),
                   jax.ShapeDtypeStruct((NT, M_HEADS * M_DK), BF16)],
        compiler_params=_params("parallel", "parallel"),
        name="conv_qk",
    )(xm, xm, xm, cw, cb, wqk)


N_CHUNK = SEQ // CHUNK


def _mlstm_step(dirn, q, k, v, gcol, grow, is_tail, c_ref, n_ref, m_ref):
    t = CHUNK
    li_c = gcol[:, 2 * dirn:2 * dirn + 1]
    lf_c = _log_sigmoid(gcol[:, 2 * dirn + 1:2 * dirn + 2])
    li_r = grow[2 * dirn:2 * dirn + 1, :]
    lf_r = _log_sigmoid(grow[2 * dirn + 1:2 * dirn + 2, :])
    if is_tail:
        pad_c = lax.broadcasted_iota(jnp.int32, (t, 1), 0) < NPAD
        pad_r = lax.broadcasted_iota(jnp.int32, (1, t), 1) < NPAD
        li_c = jnp.where(pad_c, NEG_LOG_GATE, li_c)
        lf_c = jnp.where(pad_c, 0.0, lf_c)
        li_r = jnp.where(pad_r, NEG_LOG_GATE, li_r)
        lf_r = jnp.where(pad_r, 0.0, lf_r)
    ri = lax.broadcasted_iota(jnp.int32, (t, t), 0)
    ci = lax.broadcasted_iota(jnp.int32, (t, t), 1)
    tri = (ci <= ri) if dirn == 0 else (ci >= ri)
    tri_t = (ri <= ci) if dirn == 0 else (ri >= ci)
    b_c = jnp.sum(jnp.where(tri, lf_r, 0.0), axis=1, keepdims=True)
    b_r = jnp.sum(jnp.where(tri_t, lf_c, 0.0), axis=0, keepdims=True)
    d = jnp.where(tri, b_c - b_r + li_r, MASKED)
    m_prev = m_ref[dirn]
    m_inter = b_c + m_prev
    m_t = jnp.maximum(m_inter, jnp.max(d, axis=1, keepdims=True))
    w_inter = jnp.exp(m_inter - m_t)
    p = _dot_nt(q, k) * jnp.exp(d - m_t)
    c_old = c_ref[dirn]
    n_old = n_ref[dirn]
    vb = v.astype(BF16)
    num = w_inter * _dot(q, c_old.astype(BF16)) + _dot(p.astype(BF16), vb)
    den = (w_inter * jnp.sum(q.astype(F32) * n_old, axis=1, keepdims=True)
           + jnp.sum(p, axis=1, keepdims=True))
    h = num / jnp.maximum(jnp.abs(den), jnp.exp(-m_t))
    b_end = b_c[t - 1:t, :] if dirn == 0 else b_c[0:1, :]
    a_c = b_end - b_c + li_c
    m_new = jnp.maximum(b_end + m_prev, jnp.max(a_c, axis=0, keepdims=True))
    decay = jnp.exp(b_end + m_prev - m_new)
    kw = k.astype(F32) * jnp.exp(a_c - m_new)
    c_ref[dirn] = decay * c_old + _dot_tn(kw.astype(BF16), vb)
    n_ref[dirn] = decay * n_old + jnp.sum(kw, axis=0, keepdims=True)
    m_ref[dirn] = m_new
    return h


def _mlstm_kernel(q_ref, qt_ref, k_ref, kt_ref, v_ref, vt_ref, gc_ref, gct_ref, gr_ref, grt_ref,
                  o_ref, ot_ref, c_ref, n_ref, m_ref):
    c_ref[...] = jnp.zeros_like(c_ref)
    n_ref[...] = jnp.zeros_like(n_ref)
    m_ref[...] = jnp.zeros_like(m_ref)
    state = (c_ref, n_ref, m_ref)

    def real_step(dirn, c):
        r0 = pl.multiple_of(c * CHUNK, CHUNK)
        rows = pl.ds(r0, CHUNK)
        return _mlstm_step(dirn, q_ref[rows, :], k_ref[rows, :], v_ref[rows, :],
                           gc_ref[rows, :], gr_ref[:, rows], False, *state), rows

    def tail_step(dirn):
        return _mlstm_step(dirn, qt_ref[...], kt_ref[...], vt_ref[...],
                           gct_ref[...], grt_ref[...], True, *state)

    ot_ref[...] = tail_step(0)

    def first_half(i, carry):
        hf, rows_f = real_step(0, i)
        o_ref[rows_f, :] = hf
        hb, rows_b = real_step(1, N_CHUNK - 1 - i)
        o_ref[rows_b, :] = hb
        return carry

    def second_half(i, carry):
        hf, rows_f = real_step(0, i)
        o_ref[rows_f, :] += hf
        hb, rows_b = real_step(1, N_CHUNK - 1 - i)
        o_ref[rows_b, :] += hb
        return carry

    lax.fori_loop(0, N_CHUNK // 2, first_half, 0)
    lax.fori_loop(N_CHUNK // 2, N_CHUNK, second_half, 0)
    ot_ref[...] += tail_step(1)


def _mlstm(q, k, xm, gcol, grow):
    tail_blk = N_REAL // TAIL
    real = lambda w: pl.BlockSpec((SEQ, w), lambda b, h: (b, h))
    tail = lambda w: pl.BlockSpec((TAIL, w), lambda b, h: (tail_blk + b, h))
    return pl.pallas_call(
        _mlstm_kernel,
        grid=(BATCH, M_HEADS),
        in_specs=[real(M_DK), tail(M_DK), real(M_DK), tail(M_DK), real(M_DV), tail(M_DV),
                  pl.BlockSpec((None, SEQ, GATE_PAD), lambda b, h: (h, b, 0)),
                  pl.BlockSpec((None, TAIL, GATE_PAD), lambda b, h: (h, tail_blk + b, 0)),
                  pl.BlockSpec((GATE_PAD, SEQ), lambda b, h: (h, b)),
                  pl.BlockSpec((GATE_PAD, TAIL), lambda b, h: (h, tail_blk + b))],
        out_specs=[pl.BlockSpec((SEQ, M_DV), lambda b, h: (b, h)),
                   pl.BlockSpec((TAIL, M_DV), lambda b, h: (b, h))],
        out_shape=[jax.ShapeDtypeStruct((N_REAL, M_INNER), F32),
                   jax.ShapeDtypeStruct((BATCH * TAIL, M_INNER), F32)],
        scratch_shapes=[pltpu.VMEM((2, M_DK, M_DV), F32),
                        pltpu.VMEM((2, 1, M_DK), F32),
                        pltpu.VMEM((2, 1, 1), F32)],
        compiler_params=_params("parallel", "parallel"),
        name="mlstm",
    )(q, q, k, k, xm, xm, gcol, gcol, grow, grow)


PAIR = 2 * NA_DH


def _natten_kernel(q_ref, qt_ref, k_ref, kt_ref, v_ref, vt_ref, bias_ref, o_ref, ot_ref):
    lane = lax.broadcasted_iota(jnp.int32, (1, PAIR), 1)
    first = lane < NA_DH
    k_tail = kt_ref[...]
    v_tail = vt_ref[...]

    def attend(q, keys, vals, bias_t):
        n = q.shape[0]
        zero = jnp.zeros_like(q)
        qs = jnp.concatenate([jnp.where(first, q, zero), jnp.where(first, zero, q)], axis=0)
        s = _dot_nt(keys, qs) + bias_t
        e = jnp.exp(s - jnp.max(s, axis=0, keepdims=True))
        p = e * (1.0 / jnp.sum(e, axis=0, keepdims=True))
        o = _dot(jnp.transpose(p.astype(BF16)), vals)
        return jnp.where(first, o[:n], o[n:])

    def row_block(r, carry):
        r0 = jnp.clip(r - NA_WIN_H // 2, 0, ROWS - NA_WIN_H)
        qrows = pl.ds(pl.multiple_of(r * GRID_W, GRID_W), GRID_W)
        krows = pl.ds(pl.multiple_of(r0 * GRID_W, GRID_W), NA_WIN_H * GRID_W)
        keys = jnp.concatenate([k_ref[krows, :], k_tail], axis=0)
        vals = jnp.concatenate([v_ref[krows, :], v_tail], axis=0)
        o = attend(q_ref[qrows, :], keys, vals, bias_ref[r - r0])
        o_ref[qrows, :] = o.astype(o_ref.dtype)
        return carry

    lax.fori_loop(0, ROWS, row_block, 0, unroll=8)
    tb = bias_ref[0, NA_WIN_H * GRID_W:, :]
    tail_bias = jnp.concatenate([jnp.broadcast_to(tb[:, 0:1], (TAIL, TAIL)),
                                 jnp.broadcast_to(tb[:, NA_DH:NA_DH + 1], (TAIL, TAIL))], axis=1)
    ot_ref[...] = attend(qt_ref[...], k_tail, v_tail, tail_bias).astype(ot_ref.dtype)


def _natten(qn, kn, vn, bias):
    tail_blk = N_REAL // TAIL
    real = pl.BlockSpec((SEQ, PAIR), lambda b, p: (b, p))
    tail = pl.BlockSpec((TAIL, PAIR), lambda b, p: (tail_blk + b, p))
    return pl.pallas_call(
        _natten_kernel,
        grid=(BATCH, NA_HEADS // 2),
        in_specs=[real, tail, real, tail, real, tail,
                  pl.BlockSpec((None, NA_WIN_H, NA_KEYS, PAIR), lambda b, p: (p, 0, 0, 0))],
        out_specs=[pl.BlockSpec((SEQ, PAIR), lambda b, p: (b, p)),
                   pl.BlockSpec((TAIL, PAIR), lambda b, p: (b, p))],
        out_shape=[jax.ShapeDtypeStruct((N_REAL, NA_INNER), BF16),
                   jax.ShapeDtypeStruct((BATCH * TAIL, NA_INNER), BF16)],
        compiler_params=_params("parallel", "parallel"),
        name="natten",
    )(qn, qn, kn, kn, vn, vn, bias)


def _natten_bias(rpb, meta_bias):
    qc = jnp.arange(GRID_W)
    kc = jnp.arange(GRID_W)
    win0 = jnp.clip(qc - NA_WIN_W // 2, 0, GRID_W - NA_WIN_W)
    ok = (kc[None, :] >= win0[:, None]) & (kc[None, :] < win0[:, None] + NA_WIN_W)
    dc = jnp.clip(kc[None, :] - qc[:, None], -(NA_WIN_W - 1), NA_WIN_W - 1) + NA_WIN_W - 1
    onehot = (dc[None] == jnp.arange(2 * NA_WIN_W - 1)[:, None, None]).astype(F32)
    t1 = jnp.einsum('hdj,jqk->hdqk', rpb.astype(F32), onehot, precision=lax.Precision.HIGHEST)
    t1 = jnp.where(ok[None, None], t1, MASKED)
    loc = jnp.stack([t1[:, NA_WIN_H - 1 - dl:2 * NA_WIN_H - 1 - dl] for dl in range(NA_WIN_H)], axis=1)
    loc = jnp.transpose(loc, (0, 1, 3, 2, 4))
    loc = loc.reshape(NA_HEADS, NA_WIN_H, GRID_W, NA_WIN_H * GRID_W)
    met = jnp.concatenate([jnp.full((NA_HEADS, NPAD), MASKED, F32), meta_bias.astype(F32)], axis=1)
    met = jnp.broadcast_to(met[:, None, None, :], (NA_HEADS, NA_WIN_H, GRID_W, TAIL))
    bias = jnp.concatenate([loc, met], axis=-1)
    bias = bias.reshape(NA_HEADS // 2, 2, NA_WIN_H, GRID_W, NA_KEYS)
    return jnp.transpose(bias, (0, 2, 4, 1, 3)).reshape(NA_HEADS // 2, NA_WIN_H, NA_KEYS, PAIR)


def _merge_kernel(hsr_ref, hst_ref, ybr_ref, ybt_ref, xr_ref, xt_ref, xc_ref, og_ref, gab_ref,
                  ng_ref, sk_ref, wa_ref, wb_ref, wo_ref, o_ref):
    def body(hs_ref, yb_ref, h_ref):
        hs = hs_ref[...]
        parts = []
        for hd in range(M_HEADS):
            sl = hs[:, hd * M_DV:(hd + 1) * M_DV]
            parts.append(sl * lax.rsqrt(jnp.mean(sl * sl, axis=-1, keepdims=True) + EPS))
        hn = jnp.concatenate(parts, axis=1) * ng_ref[...]
        y_a = _sigmoid(og_ref[...].astype(F32)) * (hn + sk_ref[...] * xc_ref[...].astype(F32))
        gab = gab_ref[...].astype(F32)
        mix = (_sigmoid(gab[:, :D_MODEL]) * _dot(y_a.astype(BF16), wa_ref[...])
               + _sigmoid(gab[:, D_MODEL:]) * _dot(yb_ref[...], wb_ref[...]))
        o_ref[...] = h_ref[...] + _dot(mix.astype(BF16), wo_ref[...])

    _real_or_tail(body, hsr_ref, hst_ref, ybr_ref, ybt_ref, xr_ref, xt_ref)


def _merge(hs_real, hs_tail, yb_real, yb_tail, x2d, tailh, xc, og, gab, ng, sk, wa, wb, wo):
    return pl.pallas_call(
        _merge_kernel,
        grid=(N_TILES,),
        in_specs=[_real_spec(M_INNER), _tail_spec(M_INNER), _real_spec(NA_INNER), _tail_spec(NA_INNER),
                  _real_spec(D_MODEL), _tail_spec(D_MODEL),
                  _row_spec(M_INNER), _row_spec(M_INNER), _row_spec(2 * D_MODEL)]
                 + [_const_spec(a.shape) for a in (ng, sk, wa, wb, wo)],
        out_specs=_row_spec(D_MODEL),
        out_shape=jax.ShapeDtypeStruct((NT, D_MODEL), F32),
        compiler_params=_params("parallel"),
        name="merge",
    )(hs_real, hs_tail, yb_real, yb_tail, x2d, tailh, xc, og, gab, ng, sk, wa, wb, wo)


def _ffn_kernel(h_ref, g2_ref, w1_ref, w2_ref, or_ref, ot_ref):
    def body(o_ref):
        h = h_ref[...]
        xn = (h * lax.rsqrt(jnp.mean(h * h, axis=-1, keepdims=True) + EPS) * g2_ref[...]).astype(BF16)
        z = jnp.maximum(_dot(xn, w1_ref[...]), 0.0)
        o_ref[...] = h + _dot((z * z).astype(BF16), w2_ref[...])

    _real_or_tail(body, or_ref, ot_ref)


def _ffn(h1, g2, w1, w2):
    return pl.pallas_call(
        _ffn_kernel,
        grid=(N_TILES,),
        in_specs=[_row_spec(D_MODEL), _const_spec(g2.shape), _const_spec(w1.shape), _const_spec(w2.shape)],
        out_specs=[_real_spec(D_MODEL), _tail_spec(D_MODEL)],
        out_shape=[jax.ShapeDtypeStruct((N_REAL, D_MODEL), F32),
                   jax.ShapeDtypeStruct((BATCH * TAIL, D_MODEL), F32)],
        compiler_params=_params("arbitrary"),
        name="ffn",
    )(h1, g2, w1, w2)


def kernel(x, meta_tokens, norm1_g, w_in, mlstm_conv_w, mlstm_conv_b, mlstm_wq, mlstm_wk, mlstm_gate_b, mlstm_norm_g, mlstm_skip, na_q_norm_g, na_k_norm_g, na_rpb, na_meta_bias, w_branch_a, w_branch_b, w_out, norm2_g, w_ff1, w_ff2):
    off_mo = M_INNER
    off_mg = off_mo + M_INNER
    off_q = off_mg + 4 * M_HEADS
    off_v = off_q + 2 * NA_INNER
    off_g = off_v + NA_INNER

    x2d = x.astype(F32).reshape(N_REAL, D_MODEL)
    tail = jnp.concatenate([jnp.zeros((NPAD, D_MODEL), F32), meta_tokens.astype(F32)], axis=0)
    tailh = jnp.tile(tail, (BATCH, 1))

    wb16 = w_in.astype(BF16)
    pad_gates = lambda a: jnp.pad(
        jnp.swapaxes(a.reshape(a.shape[0], 4, M_HEADS), 1, 2),
        ((0, 0), (0, 0), (0, GATE_PAD - 4))).reshape(a.shape[0], M_HEADS * GATE_PAD)
    wg = jnp.pad(pad_gates(wb16[:, off_mg:off_q]), ((0, 0), (0, 128 - M_HEADS * GATE_PAD)))
    gb = jnp.pad(pad_gates(mlstm_gate_b.astype(F32).reshape(1, 4 * M_HEADS)),
                 ((0, 0), (0, 128 - M_HEADS * GATE_PAD)))
    bd =jnp.kron(jnp.eye(NA_HEADS, dtype=F32), jnp.full((NA_DH, NA_DH), 1.0 / NA_DH, F32)).astype(BF16)
    qg = jnp.tile(na_q_norm_g.astype(F32), NA_HEADS)[None, :] * (NA_DH ** -0.5)
    kg = jnp.tile(na_k_norm_g.astype(F32), NA_HEADS)[None, :]

    xm, og, gcol, grow, qn, kn, vn, gab = _inproj(
        x2d, tailh, norm1_g.astype(F32)[None, :], wb16[:, :off_mo], wb16[:, off_mo:off_mg], wg,
        wb16[:, off_q:off_g], wb16[:, off_g:], gb, qg, kg, bd)

    wqk = jnp.concatenate([mlstm_wq, mlstm_wk], axis=-1).astype(BF16)
    xc, qm, km = _conv(xm, mlstm_conv_w.astype(F32).reshape(3, M_INNER),
                       mlstm_conv_b.astype(F32)[None, :], wqk)

    hs_real, hs_tail = _mlstm(qm, km, xm, gcol, grow)
    yb_real, yb_tail = _natten(qn, kn, vn, _natten_bias(na_rpb, na_meta_bias))

    h1 = _merge(hs_real, hs_tail, yb_real, yb_tail, x2d, tailh, xc, og, gab,
                mlstm_norm_g.astype(F32).reshape(1, M_INNER), mlstm_skip.astype(F32)[None, :],
                w_branch_a.astype(BF16), w_branch_b.astype(BF16), w_out.astype(BF16))
    out_real, _ = _ffn(h1, norm2_g.astype(F32)[None, :], w_ff1.astype(BF16), w_ff2.astype(BF16))
    return out_real.reshape(BATCH, SEQ, D_MODEL)
```

```python
import functools

import jax
import jax.numpy as jnp
from jax import lax
from jax.experimental import pallas as pl
from jax.experimental.pallas import tpu as pltpu

D_MODEL = 1024
BATCH = 4
SEQ = 4096
N_META = 16
GRID_W = 64
ROWS = SEQ // GRID_W
M_HEADS = 4
M_DV = 256
M_DK = 128
M_INNER = M_HEADS * M_DV
NA_HEADS = 8
NA_DH = 64
NA_INNER = NA_HEADS * NA_DH
NA_WIN_H = 8
NA_WIN_W = 16
D_FF = 4 * D_MODEL
EPS = 1e-6
NEG_LOG_GATE = -1e9
MASKED = -1e30

TAIL = 128
NPAD = TAIL - N_META
N_REAL = BATCH * SEQ
NT = N_REAL + BATCH * TAIL
TM = 512
N_TILES = NT // TM
N_REAL_TILES = N_REAL // TM
NA_KEYS = NA_WIN_H * GRID_W + TAIL

VMEM_LIMIT = 56 * 1024 * 1024

F32 = jnp.float32
BF16 = jnp.bfloat16


def _dot(a, b):
    return jnp.dot(a, b, preferred_element_type=F32)


def _dot_nt(a, b):
    return lax.dot_general(a, b, (((1,), (1,)), ((), ())), preferred_element_type=F32)


def _dot_tn(a, b):
    return lax.dot_general(a, b, (((0,), (0,)), ((), ())), preferred_element_type=F32)


def _sigmoid(z):
    return 1.0 / (1.0 + jnp.exp(-z))


def _log_sigmoid(z):
    return jnp.minimum(z, 0.0) - jnp.log1p(jnp.exp(-jnp.abs(z)))


def _const_spec(shape):
    nd = len(shape)
    return pl.BlockSpec(shape, lambda *_: (0,) * nd, pipeline_mode=pl.Buffered(1))


def _params(*sem):
    return pltpu.CompilerParams(dimension_semantics=sem, vmem_limit_bytes=VMEM_LIMIT)


GATE_PAD = 8


def _inproj_kernel(x_ref, tail_ref, g1_ref, wx_ref, wo_ref, wg_ref, wqkv_ref, wgab_ref, gb_ref,
                   qg_ref, kg_ref, bd_ref,
                   xm_ref, og_ref, grow_ref, q_ref, k_ref, v_ref, gab_ref):
    def body(h_ref):
        h = h_ref[...]
        ms = jnp.mean(h * h, axis=-1, keepdims=True)
        xn = (h * lax.rsqrt(ms + EPS) * g1_ref[...]).astype(BF16)
        xm_ref[...] = _dot(xn, wx_ref[...])
        og_ref[...] = _dot(xn, wo_ref[...]).astype(BF16)
        gates = _dot(xn, wg_ref[...]) + gb_ref[...]
        grow_ref[...] = jnp.transpose(gates)[:M_HEADS * GATE_PAD, :]
        gab_ref[...] = _dot(xn, wgab_ref[...]).astype(BF16)
        qkv = _dot(xn, wqkv_ref[...])
        uq = qkv[:, :NA_INNER]
        uk = qkv[:, NA_INNER:2 * NA_INNER]
        msq = _dot((uq * uq).astype(BF16), bd_ref[...])
        msk = _dot((uk * uk).astype(BF16), bd_ref[...])
        q_ref[...] = (uq * lax.rsqrt(msq + EPS) * qg_ref[...]).astype(BF16)
        k_ref[...] = (uk * lax.rsqrt(msk + EPS) * kg_ref[...]).astype(BF16)
        v_ref[...] = qkv[:, 2 * NA_INNER:].astype(BF16)

    _real_or_tail(body, x_ref, tail_ref)


def _real_or_tail(body, *ref_pairs):
    i = pl.program_id(0)
    pl.when(i < N_REAL_TILES)(lambda: body(*ref_pairs[0::2]))
    pl.when(i == N_REAL_TILES)(lambda: body(*ref_pairs[1::2]))


def _real_spec(width):
    return pl.BlockSpec((TM, width), lambda i: (jnp.minimum(i, N_REAL_TILES - 1), 0))


def _tail_spec(width):
    return pl.BlockSpec((TM, width), lambda i: (0, 0))


def _row_spec(width):
    return pl.BlockSpec((TM, width), lambda i: (i, 0))


def _inproj(x2d, tailh, g1, wx, wo, wg, wqkv, wgab, gb, qg, kg, bd):
    return pl.pallas_call(
        _inproj_kernel,
        grid=(N_TILES,),
        in_specs=[_real_spec(D_MODEL), _tail_spec(D_MODEL)]
                 + [_const_spec(a.shape) for a in (g1, wx, wo, wg, wqkv, wgab, gb, qg, kg, bd)],
        out_specs=[_row_spec(M_INNER), _row_spec(M_INNER),
                   pl.BlockSpec((M_HEADS * GATE_PAD, TM), lambda i: (0, i)),
                   _row_spec(NA_INNER), _row_spec(NA_INNER), _row_spec(NA_INNER), _row_spec(2 * D_MODEL)],
        out_shape=[jax.ShapeDtypeStruct((NT, M_INNER), F32),
                   jax.ShapeDtypeStruct((NT, M_INNER), BF16),
                   jax.ShapeDtypeStruct((M_HEADS * GATE_PAD, NT), F32),
                   jax.ShapeDtypeStruct((NT, NA_INNER), BF16),
                   jax.ShapeDtypeStruct((NT, NA_INNER), BF16),
                   jax.ShapeDtypeStruct((NT, NA_INNER), BF16),
                   jax.ShapeDtypeStruct((NT, 2 * D_MODEL), BF16)],
        compiler_params=_params("parallel"),
        name="inproj",
    )(x2d, tailh, g1, wx, wo, wg, wqkv, wgab, gb, qg, kg, bd)


CT = 128
N_CT = SEQ // CT
HALO = 8


def _conv_kernel(x_ref, prev_ref, next_ref, cw_ref, cb_ref, wqk_ref, xc_ref, q_ref, k_ref):
    x = x_ref[...]
    rows = lax.broadcasted_iota(jnp.int32, (CT, M_INNER), 0)
    x_m1 = jnp.where(rows == 0, prev_ref[HALO - 1:HALO, :], pltpu.roll(x, 1, axis=0))
    x_p1 = jnp.where(rows == CT - 1, next_ref[0:1, :], pltpu.roll(x, CT - 1, axis=0))
    z = cw_ref[0:1, :] * x_m1 + cw_ref[1:2, :] * x + cw_ref[2:3, :] * x_p1 + cb_ref[...]
    xc = z * _sigmoid(z)
    xc_ref[...] = xc.astype(xc_ref.dtype)
    for hd in range(M_HEADS):
        qk = _dot(xc[:, hd * M_DV:(hd + 1) * M_DV].astype(BF16), wqk_ref[hd])
        q_ref[:, hd * M_DK:(hd + 1) * M_DK] = qk[:, :M_DK].astype(BF16)
        k_ref[:, hd * M_DK:(hd + 1) * M_DK] = (qk[:, M_DK:] * (M_DK ** -0.5)).astype(BF16)


def _conv(xm, cw, cb, wqk):
    real_blocks = SEQ // HALO
    tail_blocks = TAIL // HALO
    tail0 = N_REAL // HALO

    def main_map(b, j):
        return (jnp.where(j < N_CT, b * N_CT + j, N_REAL // CT + b), 0)

    def prev_map(b, j):
        inner = b * real_blocks + j * (CT // HALO) - 1
        first = tail0 + b * tail_blocks + tail_blocks - 1
        tail = b * real_blocks + real_blocks - 1
        return (jnp.where(j == 0, first, jnp.where(j == N_CT, tail, inner)), 0)

    def next_map(b, j):
        inner = b * real_blocks + (j + 1) * (CT // HALO)
        last = tail0 + b * tail_blocks
        tail = b * real_blocks
        return (jnp.where(j == N_CT - 1, last, jnp.where(j == N_CT, tail, inner)), 0)

    return pl.pallas_call(
        _conv_kernel,
        grid=(BATCH, N_CT + 1),
        in_specs=[pl.BlockSpec((CT, M_INNER), main_map),
                  pl.BlockSpec((HALO, M_INNER), prev_map),
                  pl.BlockSpec((HALO, M_INNER), next_map),
                  _const_spec(cw.shape), _const_spec(cb.shape), _const_spec(wqk.shape)],
        out_specs=[pl.BlockSpec((CT, M_INNER), main_map),
                   pl.BlockSpec((CT, M_HEADS * M_DK), main_map),
                   pl.BlockSpec((CT, M_HEADS * M_DK), main_map)],
        out_shape=[jax.ShapeDtypeStruct((NT, M_INNER), BF16),
                   jax.ShapeDtypeStruct((NT, M_HEADS * M_DK), BF16),
                   jax.ShapeDtypeStruct((NT, M_HEADS * M_DK), BF16)],
        compiler_params=_params("parallel", "parallel"),
        name="conv_qk",
    )(xm, xm, xm, cw, cb, wqk)


MT = 256
N_MCHUNK = SEQ // MT
STATE_ROWS = M_DV + 16
ROWS_PER_DIR = 8


def _split3(x):
    hi = x.astype(BF16)
    r1 = x - hi.astype(F32)
    mid = r1.astype(BF16)
    lo = (r1 - mid.astype(F32)).astype(BF16)
    return hi, mid, lo


def _mlstm_local(q, k, v, g8, is_tail):
    t = q.shape[0]
    li = [g8[0:1], g8[2:3]]
    lf = [_log_sigmoid(g8[1:2]), _log_sigmoid(g8[3:4])]
    if is_tail:
        pad = lax.broadcasted_iota(jnp.int32, (1, t), 1) < NPAD
        li = [jnp.where(pad, NEG_LOG_GATE, x) for x in li]
        lf = [jnp.where(pad, 0.0, x) for x in lf]
    si = lax.broadcasted_iota(jnp.int32, (t, t), 0)
    ti = lax.broadcasted_iota(jnp.int32, (t, t), 1)
    hi, mid, lo = _split3(jnp.concatenate(lf, axis=0))
    lhs = jnp.concatenate([hi, mid, lo, jnp.zeros((10, t), BF16)], axis=0)
    pref = _dot(lhs, (si <= ti).astype(BF16))
    pre_f = pref[0:1] + pref[2:3] + pref[4:5]
    pre_b = pref[1:2] + pref[3:4] + pref[5:6]
    b_end = [pre_f[:, t - 1:t], pre_b[:, t - 1:t]]
    b = [pre_f, b_end[1] - pre_b + lf[1]]
    s_t = _dot_nt(k, q)
    v_t = jnp.transpose(v.astype(BF16))
    one = jnp.ones((1, t), BF16)
    out = []
    for dirn in range(2):
        g = li[dirn] - b[dirn]
        a = b_end[dirn] + g
        gh, gm, gl = _split3(g)
        bh, bm, bl = _split3(b[dirn])
        zpad = jnp.zeros((10, t), BF16)
        d_t = _dot_tn(jnp.concatenate([gh, gm, gl, one, one, one, zpad], axis=0),
                      jnp.concatenate([one, one, one, bh, bm, bl, zpad], axis=0))
        d_t = jnp.where((si <= ti) if dirn == 0 else (si >= ti), d_t, MASKED)
        m_loc = jnp.max(d_t, axis=0, keepdims=True)
        p_t = s_t * jnp.exp(d_t - m_loc)
        den = jnp.sum(p_t, axis=0, keepdims=True)
        nl_t = _dot(v_t, p_t.astype(BF16))
        rows = jnp.concatenate([m_loc, den, b[dirn], a, jnp.broadcast_to(b_end[dirn], (1, t)),
                                jnp.zeros((ROWS_PER_DIR - 5, t), F32)], axis=0)
        out.append((nl_t, rows))
    return v_t, out


def _mlstm_state(dirn, q, k, v_t, nl_t, rows, s_ref, m_ref):
    t = q.shape[0]
    m_loc, den_loc, b, a = rows[0:1], rows[1:2], rows[2:3], rows[3:4]
    b_end = rows[4:5, 0:1]
    m_prev = m_ref[dirn]
    s_old = s_ref[dirn]
    m_new = jnp.maximum(b_end + m_prev, jnp.max(a, axis=1, keepdims=True))
    decay = jnp.exp(b_end + m_prev - m_new)
    w = jnp.exp(a - m_new)
    lhs = jnp.concatenate([(v_t.astype(F32) * w).astype(BF16),
                           jnp.broadcast_to(w, (STATE_ROWS - M_DV, t)).astype(BF16)], axis=0)
    inter = _dot_nt(s_old.astype(BF16), q)
    m_inter = b + m_prev
    m_t = jnp.maximum(m_inter, m_loc)
    w_inter = jnp.exp(m_inter - m_t)
    w_loc = jnp.exp(m_loc - m_t)
    den = w_inter * inter[M_DV:M_DV + 1] + w_loc * den_loc
    scale = 1.0 / jnp.maximum(jnp.abs(den), jnp.exp(-m_t))
    h_t = (w_inter * scale) * inter[:M_DV] + (w_loc * scale) * nl_t
    s_ref[dirn] = decay * s_old + _dot(lhs, k)
    m_ref[dirn] = m_new
    return h_t


def _mlstm_kernel(q_ref, qt_ref, k_ref, kt_ref, v_ref, vt_ref, g_ref, gt_ref, o_ref, ot_ref,
                  vt_s, nl_s, rows_s, ht_s, s_ref, m_ref):
    s_ref[...] = jnp.zeros_like(s_ref)
    m_ref[...] = jnp.zeros_like(m_ref)
    ht_s[...] = jnp.zeros_like(ht_s)
    tail_cols = pl.ds(SEQ, TAIL)

    def chunk(c):
        r0 = pl.multiple_of(c * MT, MT)
        return pl.ds(r0, MT)

    def keep_local(cols, v_t, per_dir):
        vt_s[:, cols] = v_t
        for dirn, (nl_t, rows) in enumerate(per_dir):
            nl_s[dirn, :, cols] = nl_t
            rows_s[dirn, :, cols] = rows

    def local_pass(c, carry):
        cols = chunk(c)
        keep_local(cols, *_mlstm_local(q_ref[cols, :], k_ref[cols, :], v_ref[cols, :], g_ref[:, cols], False))
        return carry

    def state_step(dirn, q, k, cols):
        ht_s[:, cols] += _mlstm_state(dirn, q, k, vt_s[:, cols], nl_s[dirn, :, cols],
                                      rows_s[dirn, :, cols], s_ref, m_ref)

    def state_pass(i, carry):
        cf = chunk(i)
        state_step(0, q_ref[cf, :], k_ref[cf, :], cf)
        cb = chunk(N_MCHUNK - 1 - i)
        state_step(1, q_ref[cb, :], k_ref[cb, :], cb)
        return carry

    def emit(c, carry):
        cols = chunk(c)
        o_ref[cols, :] = jnp.transpose(ht_s[:, cols])
        return carry

    keep_local(tail_cols, *_mlstm_local(qt_ref[...], kt_ref[...], vt_ref[...], gt_ref[...], True))
    lax.fori_loop(0, N_MCHUNK, local_pass, 0, unroll=4)
    state_step(0, qt_ref[...], kt_ref[...], tail_cols)
    lax.fori_loop(0, N_MCHUNK, state_pass, 0)
    state_step(1, qt_ref[...], kt_ref[...], tail_cols)
    lax.fori_loop(0, N_MCHUNK, emit, 0)
    ot_ref[...] = jnp.transpose(ht_s[:, tail_cols])


def _mlstm(q, k, xm, grow):
    tail_blk = N_REAL // TAIL
    real = lambda w: pl.BlockSpec((SEQ, w), lambda b, h: (b, h))
    tail = lambda w: pl.BlockSpec((TAIL, w), lambda b, h: (tail_blk + b, h))
    lp = SEQ + TAIL
    return pl.pallas_call(
        _mlstm_kernel,
        grid=(BATCH, M_HEADS),
        in_specs=[real(M_DK), tail(M_DK), real(M_DK), tail(M_DK), real(M_DV), tail(M_DV),
                  pl.BlockSpec((GATE_PAD, SEQ), lambda b, h: (h, b)),
                  pl.BlockSpec((GATE_PAD, TAIL), lambda b, h: (h, tail_blk + b))],
        out_specs=[pl.BlockSpec((SEQ, M_DV), lambda b, h: (b, h)),
                   pl.BlockSpec((TAIL, M_DV), lambda b, h: (b, h))],
        out_shape=[jax.ShapeDtypeStruct((N_REAL, M_INNER), F32),
                   jax.ShapeDtypeStruct((BATCH * TAIL, M_INNER), F32)],
        scratch_shapes=[pltpu.VMEM((M_DV, lp), BF16),
                        pltpu.VMEM((2, M_DV, lp), F32),
                        pltpu.VMEM((2, ROWS_PER_DIR, lp), F32),
                        pltpu.VMEM((M_DV, lp), F32),
                        pltpu.VMEM((2, STATE_ROWS, M_DK), F32),
                        pltpu.VMEM((2, 1, 1), F32)],
        compiler_params=_params("parallel", "parallel"),
        name="mlstm",
    )(q, q, k, k, xm, xm, grow, grow)


PAIR = 2 * NA_DH


def _natten_kernel(q_ref, qt_ref, k_ref, kt_ref, v_ref, vt_ref, bias_ref, o_ref, ot_ref):
    lane = lax.broadcasted_iota(jnp.int32, (1, PAIR), 1)
    first = lane < NA_DH
    k_tail = kt_ref[...]
    v_tail = vt_ref[...]

    def attend(q, keys, vals, bias_t):
        n = q.shape[0]
        zero = jnp.zeros_like(q)
        qs = jnp.concatenate([jnp.where(first, q, zero), jnp.where(first, zero, q)], axis=0)
        s = _dot_nt(keys, qs) + bias_t
        e = jnp.exp(s - jnp.max(s, axis=0, keepdims=True))
        p = e * (1.0 / jnp.sum(e, axis=0, keepdims=True))
        o = _dot(jnp.transpose(p.astype(BF16)), vals)
        return jnp.where(first, o[:n], o[n:])

    def row_block(r, carry):
        r0 = jnp.clip(r - NA_WIN_H // 2, 0, ROWS - NA_WIN_H)
        qrows = pl.ds(pl.multiple_of(r * GRID_W, GRID_W), GRID_W)
        krows = pl.ds(pl.multiple_of(r0 * GRID_W, GRID_W), NA_WIN_H * GRID_W)
        keys = jnp.concatenate([k_ref[krows, :], k_tail], axis=0)
        vals = jnp.concatenate([v_ref[krows, :], v_tail], axis=0)
        o = attend(q_ref[qrows, :], keys, vals, bias_ref[r - r0])
        o_ref[qrows, :] = o.astype(o_ref.dtype)
        return carry

    lax.fori_loop(0, ROWS, row_block, 0, unroll=8)
    tb = bias_ref[0, NA_WIN_H * GRID_W:, :]
    tail_bias = jnp.concatenate([jnp.broadcast_to(tb[:, 0:1], (TAIL, TAIL)),
                                 jnp.broadcast_to(tb[:, NA_DH:NA_DH + 1], (TAIL, TAIL))], axis=1)
    ot_ref[...] = attend(qt_ref[...], k_tail, v_tail, tail_bias).astype(ot_ref.dtype)


def _natten(qn, kn, vn, bias):
    tail_blk = N_REAL // TAIL
    real = pl.BlockSpec((SEQ, PAIR), lambda b, p: (b, p))
    tail = pl.BlockSpec((TAIL, PAIR), lambda b, p: (tail_blk + b, p))
    return pl.pallas_call(
        _natten_kernel,
        grid=(BATCH, NA_HEADS // 2),
        in_specs=[real, tail, real, tail, real, tail,
                  pl.BlockSpec((None, NA_WIN_H, NA_KEYS, PAIR), lambda b, p: (p, 0, 0, 0))],
        out_specs=[pl.BlockSpec((SEQ, PAIR), lambda b, p: (b, p)),
                   pl.BlockSpec((TAIL, PAIR), lambda b, p: (b, p))],
        out_shape=[jax.ShapeDtypeStruct((N_REAL, NA_INNER), BF16),
                   jax.ShapeDtypeStruct((BATCH * TAIL, NA_INNER), BF16)],
        compiler_params=_params("parallel", "parallel"),
        name="natten",
    )(qn, qn, kn, kn, vn, vn, bias)


def _natten_bias(rpb, meta_bias):
    qc = jnp.arange(GRID_W)
    kc = jnp.arange(GRID_W)
    win0 = jnp.clip(qc - NA_WIN_W // 2, 0, GRID_W - NA_WIN_W)
    ok = (kc[None, :] >= win0[:, None]) & (kc[None, :] < win0[:, None] + NA_WIN_W)
    dc = jnp.clip(kc[None, :] - qc[:, None], -(NA_WIN_W - 1), NA_WIN_W - 1) + NA_WIN_W - 1
    onehot = (dc[None] == jnp.arange(2 * NA_WIN_W - 1)[:, None, None]).astype(F32)
    t1 = jnp.einsum('hdj,jqk->hdqk', rpb.astype(F32), onehot, precision=lax.Precision.HIGHEST)
    t1 = jnp.where(ok[None, None], t1, MASKED)
    loc = jnp.stack([t1[:, NA_WIN_H - 1 - dl:2 * NA_WIN_H - 1 - dl] for dl in range(NA_WIN_H)], axis=1)
    loc = jnp.transpose(loc, (0, 1, 3, 2, 4))
    loc = loc.reshape(NA_HEADS, NA_WIN_H, GRID_W, NA_WIN_H * GRID_W)
    met = jnp.concatenate([jnp.full((NA_HEADS, NPAD), MASKED, F32), meta_bias.astype(F32)], axis=1)
    met = jnp.broadcast_to(met[:, None, None, :], (NA_HEADS, NA_WIN_H, GRID_W, TAIL))
    bias = jnp.concatenate([loc, met], axis=-1)
    bias = bias.reshape(NA_HEADS // 2, 2, NA_WIN_H, GRID_W, NA_KEYS)
    return jnp.transpose(bias, (0, 2, 4, 1, 3)).reshape(NA_HEADS // 2, NA_WIN_H, NA_KEYS, PAIR)


def _merge_kernel(hsr_ref, hst_ref, ybr_ref, ybt_ref, xr_ref, xt_ref, xc_ref, og_ref, gab_ref,
                  ng_ref, sk_ref, wa_ref, wb_ref, wo_ref, o_ref):
    def body(hs_ref, yb_ref, h_ref):
        hs = hs_ref[...]
        parts = []
        for hd in range(M_HEADS):
            sl = hs[:, hd * M_DV:(hd + 1) * M_DV]
            parts.append(sl * lax.rsqrt(jnp.mean(sl * sl, axis=-1, keepdims=True) + EPS))
        hn = jnp.concatenate(parts, axis=1) * ng_ref[...]
        y_a = _sigmoid(og_ref[...].astype(F32)) * (hn + sk_ref[...] * xc_ref[...].astype(F32))
        gab = gab_ref[...].astype(F32)
        mix = (_sigmoid(gab[:, :D_MODEL]) * _dot(y_a.astype(BF16), wa_ref[...])
               + _sigmoid(gab[:, D_MODEL:]) * _dot(yb_ref[...], wb_ref[...]))
        o_ref[...] = h_ref[...] + _dot(mix.astype(BF16), wo_ref[...])

    _real_or_tail(body, hsr_ref, hst_ref, ybr_ref, ybt_ref, xr_ref, xt_ref)


def _merge(hs_real, hs_tail, yb_real, yb_tail, x2d, tailh, xc, og, gab, ng, sk, wa, wb, wo):
    return pl.pallas_call(
        _merge_kernel,
        grid=(N_TILES,),
        in_specs=[_real_spec(M_INNER), _tail_spec(M_INNER), _real_spec(NA_INNER), _tail_spec(NA_INNER),
                  _real_spec(D_MODEL), _tail_spec(D_MODEL),
                  _row_spec(M_INNER), _row_spec(M_INNER), _row_spec(2 * D_MODEL)]
                 + [_const_spec(a.shape) for a in (ng, sk, wa, wb, wo)],
        out_specs=_row_spec(D_MODEL),
        out_shape=jax.ShapeDtypeStruct((NT, D_MODEL), F32),
        compiler_params=_params("parallel"),
        name="merge",
    )(hs_real, hs_tail, yb_real, yb_tail, x2d, tailh, xc, og, gab, ng, sk, wa, wb, wo)


def _ffn_kernel(h_ref, g2_ref, w1_ref, w2_ref, or_ref, ot_ref):
    def body(o_ref):
        h = h_ref[...]
        xn = (h * lax.rsqrt(jnp.mean(h * h, axis=-1, keepdims=True) + EPS) * g2_ref[...]).astype(BF16)
        z = jnp.maximum(_dot(xn, w1_ref[...]), 0.0)
        o_ref[...] = h + _dot((z * z).astype(BF16), w2_ref[...])

    _real_or_tail(body, or_ref, ot_ref)


def _ffn(h1, g2, w1, w2):
    return pl.pallas_call(
        _ffn_kernel,
        grid=(N_TILES,),
        in_specs=[_row_spec(D_MODEL), _const_spec(g2.shape), _const_spec(w1.shape), _const_spec(w2.shape)],
        out_specs=[_real_spec(D_MODEL), _tail_spec(D_MODEL)],
        out_shape=[jax.ShapeDtypeStruct((N_REAL, D_MODEL), F32),
                   jax.ShapeDtypeStruct((BATCH * TAIL, D_MODEL), F32)],
        compiler_params=_params("arbitrary"),
        name="ffn",
    )(h1, g2, w1, w2)


def kernel(x, meta_tokens, norm1_g, w_in, mlstm_conv_w, mlstm_conv_b, mlstm_wq, mlstm_wk, mlstm_gate_b, mlstm_norm_g, mlstm_skip, na_q_norm_g, na_k_norm_g, na_rpb, na_meta_bias, w_branch_a, w_branch_b, w_out, norm2_g, w_ff1, w_ff2):
    off_mo = M_INNER
    off_mg = off_mo + M_INNER
    off_q = off_mg + 4 * M_HEADS
    off_v = off_q + 2 * NA_INNER
    off_g = off_v + NA_INNER

    x2d = x.astype(F32).reshape(N_REAL, D_MODEL)
    tail = jnp.concatenate([jnp.zeros((NPAD, D_MODEL), F32), meta_tokens.astype(F32)], axis=0)
    tailh = jnp.tile(tail, (BATCH, 1))

    wb16 = w_in.astype(BF16)
    pad_gates = lambda a: jnp.pad(
        jnp.swapaxes(a.reshape(a.shape[0], 4, M_HEADS), 1, 2),
        ((0, 0), (0, 0), (0, GATE_PAD - 4))).reshape(a.shape[0], M_HEADS * GATE_PAD)
    wg = jnp.pad(pad_gates(wb16[:, off_mg:off_q]), ((0, 0), (0, 128 - M_HEADS * GATE_PAD)))
    gb = jnp.pad(pad_gates(mlstm_gate_b.astype(F32).reshape(1, 4 * M_HEADS)),
                 ((0, 0), (0, 128 - M_HEADS * GATE_PAD)))
    bd =jnp.kron(jnp.eye(NA_HEADS, dtype=F32), jnp.full((NA_DH, NA_DH), 1.0 / NA_DH, F32)).astype(BF16)
    qg = jnp.tile(na_q_norm_g.astype(F32), NA_HEADS)[None, :] * (NA_DH ** -0.5)
    kg = jnp.tile(na_k_norm_g.astype(F32), NA_HEADS)[None, :]

    xm, og, grow, qn, kn, vn, gab = _inproj(
        x2d, tailh, norm1_g.astype(F32)[None, :], wb16[:, :off_mo], wb16[:, off_mo:off_mg], wg,
        wb16[:, off_q:off_g], wb16[:, off_g:], gb, qg, kg, bd)

    wqk = jnp.concatenate([mlstm_wq, mlstm_wk], axis=-1).astype(BF16)
    xc, qm, km = _conv(xm, mlstm_conv_w.astype(F32).reshape(3, M_INNER),
                       mlstm_conv_b.astype(F32)[None, :], wqk)

    hs_real, hs_tail = _mlstm(qm, km, xm, grow)
    yb_real, yb_tail = _natten(qn, kn, vn, _natten_bias(na_rpb, na_meta_bias))

    h1 = _merge(hs_real, hs_tail, yb_real, yb_tail, x2d, tailh, xc, og, gab,
                mlstm_norm_g.astype(F32).reshape(1, M_INNER), mlstm_skip.astype(F32)[None, :],
                w_branch_a.astype(BF16), w_branch_b.astype(BF16), w_out.astype(BF16))
    out_real, _ = _ffn(h1, norm2_g.astype(F32)[None, :], w_ff1.astype(BF16), w_ff2.astype(BF16))
    return out_real.reshape(BATCH, SEQ, D_MODEL)
```

```python
import functools

import jax
import jax.numpy as jnp
from jax import lax
from jax.experimental import pallas as pl
from jax.experimental.pallas import tpu as pltpu

D_MODEL = 1024
BATCH = 4
SEQ = 4096
N_META = 16
GRID_W = 64
ROWS = SEQ // GRID_W
M_HEADS = 4
M_DV = 256
M_DK = 128
M_INNER = M_HEADS * M_DV
NA_HEADS = 8
NA_DH = 64
NA_INNER = NA_HEADS * NA_DH
NA_WIN_H = 8
NA_WIN_W = 16
D_FF = 4 * D_MODEL
EPS = 1e-6
NEG_LOG_GATE = -1e9
MASKED = -1e30
LOG2E = 1.4426950408889634

TAIL = 128
NPAD = TAIL - N_META
N_REAL = BATCH * SEQ
NT = N_REAL + BATCH * TAIL
TM = 512
N_TILES = NT // TM
N_REAL_TILES = N_REAL // TM
NA_KEYS = NA_WIN_H * GRID_W + TAIL

VMEM_LIMIT = 56 * 1024 * 1024

F32 = jnp.float32
BF16 = jnp.bfloat16


def _dot(a, b):
    return jnp.dot(a, b, preferred_element_type=F32)


def _dot_nt(a, b):
    return lax.dot_general(a, b, (((1,), (1,)), ((), ())), preferred_element_type=F32)


def _dot_tn(a, b):
    return lax.dot_general(a, b, (((0,), (0,)), ((), ())), preferred_element_type=F32)


def _sigmoid(z):
    return 0.5 * jnp.tanh(0.5 * z) + 0.5


def _log_sigmoid(z):
    return jnp.minimum(z, 0.0) - jnp.log1p(jnp.exp(-jnp.abs(z)))


def _const_spec(shape):
    nd = len(shape)
    return pl.BlockSpec(shape, lambda *_: (0,) * nd, pipeline_mode=pl.Buffered(1))


def _params(*sem):
    return pltpu.CompilerParams(dimension_semantics=sem, vmem_limit_bytes=VMEM_LIMIT)


GATE_PAD = 8


def _inproj_kernel(x_ref, tail_ref, g1_ref, wx_ref, wo_ref, wg_ref, wqkv_ref, wgab_ref, gb_ref,
                   qg_ref, kg_ref, bd_ref,
                   xm_ref, og_ref, grow_ref, q_ref, k_ref, v_ref, gab_ref):
    def body(h_ref):
        h = h_ref[...]
        ms = jnp.mean(h * h, axis=-1, keepdims=True)
        xn = (h * lax.rsqrt(ms + EPS) * g1_ref[...]).astype(BF16)
        xm_ref[...] = _dot(xn, wx_ref[...])
        og_ref[...] = _dot(xn, wo_ref[...]).astype(BF16)
        gates = _dot(xn, wg_ref[...]) + gb_ref[...]
        grow_ref[...] = jnp.transpose(gates)[:M_HEADS * GATE_PAD, :]
        gab_ref[...] = _dot(xn, wgab_ref[...]).astype(BF16)
        qkv = _dot(xn, wqkv_ref[...])
        uq = qkv[:, :NA_INNER]
        uk = qkv[:, NA_INNER:2 * NA_INNER]
        msq = _dot((uq * uq).astype(BF16), bd_ref[...])
        msk = _dot((uk * uk).astype(BF16), bd_ref[...])
        q_ref[...] = (uq * lax.rsqrt(msq + EPS) * qg_ref[...]).astype(BF16)
        k_ref[...] = (uk * lax.rsqrt(msk + EPS) * kg_ref[...]).astype(BF16)
        v_ref[...] = qkv[:, 2 * NA_INNER:].astype(BF16)

    _real_or_tail(body, x_ref, tail_ref)


def _real_or_tail(body, *ref_pairs):
    i = pl.program_id(0)
    pl.when(i < N_REAL_TILES)(lambda: body(*ref_pairs[0::2]))
    pl.when(i == N_REAL_TILES)(lambda: body(*ref_pairs[1::2]))


def _real_spec(width):
    return pl.BlockSpec((TM, width), lambda i: (jnp.minimum(i, N_REAL_TILES - 1), 0))


def _tail_spec(width):
    return pl.BlockSpec((TM, width), lambda i: (0, 0))


def _row_spec(width):
    return pl.BlockSpec((TM, width), lambda i: (i, 0))


def _inproj(x2d, tailh, g1, wx, wo, wg, wqkv, wgab, gb, qg, kg, bd):
    return pl.pallas_call(
        _inproj_kernel,
        grid=(N_TILES,),
        in_specs=[_real_spec(D_MODEL), _tail_spec(D_MODEL)]
                 + [_const_spec(a.shape) for a in (g1, wx, wo, wg, wqkv, wgab, gb, qg, kg, bd)],
        out_specs=[_row_spec(M_INNER), _row_spec(M_INNER),
                   pl.BlockSpec((M_HEADS * GATE_PAD, TM), lambda i: (0, i)),
                   _row_spec(NA_INNER), _row_spec(NA_INNER), _row_spec(NA_INNER), _row_spec(2 * D_MODEL)],
        out_shape=[jax.ShapeDtypeStruct((NT, M_INNER), F32),
                   jax.ShapeDtypeStruct((NT, M_INNER), BF16),
                   jax.ShapeDtypeStruct((M_HEADS * GATE_PAD, NT), F32),
                   jax.ShapeDtypeStruct((NT, NA_INNER), BF16),
                   jax.ShapeDtypeStruct((NT, NA_INNER), BF16),
                   jax.ShapeDtypeStruct((NT, NA_INNER), BF16),
                   jax.ShapeDtypeStruct((NT, 2 * D_MODEL), BF16)],
        compiler_params=_params("parallel"),
        name="inproj",
    )(x2d, tailh, g1, wx, wo, wg, wqkv, wgab, gb, qg, kg, bd)


CT = 128
N_CT = SEQ // CT
HALO = 8


def _conv_kernel(x_ref, prev_ref, next_ref, cw_ref, cb_ref, wqk_ref, xc_ref, q_ref, k_ref):
    x = x_ref[...]
    rows = lax.broadcasted_iota(jnp.int32, (CT, M_INNER), 0)
    x_m1 = jnp.where(rows == 0, prev_ref[HALO - 1:HALO, :], pltpu.roll(x, 1, axis=0))
    x_p1 = jnp.where(rows == CT - 1, next_ref[0:1, :], pltpu.roll(x, CT - 1, axis=0))
    z = cw_ref[0:1, :] * x_m1 + cw_ref[1:2, :] * x + cw_ref[2:3, :] * x_p1 + cb_ref[...]
    xc = z * _sigmoid(z)
    xc_ref[...] = xc.astype(xc_ref.dtype)
    for hd in range(M_HEADS):
        qk = _dot(xc[:, hd * M_DV:(hd + 1) * M_DV].astype(BF16), wqk_ref[hd])
        q_ref[:, hd * M_DK:(hd + 1) * M_DK] = qk[:, :M_DK].astype(BF16)
        k_ref[:, hd * M_DK:(hd + 1) * M_DK] = (qk[:, M_DK:] * (M_DK ** -0.5)).astype(BF16)


def _conv(xm, cw, cb, wqk):
    real_blocks = SEQ // HALO
    tail_blocks = TAIL // HALO
    tail0 = N_REAL // HALO

    def main_map(b, j):
        return (jnp.where(j < N_CT, b * N_CT + j, N_REAL // CT + b), 0)

    def prev_map(b, j):
        inner = b * real_blocks + j * (CT // HALO) - 1
        first = tail0 + b * tail_blocks + tail_blocks - 1
        tail = b * real_blocks + real_blocks - 1
        return (jnp.where(j == 0, first, jnp.where(j == N_CT, tail, inner)), 0)

    def next_map(b, j):
        inner = b * real_blocks + (j + 1) * (CT // HALO)
        last = tail0 + b * tail_blocks
        tail = b * real_blocks
        return (jnp.where(j == N_CT - 1, last, jnp.where(j == N_CT, tail, inner)), 0)

    return pl.pallas_call(
        _conv_kernel,
        grid=(BATCH, N_CT + 1),
        in_specs=[pl.BlockSpec((CT, M_INNER), main_map),
                  pl.BlockSpec((HALO, M_INNER), prev_map),
                  pl.BlockSpec((HALO, M_INNER), next_map),
                  _const_spec(cw.shape), _const_spec(cb.shape), _const_spec(wqk.shape)],
        out_specs=[pl.BlockSpec((CT, M_INNER), main_map),
                   pl.BlockSpec((CT, M_HEADS * M_DK), main_map),
                   pl.BlockSpec((CT, M_HEADS * M_DK), main_map)],
        out_shape=[jax.ShapeDtypeStruct((NT, M_INNER), BF16),
                   jax.ShapeDtypeStruct((NT, M_HEADS * M_DK), BF16),
                   jax.ShapeDtypeStruct((NT, M_HEADS * M_DK), BF16)],
        compiler_params=_params("parallel", "parallel"),
        name="conv_qk",
    )(xm, xm, xm, cw, cb, wqk)


MT = 256
N_MCHUNK = SEQ // MT
STATE_ROWS = M_DV + 16
ROWS_PER_DIR = 8


def _split3(x):
    hi = x.astype(BF16)
    r1 = x - hi.astype(F32)
    mid = r1.astype(BF16)
    lo = (r1 - mid.astype(F32)).astype(BF16)
    return hi, mid, lo


def _mlstm_local(q, k, v, g8, is_tail):
    t = q.shape[0]
    li = [g8[0:1], g8[2:3]]
    lf = [_log_sigmoid(g8[1:2]), _log_sigmoid(g8[3:4])]
    if is_tail:
        pad = lax.broadcasted_iota(jnp.int32, (1, t), 1) < NPAD
        li = [jnp.where(pad, NEG_LOG_GATE, x) for x in li]
        lf = [jnp.where(pad, 0.0, x) for x in lf]
    si = lax.broadcasted_iota(jnp.int32, (t, t), 0)
    ti = lax.broadcasted_iota(jnp.int32, (t, t), 1)
    hi, mid, lo = _split3(jnp.concatenate(lf, axis=0))
    lhs = jnp.concatenate([hi, mid, lo, jnp.zeros((10, t), BF16)], axis=0)
    pref = _dot(lhs, (si <= ti).astype(BF16))
    pre_f = pref[0:1] + pref[2:3] + pref[4:5]
    pre_b = pref[1:2] + pref[3:4] + pref[5:6]
    b_end = [pre_f[:, t - 1:t], pre_b[:, t - 1:t]]
    b = [pre_f, b_end[1] - pre_b + lf[1]]
    s_t = _dot_nt(k, q)
    v_t = jnp.transpose(v.astype(BF16))
    one = jnp.ones((1, t), BF16)
    out = []
    for dirn in range(2):
        g = li[dirn] - b[dirn]
        a = b_end[dirn] + g
        gh, gm, gl = _split3(g)
        bh, bm, bl = _split3(b[dirn])
        zpad = jnp.zeros((10, t), BF16)
        d_t = _dot_tn(jnp.concatenate([gh, gm, gl, one, one, one, zpad], axis=0),
                      jnp.concatenate([one, one, one, bh, bm, bl, zpad], axis=0))
        d_t = jnp.where((si <= ti) if dirn == 0 else (si >= ti), d_t, MASKED)
        m_loc = jnp.max(d_t, axis=0, keepdims=True)
        p_t = s_t * jnp.exp(d_t - m_loc)
        den = jnp.sum(p_t, axis=0, keepdims=True)
        nl_t = _dot(v_t, p_t.astype(BF16))
        rows = jnp.concatenate([m_loc, den, b[dirn], a, jnp.broadcast_to(b_end[dirn], (1, t)),
                                jnp.zeros((ROWS_PER_DIR - 5, t), F32)], axis=0)
        out.append((nl_t, rows))
    return v_t, out


def _mlstm_state(dirn, q, k, v_t, nl_t, rows, s_ref, m_ref):
    t = q.shape[0]
    m_loc, den_loc, b, a = rows[0:1], rows[1:2], rows[2:3], rows[3:4]
    b_end = rows[4:5, 0:1]
    m_prev = m_ref[dirn]
    s_old = s_ref[dirn]
    m_new = jnp.maximum(b_end + m_prev, jnp.max(a, axis=1, keepdims=True))
    decay = jnp.exp(b_end + m_prev - m_new)
    w = jnp.exp(a - m_new)
    lhs = jnp.concatenate([(v_t.astype(F32) * w).astype(BF16),
                           jnp.broadcast_to(w, (STATE_ROWS - M_DV, t)).astype(BF16)], axis=0)
    inter = _dot_nt(s_old.astype(BF16), q)
    m_inter = b + m_prev
    m_t = jnp.maximum(m_inter, m_loc)
    w_inter = jnp.exp(m_inter - m_t)
    w_loc = jnp.exp(m_loc - m_t)
    den = w_inter * inter[M_DV:M_DV + 1] + w_loc * den_loc
    scale = 1.0 / jnp.maximum(jnp.abs(den), jnp.exp(-m_t))
    h_t = (w_inter * scale) * inter[:M_DV] + (w_loc * scale) * nl_t
    s_ref[dirn] = decay * s_old + _dot(lhs, k)
    m_ref[dirn] = m_new
    return h_t


def _mlstm_kernel(q_ref, qt_ref, k_ref, kt_ref, v_ref, vt_ref, g_ref, gt_ref, o_ref, ot_ref,
                  vt_s, nl_s, rows_s, ht_s, s_ref, m_ref):
    s_ref[...] = jnp.zeros_like(s_ref)
    m_ref[...] = jnp.zeros_like(m_ref)
    ht_s[...] = jnp.zeros_like(ht_s)
    tail_cols = pl.ds(SEQ, TAIL)

    def chunk(c):
        r0 = pl.multiple_of(c * MT, MT)
        return pl.ds(r0, MT)

    def keep_local(cols, v_t, per_dir):
        vt_s[:, cols] = v_t
        for dirn, (nl_t, rows) in enumerate(per_dir):
            nl_s[dirn, :, cols] = nl_t
            rows_s[dirn, :, cols] = rows

    def local_pass(c, carry):
        cols = chunk(c)
        keep_local(cols, *_mlstm_local(q_ref[cols, :], k_ref[cols, :], v_ref[cols, :], g_ref[:, cols], False))
        return carry

    def state_step(dirn, q, k, cols):
        ht_s[:, cols] += _mlstm_state(dirn, q, k, vt_s[:, cols], nl_s[dirn, :, cols],
                                      rows_s[dirn, :, cols], s_ref, m_ref)

    def state_pass(i, carry):
        cf = chunk(i)
        state_step(0, q_ref[cf, :], k_ref[cf, :], cf)
        cb = chunk(N_MCHUNK - 1 - i)
        state_step(1, q_ref[cb, :], k_ref[cb, :], cb)
        return carry

    def emit(c, carry):
        cols = chunk(c)
        o_ref[cols, :] = jnp.transpose(ht_s[:, cols])
        return carry

    keep_local(tail_cols, *_mlstm_local(qt_ref[...], kt_ref[...], vt_ref[...], gt_ref[...], True))
    lax.fori_loop(0, N_MCHUNK, local_pass, 0, unroll=4)
    state_step(0, qt_ref[...], kt_ref[...], tail_cols)
    lax.fori_loop(0, N_MCHUNK, state_pass, 0)
    state_step(1, qt_ref[...], kt_ref[...], tail_cols)
    lax.fori_loop(0, N_MCHUNK, emit, 0)
    ot_ref[...] = jnp.transpose(ht_s[:, tail_cols])


def _mlstm(q, k, xm, grow):
    tail_blk = N_REAL // TAIL
    real = lambda w: pl.BlockSpec((SEQ, w), lambda b, h: (b, h))
    tail = lambda w: pl.BlockSpec((TAIL, w), lambda b, h: (tail_blk + b, h))
    lp = SEQ + TAIL
    return pl.pallas_call(
        _mlstm_kernel,
        grid=(BATCH, M_HEADS),
        in_specs=[real(M_DK), tail(M_DK), real(M_DK), tail(M_DK), real(M_DV), tail(M_DV),
                  pl.BlockSpec((GATE_PAD, SEQ), lambda b, h: (h, b)),
                  pl.BlockSpec((GATE_PAD, TAIL), lambda b, h: (h, tail_blk + b))],
        out_specs=[pl.BlockSpec((SEQ, M_DV), lambda b, h: (b, h)),
                   pl.BlockSpec((TAIL, M_DV), lambda b, h: (b, h))],
        out_shape=[jax.ShapeDtypeStruct((N_REAL, M_INNER), F32),
                   jax.ShapeDtypeStruct((BATCH * TAIL, M_INNER), F32)],
        scratch_shapes=[pltpu.VMEM((M_DV, lp), BF16),
                        pltpu.VMEM((2, M_DV, lp), F32),
                        pltpu.VMEM((2, ROWS_PER_DIR, lp), F32),
                        pltpu.VMEM((M_DV, lp), F32),
                        pltpu.VMEM((2, STATE_ROWS, M_DK), F32),
                        pltpu.VMEM((2, 1, 1), F32)],
        compiler_params=_params("parallel", "parallel"),
        name="mlstm",
    )(q, q, k, k, xm, xm, grow, grow)


PAIR = 2 * NA_DH


def _natten_kernel(q_ref, qt_ref, k_ref, kt_ref, v_ref, vt_ref, bias_ref, o_ref, ot_ref):
    lane = lax.broadcasted_iota(jnp.int32, (1, PAIR), 1)
    first = lane < NA_DH
    k_tail = kt_ref[...]
    v_tail = vt_ref[...]

    def attend(q, keys, vals, bias_t):
        n = q.shape[0]
        zero = jnp.zeros_like(q)
        qs = jnp.concatenate([jnp.where(first, q, zero), jnp.where(first, zero, q)], axis=0)
        s = _dot_nt(keys, qs) + bias_t
        e = jnp.exp2(s - jnp.max(s, axis=0, keepdims=True))
        o = _dot(jnp.transpose(e.astype(BF16)), vals)
        inv = jnp.broadcast_to(1.0 / jnp.sum(e, axis=0, keepdims=True), (PAIR, 2 * n))
        o = o * jnp.transpose(inv)
        return jnp.where(first, o[:n], o[n:])

    def row_block(r, carry):
        r0 = jnp.clip(r - NA_WIN_H // 2, 0, ROWS - NA_WIN_H)
        qrows = pl.ds(pl.multiple_of(r * GRID_W, GRID_W), GRID_W)
        krows = pl.ds(pl.multiple_of(r0 * GRID_W, GRID_W), NA_WIN_H * GRID_W)
        keys = jnp.concatenate([k_ref[krows, :], k_tail], axis=0)
        vals = jnp.concatenate([v_ref[krows, :], v_tail], axis=0)
        o = attend(q_ref[qrows, :], keys, vals, bias_ref[r - r0])
        o_ref[qrows, :] = o.astype(o_ref.dtype)
        return carry

    lax.fori_loop(0, ROWS, row_block, 0, unroll=8)
    tb = bias_ref[0, NA_WIN_H * GRID_W:, :]
    tail_bias = jnp.concatenate([jnp.broadcast_to(tb[:, 0:1], (TAIL, TAIL)),
                                 jnp.broadcast_to(tb[:, NA_DH:NA_DH + 1], (TAIL, TAIL))], axis=1)
    ot_ref[...] = attend(qt_ref[...], k_tail, v_tail, tail_bias).astype(ot_ref.dtype)


def _natten(qn, kn, vn, bias):
    tail_blk = N_REAL // TAIL
    real = pl.BlockSpec((SEQ, PAIR), lambda b, p: (b, p))
    tail = pl.BlockSpec((TAIL, PAIR), lambda b, p: (tail_blk + b, p))
    return pl.pallas_call(
        _natten_kernel,
        grid=(BATCH, NA_HEADS // 2),
        in_specs=[real, tail, real, tail, real, tail,
                  pl.BlockSpec((None, NA_WIN_H, NA_KEYS, PAIR), lambda b, p: (p, 0, 0, 0))],
        out_specs=[pl.BlockSpec((SEQ, PAIR), lambda b, p: (b, p)),
                   pl.BlockSpec((TAIL, PAIR), lambda b, p: (b, p))],
        out_shape=[jax.ShapeDtypeStruct((N_REAL, NA_INNER), BF16),
                   jax.ShapeDtypeStruct((BATCH * TAIL, NA_INNER), BF16)],
        compiler_params=_params("parallel", "parallel"),
        name="natten",
    )(qn, qn, kn, kn, vn, vn, bias)


def _natten_bias(rpb, meta_bias):
    qc = jnp.arange(GRID_W)
    kc = jnp.arange(GRID_W)
    win0 = jnp.clip(qc - NA_WIN_W // 2, 0, GRID_W - NA_WIN_W)
    ok = (kc[None, :] >= win0[:, None]) & (kc[None, :] < win0[:, None] + NA_WIN_W)
    dc = jnp.clip(kc[None, :] - qc[:, None], -(NA_WIN_W - 1), NA_WIN_W - 1) + NA_WIN_W - 1
    onehot = (dc[None] == jnp.arange(2 * NA_WIN_W - 1)[:, None, None]).astype(F32)
    t1 = jnp.einsum('hdj,jqk->hdqk', rpb.astype(F32), onehot, precision=lax.Precision.HIGHEST)
    t1 = jnp.where(ok[None, None], t1, MASKED)
    loc = jnp.stack([t1[:, NA_WIN_H - 1 - dl:2 * NA_WIN_H - 1 - dl] for dl in range(NA_WIN_H)], axis=1)
    loc = jnp.transpose(loc, (0, 1, 3, 2, 4))
    loc = loc.reshape(NA_HEADS, NA_WIN_H, GRID_W, NA_WIN_H * GRID_W)
    met = jnp.concatenate([jnp.full((NA_HEADS, NPAD), MASKED, F32), meta_bias.astype(F32)], axis=1)
    met = jnp.broadcast_to(met[:, None, None, :], (NA_HEADS, NA_WIN_H, GRID_W, TAIL))
    bias = jnp.concatenate([loc, met], axis=-1)
    bias = bias.reshape(NA_HEADS // 2, 2, NA_WIN_H, GRID_W, NA_KEYS)
    bias = jnp.transpose(bias, (0, 2, 4, 1, 3)).reshape(NA_HEADS // 2, NA_WIN_H, NA_KEYS, PAIR)
    return bias * LOG2E


def _merge_kernel(hsr_ref, hst_ref, ybr_ref, ybt_ref, xr_ref, xt_ref, xc_ref, og_ref, gab_ref,
                  ng_ref, sk_ref, wa_ref, wb_ref, wo_ref, o_ref):
    def body(hs_ref, yb_ref, h_ref):
        hs = hs_ref[...]
        parts = []
        for hd in range(M_HEADS):
            sl = hs[:, hd * M_DV:(hd + 1) * M_DV]
            parts.append(sl * lax.rsqrt(jnp.mean(sl * sl, axis=-1, keepdims=True) + EPS))
        hn = jnp.concatenate(parts, axis=1) * ng_ref[...]
        y_a = _sigmoid(og_ref[...].astype(F32)) * (hn + sk_ref[...] * xc_ref[...].astype(F32))
        gab = gab_ref[...].astype(F32)
        mix = (_sigmoid(gab[:, :D_MODEL]) * _dot(y_a.astype(BF16), wa_ref[...])
               + _sigmoid(gab[:, D_MODEL:]) * _dot(yb_ref[...], wb_ref[...]))
        o_ref[...] = h_ref[...] + _dot(mix.astype(BF16), wo_ref[...])

    _real_or_tail(body, hsr_ref, hst_ref, ybr_ref, ybt_ref, xr_ref, xt_ref)


def _merge(hs_real, hs_tail, yb_real, yb_tail, x2d, tailh, xc, og, gab, ng, sk, wa, wb, wo):
    return pl.pallas_call(
        _merge_kernel,
        grid=(N_TILES,),
        in_specs=[_real_spec(M_INNER), _tail_spec(M_INNER), _real_spec(NA_INNER), _tail_spec(NA_INNER),
                  _real_spec(D_MODEL), _tail_spec(D_MODEL),
                  _row_spec(M_INNER), _row_spec(M_INNER), _row_spec(2 * D_MODEL)]
                 + [_const_spec(a.shape) for a in (ng, sk, wa, wb, wo)],
        out_specs=_row_spec(D_MODEL),
        out_shape=jax.ShapeDtypeStruct((NT, D_MODEL), F32),
        compiler_params=_params("parallel"),
        name="merge",
    )(hs_real, hs_tail, yb_real, yb_tail, x2d, tailh, xc, og, gab, ng, sk, wa, wb, wo)


def _ffn_kernel(h_ref, g2_ref, w1_ref, w2_ref, or_ref, ot_ref):
    def body(o_ref):
        h = h_ref[...]
        xn = (h * lax.rsqrt(jnp.mean(h * h, axis=-1, keepdims=True) + EPS) * g2_ref[...]).astype(BF16)
        z = jnp.maximum(_dot(xn, w1_ref[...]), 0.0)
        o_ref[...] = h + _dot((z * z).astype(BF16), w2_ref[...])

    _real_or_tail(body, or_ref, ot_ref)


def _ffn(h1, g2, w1, w2):
    return pl.pallas_call(
        _ffn_kernel,
        grid=(N_TILES,),
        in_specs=[_row_spec(D_MODEL), _const_spec(g2.shape), _const_spec(w1.shape), _const_spec(w2.shape)],
        out_specs=[_real_spec(D_MODEL), _tail_spec(D_MODEL)],
        out_shape=[jax.ShapeDtypeStruct((N_REAL, D_MODEL), F32),
                   jax.ShapeDtypeStruct((BATCH * TAIL, D_MODEL), F32)],
        compiler_params=_params("arbitrary"),
        name="ffn",
    )(h1, g2, w1, w2)


def kernel(x, meta_tokens, norm1_g, w_in, mlstm_conv_w, mlstm_conv_b, mlstm_wq, mlstm_wk, mlstm_gate_b, mlstm_norm_g, mlstm_skip, na_q_norm_g, na_k_norm_g, na_rpb, na_meta_bias, w_branch_a, w_branch_b, w_out, norm2_g, w_ff1, w_ff2):
    off_mo = M_INNER
    off_mg = off_mo + M_INNER
    off_q = off_mg + 4 * M_HEADS
    off_v = off_q + 2 * NA_INNER
    off_g = off_v + NA_INNER

    x2d = x.astype(F32).reshape(N_REAL, D_MODEL)
    tail = jnp.concatenate([jnp.zeros((NPAD, D_MODEL), F32), meta_tokens.astype(F32)], axis=0)
    tailh = jnp.tile(tail, (BATCH, 1))

    wb16 = w_in.astype(BF16)
    pad_gates = lambda a: jnp.pad(
        jnp.swapaxes(a.reshape(a.shape[0], 4, M_HEADS), 1, 2),
        ((0, 0), (0, 0), (0, GATE_PAD - 4))).reshape(a.shape[0], M_HEADS * GATE_PAD)
    wg = jnp.pad(pad_gates(wb16[:, off_mg:off_q]), ((0, 0), (0, 128 - M_HEADS * GATE_PAD)))
    gb = jnp.pad(pad_gates(mlstm_gate_b.astype(F32).reshape(1, 4 * M_HEADS)),
                 ((0, 0), (0, 128 - M_HEADS * GATE_PAD)))
    bd =jnp.kron(jnp.eye(NA_HEADS, dtype=F32), jnp.full((NA_DH, NA_DH), 1.0 / NA_DH, F32)).astype(BF16)
    qg = jnp.tile(na_q_norm_g.astype(F32), NA_HEADS)[None, :] * (NA_DH ** -0.5 * LOG2E)
    kg = jnp.tile(na_k_norm_g.astype(F32), NA_HEADS)[None, :]

    xm, og, grow, qn, kn, vn, gab = _inproj(
        x2d, tailh, norm1_g.astype(F32)[None, :], wb16[:, :off_mo], wb16[:, off_mo:off_mg], wg,
        wb16[:, off_q:off_g], wb16[:, off_g:], gb, qg, kg, bd)

    wqk = jnp.concatenate([mlstm_wq, mlstm_wk], axis=-1).astype(BF16)
    xc, qm, km = _conv(xm, mlstm_conv_w.astype(F32).reshape(3, M_INNER),
                       mlstm_conv_b.astype(F32)[None, :], wqk)

    hs_real, hs_tail = _mlstm(qm, km, xm, grow)
    yb_real, yb_tail = _natten(qn, kn, vn, _natten_bias(na_rpb, na_meta_bias))

    h1 = _merge(hs_real, hs_tail, yb_real, yb_tail, x2d, tailh, xc, og, gab,
                mlstm_norm_g.astype(F32).reshape(1, M_INNER), mlstm_skip.astype(F32)[None, :],
                w_branch_a.astype(BF16), w_branch_b.astype(BF16), w_out.astype(BF16))
    out_real, _ = _ffn(h1, norm2_g.astype(F32)[None, :], w_ff1.astype(BF16), w_ff2.astype(BF16))
    return out_real.reshape(BATCH, SEQ, D_MODEL)
```

```python
import functools

import jax
import jax.numpy as jnp
from jax import lax
from jax.experimental import pallas as pl
from jax.experimental.pallas import tpu as pltpu

D_MODEL = 1024
BATCH = 4
SEQ = 4096
N_META = 16
GRID_W = 64
ROWS = SEQ // GRID_W
M_HEADS = 4
M_DV = 256
M_DK = 128
M_INNER = M_HEADS * M_DV
NA_HEADS = 8
NA_DH = 64
NA_INNER = NA_HEADS * NA_DH
NA_WIN_H = 8
NA_WIN_W = 16
D_FF = 4 * D_MODEL
EPS = 1e-6
NEG_LOG_GATE = -1e9
MASKED = -1e30
LOG2E = 1.4426950408889634

TAIL = 128
NPAD = TAIL - N_META
N_REAL = BATCH * SEQ
NT = N_REAL + BATCH * TAIL
TM = 512
N_TILES = NT // TM
N_REAL_TILES = N_REAL // TM
NA_KEYS = NA_WIN_H * GRID_W + TAIL

VMEM_LIMIT = 56 * 1024 * 1024

F32 = jnp.float32
BF16 = jnp.bfloat16


def _dot(a, b):
    return jnp.dot(a, b, preferred_element_type=F32)


def _dot_nt(a, b):
    return lax.dot_general(a, b, (((1,), (1,)), ((), ())), preferred_element_type=F32)


def _dot_tn(a, b):
    return lax.dot_general(a, b, (((0,), (0,)), ((), ())), preferred_element_type=F32)


def _sigmoid(z):
    return 0.5 * jnp.tanh(0.5 * z) + 0.5


def _log_sigmoid(z):
    return jnp.minimum(z, 0.0) - jnp.log1p(jnp.exp(-jnp.abs(z)))


def _const_spec(shape):
    nd = len(shape)
    return pl.BlockSpec(shape, lambda *_: (0,) * nd, pipeline_mode=pl.Buffered(1))


def _params(*sem):
    return pltpu.CompilerParams(dimension_semantics=sem, vmem_limit_bytes=VMEM_LIMIT)


GATE_PAD = 8


def _inproj_kernel(x_ref, tail_ref, g1_ref, wx_ref, wo_ref, wg_ref, wqkv_ref, wgab_ref, gb_ref,
                   qg_ref, kg_ref, bd_ref,
                   xm_ref, og_ref, grow_ref, q_ref, k_ref, v_ref, gab_ref):
    def body(h_ref):
        h = h_ref[...]
        ms = jnp.mean(h * h, axis=-1, keepdims=True)
        xn = (h * lax.rsqrt(ms + EPS) * g1_ref[...]).astype(BF16)
        xm_ref[...] = _dot(xn, wx_ref[...])
        og_ref[...] = _dot(xn, wo_ref[...]).astype(BF16)
        gates = _dot(xn, wg_ref[...]) + gb_ref[...]
        grow_ref[...] = jnp.transpose(gates)[:M_HEADS * GATE_PAD, :]
        gab_ref[...] = _dot(xn, wgab_ref[...]).astype(BF16)
        qkv = _dot(xn, wqkv_ref[...])
        uq = qkv[:, :NA_INNER]
        uk = qkv[:, NA_INNER:2 * NA_INNER]
        msq = _dot((uq * uq).astype(BF16), bd_ref[...])
        msk = _dot((uk * uk).astype(BF16), bd_ref[...])
        q_ref[...] = (uq * lax.rsqrt(msq + EPS) * qg_ref[...]).astype(BF16)
        k_ref[...] = (uk * lax.rsqrt(msk + EPS) * kg_ref[...]).astype(BF16)
        v_ref[...] = qkv[:, 2 * NA_INNER:].astype(BF16)

    _real_or_tail(body, x_ref, tail_ref)


def _real_or_tail(body, *ref_pairs):
    i = pl.program_id(0)
    pl.when(i < N_REAL_TILES)(lambda: body(*ref_pairs[0::2]))
    pl.when(i == N_REAL_TILES)(lambda: body(*ref_pairs[1::2]))


def _real_spec(width):
    return pl.BlockSpec((TM, width), lambda i: (jnp.minimum(i, N_REAL_TILES - 1), 0))


def _tail_spec(width):
    return pl.BlockSpec((TM, width), lambda i: (0, 0))


def _row_spec(width):
    return pl.BlockSpec((TM, width), lambda i: (i, 0))


def _inproj(x2d, tailh, g1, wx, wo, wg, wqkv, wgab, gb, qg, kg, bd):
    return pl.pallas_call(
        _inproj_kernel,
        grid=(N_TILES,),
        in_specs=[_real_spec(D_MODEL), _tail_spec(D_MODEL)]
                 + [_const_spec(a.shape) for a in (g1, wx, wo, wg, wqkv, wgab, gb, qg, kg, bd)],
        out_specs=[_row_spec(M_INNER), _row_spec(M_INNER),
                   pl.BlockSpec((M_HEADS * GATE_PAD, TM), lambda i: (0, i)),
                   _row_spec(NA_INNER), _row_spec(NA_INNER), _row_spec(NA_INNER), _row_spec(2 * D_MODEL)],
        out_shape=[jax.ShapeDtypeStruct((NT, M_INNER), F32),
                   jax.ShapeDtypeStruct((NT, M_INNER), BF16),
                   jax.ShapeDtypeStruct((M_HEADS * GATE_PAD, NT), F32),
                   jax.ShapeDtypeStruct((NT, NA_INNER), BF16),
                   jax.ShapeDtypeStruct((NT, NA_INNER), BF16),
                   jax.ShapeDtypeStruct((NT, NA_INNER), BF16),
                   jax.ShapeDtypeStruct((NT, 2 * D_MODEL), BF16)],
        compiler_params=_params("parallel"),
        name="inproj",
    )(x2d, tailh, g1, wx, wo, wg, wqkv, wgab, gb, qg, kg, bd)


CT = 128
N_CT = SEQ // CT
HALO = 8


def _conv_kernel(x_ref, prev_ref, next_ref, cw_ref, cb_ref, wqk_ref, xc_ref, q_ref, k_ref):
    x = x_ref[...]
    rows = lax.broadcasted_iota(jnp.int32, (CT, M_INNER), 0)
    x_m1 = jnp.where(rows == 0, prev_ref[HALO - 1:HALO, :], pltpu.roll(x, 1, axis=0))
    x_p1 = jnp.where(rows == CT - 1, next_ref[0:1, :], pltpu.roll(x, CT - 1, axis=0))
    z = cw_ref[0:1, :] * x_m1 + cw_ref[1:2, :] * x + cw_ref[2:3, :] * x_p1 + cb_ref[...]
    xc = z * _sigmoid(z)
    xc_ref[...] = xc.astype(xc_ref.dtype)
    for hd in range(M_HEADS):
        qk = _dot(xc[:, hd * M_DV:(hd + 1) * M_DV].astype(BF16), wqk_ref[hd])
        q_ref[:, hd * M_DK:(hd + 1) * M_DK] = qk[:, :M_DK].astype(BF16)
        k_ref[:, hd * M_DK:(hd + 1) * M_DK] = (qk[:, M_DK:] * (M_DK ** -0.5)).astype(BF16)


def _conv(xm, cw, cb, wqk):
    real_blocks = SEQ // HALO
    tail_blocks = TAIL // HALO
    tail0 = N_REAL // HALO

    def main_map(b, j):
        return (jnp.where(j < N_CT, b * N_CT + j, N_REAL // CT + b), 0)

    def prev_map(b, j):
        inner = b * real_blocks + j * (CT // HALO) - 1
        first = tail0 + b * tail_blocks + tail_blocks - 1
        tail = b * real_blocks + real_blocks - 1
        return (jnp.where(j == 0, first, jnp.where(j == N_CT, tail, inner)), 0)

    def next_map(b, j):
        inner = b * real_blocks + (j + 1) * (CT // HALO)
        last = tail0 + b * tail_blocks
        tail = b * real_blocks
        return (jnp.where(j == N_CT - 1, last, jnp.where(j == N_CT, tail, inner)), 0)

    return pl.pallas_call(
        _conv_kernel,
        grid=(BATCH, N_CT + 1),
        in_specs=[pl.BlockSpec((CT, M_INNER), main_map),
                  pl.BlockSpec((HALO, M_INNER), prev_map),
                  pl.BlockSpec((HALO, M_INNER), next_map),
                  _const_spec(cw.shape), _const_spec(cb.shape), _const_spec(wqk.shape)],
        out_specs=[pl.BlockSpec((CT, M_INNER), main_map),
                   pl.BlockSpec((CT, M_HEADS * M_DK), main_map),
                   pl.BlockSpec((CT, M_HEADS * M_DK), main_map)],
        out_shape=[jax.ShapeDtypeStruct((NT, M_INNER), BF16),
                   jax.ShapeDtypeStruct((NT, M_HEADS * M_DK), BF16),
                   jax.ShapeDtypeStruct((NT, M_HEADS * M_DK), BF16)],
        compiler_params=_params("parallel", "parallel"),
        name="conv_qk",
    )(xm, xm, xm, cw, cb, wqk)


MT = 256
N_MCHUNK = SEQ // MT
STATE_ROWS = M_DV + 16
ROWS_PER_DIR = 8
LOCAL_GROUP = 8
STATE_GROUP = 4


def _split3(x):
    hi = x.astype(BF16)
    r1 = x - hi.astype(F32)
    mid = r1.astype(BF16)
    lo = (r1 - mid.astype(F32)).astype(BF16)
    return hi, mid, lo


def _mlstm_local_start(q, k, v, g8, is_tail):
    t = q.shape[0]
    li = [g8[0:1], g8[2:3]]
    lf = [_log_sigmoid(g8[1:2]), _log_sigmoid(g8[3:4])]
    if is_tail:
        pad = lax.broadcasted_iota(jnp.int32, (1, t), 1) < NPAD
        li = [jnp.where(pad, NEG_LOG_GATE, x) for x in li]
        lf = [jnp.where(pad, 0.0, x) for x in lf]
    si = lax.broadcasted_iota(jnp.int32, (t, t), 0)
    ti = lax.broadcasted_iota(jnp.int32, (t, t), 1)
    hi, mid, lo = _split3(jnp.concatenate(lf, axis=0))
    lhs = jnp.concatenate([hi, mid, lo, jnp.zeros((10, t), BF16)], axis=0)
    pref = _dot(lhs, (si <= ti).astype(BF16))
    pre_f = pref[0:1] + pref[2:3] + pref[4:5]
    pre_b = pref[1:2] + pref[3:4] + pref[5:6]
    b_end = [pre_f[:, t - 1:t], pre_b[:, t - 1:t]]
    b = [pre_f, b_end[1] - pre_b + lf[1]]
    s_t = _dot_nt(k, q)
    v_tb = jnp.transpose(v.astype(BF16))
    return li, b, b_end, s_t, v_tb, k


def _mlstm_local_finish(li, b, b_end, s_t, v_tb, k):
    t = s_t.shape[0]
    si = lax.broadcasted_iota(jnp.int32, (t, t), 0)
    ti = lax.broadcasted_iota(jnp.int32, (t, t), 1)
    v_t = v_tb.astype(F32)
    out = []
    for dirn in range(2):
        g = li[dirn] - b[dirn]
        g_col = jnp.transpose(jnp.broadcast_to(g, (128, t)))
        d_t = jnp.concatenate([g_col + b[dirn][:, j:j + 128] for j in range(0, t, 128)], axis=1)
        d_t = jnp.where((si <= ti) if dirn == 0 else (si >= ti), d_t, MASKED)
        m_loc = jnp.max(d_t, axis=0, keepdims=True)
        p_t = s_t * jnp.exp(d_t - m_loc)
        den = jnp.sum(p_t, axis=0, keepdims=True)
        nl_t = _dot(v_tb, p_t.astype(BF16))
        a = b_end[dirn] + g
        a_max = jnp.max(a, axis=1, keepdims=True)
        w = jnp.exp(a - a_max)
        u = _dot(jnp.concatenate([(v_t * w).astype(BF16),
                                  jnp.broadcast_to(w, (STATE_ROWS - M_DV, t)).astype(BF16)], axis=0), k)
        rows = jnp.concatenate([m_loc, den, b[dirn], jnp.broadcast_to(a_max, (1, t)),
                                jnp.broadcast_to(b_end[dirn], (1, t)),
                                jnp.zeros((ROWS_PER_DIR - 5, t), F32)], axis=0)
        out.append((nl_t, u, rows))
    return out


def _mlstm_state_start(dirn, q, u, rows, s_ref, m_ref):
    a_max, b_end = rows[3:4, 0:1], rows[4:5, 0:1]
    m_prev = m_ref[dirn]
    s_old = s_ref[dirn]
    inter = _dot_nt(s_old.astype(BF16), q)
    m_new = jnp.maximum(b_end + m_prev, a_max)
    s_ref[dirn] = jnp.exp(b_end + m_prev - m_new) * s_old + jnp.exp(a_max - m_new) * u
    m_ref[dirn] = m_new
    return inter, m_prev


def _mlstm_state_finish(inter, m_prev, nl_t, rows):
    m_loc, den_loc, b = rows[0:1], rows[1:2], rows[2:3]
    m_inter = b + m_prev
    m_t = jnp.maximum(m_inter, m_loc)
    w_inter = jnp.exp(m_inter - m_t)
    w_loc = jnp.exp(m_loc - m_t)
    den = w_inter * inter[M_DV:M_DV + 1] + w_loc * den_loc
    scale = 1.0 / jnp.maximum(jnp.abs(den), jnp.exp(-m_t))
    return (w_inter * scale) * inter[:M_DV] + (w_loc * scale) * nl_t


def _mlstm_kernel(q_ref, qt_ref, k_ref, kt_ref, v_ref, vt_ref, g_ref, gt_ref, o_ref, ot_ref,
                  nl_s, u_s, rows_s, ht_s, s_ref, m_ref):
    s_ref[...] = jnp.zeros_like(s_ref)
    m_ref[...] = jnp.zeros_like(m_ref)
    tail_cols = pl.ds(SEQ, TAIL)

    def chunk(c):
        r0 = pl.multiple_of(c * MT, MT)
        return pl.ds(r0, MT)

    def keep_local(c, cols, per_dir):
        for dirn, (nl_t, u, rows) in enumerate(per_dir):
            nl_s[dirn, :, cols] = nl_t
            u_s[dirn, c] = u
            rows_s[dirn, :, cols] = rows

    def local_pass(i, carry):
        group = [i * LOCAL_GROUP + j for j in range(LOCAL_GROUP)]
        started = [_mlstm_local_start(q_ref[chunk(c), :], k_ref[chunk(c), :], v_ref[chunk(c), :],
                                      g_ref[:, chunk(c)], False) for c in group]
        results = [_mlstm_local_finish(*st) for st in started]
        for c, per_dir in zip(group, results):
            keep_local(c, chunk(c), per_dir)
        return carry

    def state_steps(items):
        started = [_mlstm_state_start(dirn, q, u_s[dirn, c], rows_s[dirn, :, cols], s_ref, m_ref)
                   for dirn, q, c, cols in items]
        return [_mlstm_state_finish(inter, m_prev, nl_s[dirn, :, cols], rows_s[dirn, :, cols])
                for (inter, m_prev), (dirn, _, c, cols) in zip(started, items)]

    def visit_items(i):
        chunks = []
        for j in range(STATE_GROUP):
            step = i * STATE_GROUP + j
            chunks += [(0, step), (1, N_MCHUNK - 1 - step)]
        return [(dirn, q_ref[chunk(c), :], c, chunk(c)) for dirn, c in chunks]

    def first_visits(i, carry):
        items = visit_items(i)
        for (_, _, _, cols), h_t in zip(items, state_steps(items)):
            ht_s[:, cols] = h_t
        return carry

    def second_visits(i, carry):
        items = visit_items(i)
        for (_, _, _, cols), h_t in zip(items, state_steps(items)):
            o_ref[cols, :] = jnp.transpose((ht_s[:, cols] + h_t).astype(o_ref.dtype))
        return carry

    keep_local(N_MCHUNK, tail_cols, _mlstm_local_finish(
        *_mlstm_local_start(qt_ref[...], kt_ref[...], vt_ref[...], gt_ref[...], True)))
    lax.fori_loop(0, N_MCHUNK // LOCAL_GROUP, local_pass, 0)
    ht_s[:, tail_cols] = state_steps([(0, qt_ref[...], N_MCHUNK, tail_cols)])[0]
    half = N_MCHUNK // 2 // STATE_GROUP
    lax.fori_loop(0, half, first_visits, 0)
    lax.fori_loop(half, 2 * half, second_visits, 0)
    h_tail = ht_s[:, tail_cols] + state_steps([(1, qt_ref[...], N_MCHUNK, tail_cols)])[0]
    ot_ref[...] = jnp.transpose(h_tail.astype(ot_ref.dtype))


def _mlstm(q, k, xm, grow):
    tail_blk = N_REAL // TAIL
    real = lambda w: pl.BlockSpec((SEQ, w), lambda b, h: (b, h))
    tail = lambda w: pl.BlockSpec((TAIL, w), lambda b, h: (tail_blk + b, h))
    lp = SEQ + TAIL
    return pl.pallas_call(
        _mlstm_kernel,
        grid=(BATCH, M_HEADS),
        in_specs=[real(M_DK), tail(M_DK), real(M_DK), tail(M_DK), real(M_DV), tail(M_DV),
                  pl.BlockSpec((GATE_PAD, SEQ), lambda b, h: (h, b)),
                  pl.BlockSpec((GATE_PAD, TAIL), lambda b, h: (h, tail_blk + b))],
        out_specs=[pl.BlockSpec((SEQ, M_DV), lambda b, h: (b, h)),
                   pl.BlockSpec((TAIL, M_DV), lambda b, h: (b, h))],
        out_shape=[jax.ShapeDtypeStruct((N_REAL, M_INNER), BF16),
                   jax.ShapeDtypeStruct((BATCH * TAIL, M_INNER), BF16)],
        scratch_shapes=[pltpu.VMEM((2, M_DV, lp), F32),
                        pltpu.VMEM((2, N_MCHUNK + 1, STATE_ROWS, M_DK), F32),
                        pltpu.VMEM((2, ROWS_PER_DIR, lp), F32),
                        pltpu.VMEM((M_DV, lp), F32),
                        pltpu.VMEM((2, STATE_ROWS, M_DK), F32),
                        pltpu.VMEM((2, 1, 1), F32)],
        compiler_params=_params("parallel", "parallel"),
        name="mlstm",
    )(q, q, k, k, xm, xm, grow, grow)


PAIR = 2 * NA_DH


def _natten_kernel(q_ref, qt_ref, k_ref, kt_ref, v_ref, vt_ref, bias_ref, o_ref, ot_ref):
    lane = lax.broadcasted_iota(jnp.int32, (1, PAIR), 1)
    first = lane < NA_DH
    k_tail = kt_ref[...]
    v_tail = vt_ref[...]

    def attend(q, keys, vals, bias_t):
        n = q.shape[0]
        zero = jnp.zeros_like(q)
        qs = jnp.concatenate([jnp.where(first, q, zero), jnp.where(first, zero, q)], axis=0)
        s = _dot_nt(keys, qs) + bias_t
        e = jnp.exp2(s - jnp.max(s, axis=0, keepdims=True))
        o = _dot(jnp.transpose(e.astype(BF16)), vals)
        inv = jnp.broadcast_to(1.0 / jnp.sum(e, axis=0, keepdims=True), (PAIR, 2 * n))
        o = o * jnp.transpose(inv)
        return jnp.where(first, o[:n], o[n:])

    def row_block(r, carry):
        r0 = jnp.clip(r - NA_WIN_H // 2, 0, ROWS - NA_WIN_H)
        qrows = pl.ds(pl.multiple_of(r * GRID_W, GRID_W), GRID_W)
        krows = pl.ds(pl.multiple_of(r0 * GRID_W, GRID_W), NA_WIN_H * GRID_W)
        keys = jnp.concatenate([k_ref[krows, :], k_tail], axis=0)
        vals = jnp.concatenate([v_ref[krows, :], v_tail], axis=0)
        o = attend(q_ref[qrows, :], keys, vals, bias_ref[r - r0])
        o_ref[qrows, :] = o.astype(o_ref.dtype)
        return carry

    lax.fori_loop(0, ROWS, row_block, 0, unroll=8)
    tb = bias_ref[0, NA_WIN_H * GRID_W:, :]
    tail_bias = jnp.concatenate([jnp.broadcast_to(tb[:, 0:1], (TAIL, TAIL)),
                                 jnp.broadcast_to(tb[:, NA_DH:NA_DH + 1], (TAIL, TAIL))], axis=1)
    ot_ref[...] = attend(qt_ref[...], k_tail, v_tail, tail_bias).astype(ot_ref.dtype)


def _natten(qn, kn, vn, bias):
    tail_blk = N_REAL // TAIL
    real = pl.BlockSpec((SEQ, PAIR), lambda b, p: (b, p))
    tail = pl.BlockSpec((TAIL, PAIR), lambda b, p: (tail_blk + b, p))
    return pl.pallas_call(
        _natten_kernel,
        grid=(BATCH, NA_HEADS // 2),
        in_specs=[real, tail, real, tail, real, tail,
                  pl.BlockSpec((None, NA_WIN_H, NA_KEYS, PAIR), lambda b, p: (p, 0, 0, 0))],
        out_specs=[pl.BlockSpec((SEQ, PAIR), lambda b, p: (b, p)),
                   pl.BlockSpec((TAIL, PAIR), lambda b, p: (b, p))],
        out_shape=[jax.ShapeDtypeStruct((N_REAL, NA_INNER), BF16),
                   jax.ShapeDtypeStruct((BATCH * TAIL, NA_INNER), BF16)],
        compiler_params=_params("parallel", "parallel"),
        name="natten",
    )(qn, qn, kn, kn, vn, vn, bias)


def _natten_bias(rpb, meta_bias):
    qc = jnp.arange(GRID_W)
    kc = jnp.arange(GRID_W)
    win0 = jnp.clip(qc - NA_WIN_W // 2, 0, GRID_W - NA_WIN_W)
    ok = (kc[None, :] >= win0[:, None]) & (kc[None, :] < win0[:, None] + NA_WIN_W)
    dc = jnp.clip(kc[None, :] - qc[:, None], -(NA_WIN_W - 1), NA_WIN_W - 1) + NA_WIN_W - 1
    onehot = (dc[None] == jnp.arange(2 * NA_WIN_W - 1)[:, None, None]).astype(F32)
    t1 = jnp.einsum('hdj,jqk->hdqk', rpb.astype(F32), onehot, precision=lax.Precision.HIGHEST)
    t1 = jnp.where(ok[None, None], t1, MASKED)
    loc = jnp.stack([t1[:, NA_WIN_H - 1 - dl:2 * NA_WIN_H - 1 - dl] for dl in range(NA_WIN_H)], axis=1)
    loc = jnp.transpose(loc, (0, 1, 3, 2, 4))
    loc = loc.reshape(NA_HEADS, NA_WIN_H, GRID_W, NA_WIN_H * GRID_W)
    met = jnp.concatenate([jnp.full((NA_HEADS, NPAD), MASKED, F32), meta_bias.astype(F32)], axis=1)
    met = jnp.broadcast_to(met[:, None, None, :], (NA_HEADS, NA_WIN_H, GRID_W, TAIL))
    bias = jnp.concatenate([loc, met], axis=-1)
    bias = bias.reshape(NA_HEADS // 2, 2, NA_WIN_H, GRID_W, NA_KEYS)
    bias = jnp.transpose(bias, (0, 2, 4, 1, 3)).reshape(NA_HEADS // 2, NA_WIN_H, NA_KEYS, PAIR)
    return bias * LOG2E


def _merge_kernel(hsr_ref, hst_ref, ybr_ref, ybt_ref, xr_ref, xt_ref, xc_ref, og_ref, gab_ref,
                  ng_ref, sk_ref, wa_ref, wb_ref, wo_ref, o_ref):
    def body(hs_ref, yb_ref, h_ref):
        hs = hs_ref[...].astype(F32)
        parts = []
        for hd in range(M_HEADS):
            sl = hs[:, hd * M_DV:(hd + 1) * M_DV]
            parts.append(sl * lax.rsqrt(jnp.mean(sl * sl, axis=-1, keepdims=True) + EPS))
        hn = jnp.concatenate(parts, axis=1) * ng_ref[...]
        y_a = _sigmoid(og_ref[...].astype(F32)) * (hn + sk_ref[...] * xc_ref[...].astype(F32))
        gab = gab_ref[...].astype(F32)
        mix = (_sigmoid(gab[:, :D_MODEL]) * _dot(y_a.astype(BF16), wa_ref[...])
               + _sigmoid(gab[:, D_MODEL:]) * _dot(yb_ref[...], wb_ref[...]))
        o_ref[...] = h_ref[...] + _dot(mix.astype(BF16), wo_ref[...])

    _real_or_tail(body, hsr_ref, hst_ref, ybr_ref, ybt_ref, xr_ref, xt_ref)


def _merge(hs_real, hs_tail, yb_real, yb_tail, x2d, tailh, xc, og, gab, ng, sk, wa, wb, wo):
    return pl.pallas_call(
        _merge_kernel,
        grid=(N_TILES,),
        in_specs=[_real_spec(M_INNER), _tail_spec(M_INNER), _real_spec(NA_INNER), _tail_spec(NA_INNER),
                  _real_spec(D_MODEL), _tail_spec(D_MODEL),
                  _row_spec(M_INNER), _row_spec(M_INNER), _row_spec(2 * D_MODEL)]
                 + [_const_spec(a.shape) for a in (ng, sk, wa, wb, wo)],
        out_specs=_row_spec(D_MODEL),
        out_shape=jax.ShapeDtypeStruct((NT, D_MODEL), F32),
        compiler_params=_params("parallel"),
        name="merge",
    )(hs_real, hs_tail, yb_real, yb_tail, x2d, tailh, xc, og, gab, ng, sk, wa, wb, wo)


def _ffn_kernel(h_ref, g2_ref, w1_ref, w2_ref, or_ref, ot_ref):
    def body(o_ref):
        h = h_ref[...]
        xn = (h * lax.rsqrt(jnp.mean(h * h, axis=-1, keepdims=True) + EPS) * g2_ref[...]).astype(BF16)
        z = jnp.maximum(_dot(xn, w1_ref[...]), 0.0)
        o_ref[...] = h + _dot((z * z).astype(BF16), w2_ref[...])

    _real_or_tail(body, or_ref, ot_ref)


def _ffn(h1, g2, w1, w2):
    return pl.pallas_call(
        _ffn_kernel,
        grid=(N_TILES,),
        in_specs=[_row_spec(D_MODEL), _const_spec(g2.shape), _const_spec(w1.shape), _const_spec(w2.shape)],
        out_specs=[_real_spec(D_MODEL), _tail_spec(D_MODEL)],
        out_shape=[jax.ShapeDtypeStruct((N_REAL, D_MODEL), F32),
                   jax.ShapeDtypeStruct((BATCH * TAIL, D_MODEL), F32)],
        compiler_params=_params("arbitrary"),
        name="ffn",
    )(h1, g2, w1, w2)


def kernel(x, meta_tokens, norm1_g, w_in, mlstm_conv_w, mlstm_conv_b, mlstm_wq, mlstm_wk, mlstm_gate_b, mlstm_norm_g, mlstm_skip, na_q_norm_g, na_k_norm_g, na_rpb, na_meta_bias, w_branch_a, w_branch_b, w_out, norm2_g, w_ff1, w_ff2):
    off_mo = M_INNER
    off_mg = off_mo + M_INNER
    off_q = off_mg + 4 * M_HEADS
    off_v = off_q + 2 * NA_INNER
    off_g = off_v + NA_INNER

    x2d = x.astype(F32).reshape(N_REAL, D_MODEL)
    tail = jnp.concatenate([jnp.zeros((NPAD, D_MODEL), F32), meta_tokens.astype(F32)], axis=0)
    tailh = jnp.tile(tail, (BATCH, 1))

    wb16 = w_in.astype(BF16)
    pad_gates = lambda a: jnp.pad(
        jnp.swapaxes(a.reshape(a.shape[0], 4, M_HEADS), 1, 2),
        ((0, 0), (0, 0), (0, GATE_PAD - 4))).reshape(a.shape[0], M_HEADS * GATE_PAD)
    wg = jnp.pad(pad_gates(wb16[:, off_mg:off_q]), ((0, 0), (0, 128 - M_HEADS * GATE_PAD)))
    gb = jnp.pad(pad_gates(mlstm_gate_b.astype(F32).reshape(1, 4 * M_HEADS)),
                 ((0, 0), (0, 128 - M_HEADS * GATE_PAD)))
    bd =jnp.kron(jnp.eye(NA_HEADS, dtype=F32), jnp.full((NA_DH, NA_DH), 1.0 / NA_DH, F32)).astype(BF16)
    qg = jnp.tile(na_q_norm_g.astype(F32), NA_HEADS)[None, :] * (NA_DH ** -0.5 * LOG2E)
    kg = jnp.tile(na_k_norm_g.astype(F32), NA_HEADS)[None, :]

    xm, og, grow, qn, kn, vn, gab = _inproj(
        x2d, tailh, norm1_g.astype(F32)[None, :], wb16[:, :off_mo], wb16[:, off_mo:off_mg], wg,
        wb16[:, off_q:off_g], wb16[:, off_g:], gb, qg, kg, bd)

    wqk = jnp.concatenate([mlstm_wq, mlstm_wk], axis=-1).astype(BF16)
    xc, qm, km = _conv(xm, mlstm_conv_w.astype(F32).reshape(3, M_INNER),
                       mlstm_conv_b.astype(F32)[None, :], wqk)

    hs_real, hs_tail = _mlstm(qm, km, xm, grow)
    yb_real, yb_tail = _natten(qn, kn, vn, _natten_bias(na_rpb, na_meta_bias))

    h1 = _merge(hs_real, hs_tail, yb_real, yb_tail, x2d, tailh, xc, og, gab,
                mlstm_norm_g.astype(F32).reshape(1, M_INNER), mlstm_skip.astype(F32)[None, :],
                w_branch_a.astype(BF16), w_branch_b.astype(BF16), w_out.astype(BF16))
    out_real, _ = _ffn(h1, norm2_g.astype(F32)[None, :], w_ff1.astype(BF16), w_ff2.astype(BF16))
    return out_real.reshape(BATCH, SEQ, D_MODEL)
```

```python
import functools

import jax
import jax.numpy as jnp
from jax import lax
from jax.experimental import pallas as pl
from jax.experimental.pallas import tpu as pltpu

D_MODEL = 1024
BATCH = 4
SEQ = 4096
N_META = 16
GRID_W = 64
ROWS = SEQ // GRID_W
M_HEADS = 4
M_DV = 256
M_DK = 128
M_INNER = M_HEADS * M_DV
NA_HEADS = 8
NA_DH = 64
NA_INNER = NA_HEADS * NA_DH
NA_WIN_H = 8
NA_WIN_W = 16
D_FF = 4 * D_MODEL
EPS = 1e-6
NEG_LOG_GATE = -1e9
MASKED = -1e30
LOG2E = 1.4426950408889634

TAIL = 128
NPAD = TAIL - N_META
N_REAL = BATCH * SEQ
NT = N_REAL + BATCH * TAIL
TM = 512
N_TILES = NT // TM
N_REAL_TILES = N_REAL // TM
NA_KEYS = NA_WIN_H * GRID_W + TAIL

VMEM_LIMIT = 56 * 1024 * 1024

F32 = jnp.float32
BF16 = jnp.bfloat16


def _dot(a, b):
    return jnp.dot(a, b, preferred_element_type=F32)


def _dot_nt(a, b):
    return lax.dot_general(a, b, (((1,), (1,)), ((), ())), preferred_element_type=F32)


def _dot_tn(a, b):
    return lax.dot_general(a, b, (((0,), (0,)), ((), ())), preferred_element_type=F32)


def _sigmoid(z):
    return 0.5 * jnp.tanh(0.5 * z) + 0.5


def _log_sigmoid(z):
    return jnp.minimum(z, 0.0) - jnp.log1p(jnp.exp(-jnp.abs(z)))


def _const_spec(shape):
    nd = len(shape)
    return pl.BlockSpec(shape, lambda *_: (0,) * nd, pipeline_mode=pl.Buffered(1))


def _params(*sem):
    return pltpu.CompilerParams(dimension_semantics=sem, vmem_limit_bytes=VMEM_LIMIT)


GATE_PAD = 8


HALO = 8
HALO_PAD = 16
TILES_PER_SEQ = SEQ // TM


def _inproj_kernel(x_ref, tail_ref, xprev_ref, xnext_ref, xedge_ref, g1_ref, wx_ref, wo_ref, wg_ref,
                   wqkv_ref, wgab_ref, gb_ref, qg_ref, kg_ref, bd_ref, cw_ref, cb_ref, wqk_ref,
                   vm_ref, og_ref, grow_ref, q_ref, k_ref, v_ref, gab_ref, xc_ref, qm_ref, km_ref):
    i = pl.program_id(0)

    def norm1(h):
        ms = jnp.mean(h * h, axis=-1, keepdims=True)
        return (h * lax.rsqrt(ms + EPS) * g1_ref[...]).astype(BF16)

    def body(h_ref, is_tail):
        xn = norm1(h_ref[...])
        if is_tail:
            halo = jnp.concatenate([xedge_ref[...], jnp.zeros((HALO_PAD - 2 * BATCH, D_MODEL), F32)], axis=0)
        else:
            first = i % TILES_PER_SEQ == 0
            last = i % TILES_PER_SEQ == TILES_PER_SEQ - 1
            prev = jnp.where(first, tail_ref[TAIL - 1:TAIL, :], xprev_ref[HALO - 1:HALO, :])
            nxt = jnp.where(last, tail_ref[0:1, :], xnext_ref[0:1, :])
            halo = jnp.concatenate([prev, nxt, jnp.zeros((HALO_PAD - 2, D_MODEL), F32)], axis=0)
        xm_ext = _dot(jnp.concatenate([xn, norm1(halo)], axis=0), wx_ref[...])
        xm = xm_ext[:TM]
        rows = lax.broadcasted_iota(jnp.int32, (TM, M_INNER), 0)
        x_m1 = pltpu.roll(xm, 1, axis=0)
        x_p1 = pltpu.roll(xm, TM - 1, axis=0)
        if is_tail:
            for b in range(BATCH):
                x_m1 = jnp.where(rows == b * TAIL, xm_ext[TM + b:TM + b + 1], x_m1)
                x_p1 = jnp.where(rows == b * TAIL + TAIL - 1, xm_ext[TM + BATCH + b:TM + BATCH + b + 1], x_p1)
        else:
            x_m1 = jnp.where(rows == 0, xm_ext[TM:TM + 1], x_m1)
            x_p1 = jnp.where(rows == TM - 1, xm_ext[TM + 1:TM + 2], x_p1)
        vm_ref[...] = xm.astype(BF16)
        z = cw_ref[0:1, :] * x_m1 + cw_ref[1:2, :] * xm + cw_ref[2:3, :] * x_p1 + cb_ref[...]
        xc = z * _sigmoid(z)
        xc_ref[...] = xc.astype(BF16)
        for hd in range(M_HEADS):
            qk = _dot(xc[:, hd * M_DV:(hd + 1) * M_DV].astype(BF16), wqk_ref[hd])
            qm_ref[:, hd * M_DK:(hd + 1) * M_DK] = qk[:, :M_DK].astype(BF16)
            km_ref[:, hd * M_DK:(hd + 1) * M_DK] = (qk[:, M_DK:] * (M_DK ** -0.5)).astype(BF16)
        og_ref[...] = _dot(xn, wo_ref[...]).astype(BF16)
        gates = _dot(xn, wg_ref[...]) + gb_ref[...]
        grow_ref[...] = jnp.transpose(gates)[:M_HEADS * GATE_PAD, :]
        gab_ref[...] = _dot(xn, wgab_ref[...]).astype(BF16)
        qkv = _dot(xn, wqkv_ref[...])
        uq = qkv[:, :NA_INNER]
        uk = qkv[:, NA_INNER:2 * NA_INNER]
        msq = _dot((uq * uq).astype(BF16), bd_ref[...])
        msk = _dot((uk * uk).astype(BF16), bd_ref[...])
        q_ref[...] = (uq * lax.rsqrt(msq + EPS) * qg_ref[...]).astype(BF16)
        k_ref[...] = (uk * lax.rsqrt(msk + EPS) * kg_ref[...]).astype(BF16)
        v_ref[...] = qkv[:, 2 * NA_INNER:].astype(BF16)

    pl.when(i < N_REAL_TILES)(lambda: body(x_ref, False))
    pl.when(i == N_REAL_TILES)(lambda: body(tail_ref, True))


def _real_or_tail(body, *ref_pairs):
    i = pl.program_id(0)
    pl.when(i < N_REAL_TILES)(lambda: body(*ref_pairs[0::2]))
    pl.when(i == N_REAL_TILES)(lambda: body(*ref_pairs[1::2]))


def _real_spec(width):
    return pl.BlockSpec((TM, width), lambda i: (jnp.minimum(i, N_REAL_TILES - 1), 0))


def _tail_spec(width):
    return pl.BlockSpec((TM, width), lambda i: (0, 0))


def _row_spec(width):
    return pl.BlockSpec((TM, width), lambda i: (i, 0))


def _inproj(x2d, tailh, xedge, g1, wx, wo, wg, wqkv, wgab, gb, qg, kg, bd, cw, cb, wqk):
    halo_blocks = N_REAL // HALO
    per_tile = TM // HALO
    consts = (g1, wx, wo, wg, wqkv, wgab, gb, qg, kg, bd, cw, cb, wqk)
    return pl.pallas_call(
        _inproj_kernel,
        grid=(N_TILES,),
        in_specs=[_real_spec(D_MODEL), _tail_spec(D_MODEL),
                  pl.BlockSpec((HALO, D_MODEL), lambda i: (jnp.clip(i * per_tile - 1, 0, halo_blocks - 1), 0)),
                  pl.BlockSpec((HALO, D_MODEL), lambda i: (jnp.clip((i + 1) * per_tile, 0, halo_blocks - 1), 0)),
                  _const_spec(xedge.shape)]
                 + [_const_spec(a.shape) for a in consts],
        out_specs=[_row_spec(M_INNER), _row_spec(M_INNER),
                   pl.BlockSpec((M_HEADS * GATE_PAD, TM), lambda i: (0, i)),
                   _row_spec(NA_INNER), _row_spec(NA_INNER), _row_spec(NA_INNER), _row_spec(2 * D_MODEL),
                   _row_spec(M_INNER), _row_spec(M_HEADS * M_DK), _row_spec(M_HEADS * M_DK)],
        out_shape=[jax.ShapeDtypeStruct((NT, M_INNER), BF16),
                   jax.ShapeDtypeStruct((NT, M_INNER), BF16),
                   jax.ShapeDtypeStruct((M_HEADS * GATE_PAD, NT), F32),
                   jax.ShapeDtypeStruct((NT, NA_INNER), BF16),
                   jax.ShapeDtypeStruct((NT, NA_INNER), BF16),
                   jax.ShapeDtypeStruct((NT, NA_INNER), BF16),
                   jax.ShapeDtypeStruct((NT, 2 * D_MODEL), BF16),
                   jax.ShapeDtypeStruct((NT, M_INNER), BF16),
                   jax.ShapeDtypeStruct((NT, M_HEADS * M_DK), BF16),
                   jax.ShapeDtypeStruct((NT, M_HEADS * M_DK), BF16)],
        compiler_params=_params("parallel"),
        name="inproj",
    )(x2d, tailh, x2d, x2d, xedge, *consts)


MT = 256
N_MCHUNK = SEQ // MT
STATE_ROWS = M_DV + 16
ROWS_PER_DIR = 8
LOCAL_GROUP = 8
STATE_GROUP = 4


def _split3(x):
    hi = x.astype(BF16)
    r1 = x - hi.astype(F32)
    mid = r1.astype(BF16)
    lo = (r1 - mid.astype(F32)).astype(BF16)
    return hi, mid, lo


def _mlstm_local_start(q, k, v, g8, is_tail):
    t = q.shape[0]
    li = [g8[0:1], g8[2:3]]
    lf = [_log_sigmoid(g8[1:2]), _log_sigmoid(g8[3:4])]
    if is_tail:
        pad = lax.broadcasted_iota(jnp.int32, (1, t), 1) < NPAD
        li = [jnp.where(pad, NEG_LOG_GATE, x) for x in li]
        lf = [jnp.where(pad, 0.0, x) for x in lf]
    si = lax.broadcasted_iota(jnp.int32, (t, t), 0)
    ti = lax.broadcasted_iota(jnp.int32, (t, t), 1)
    hi, mid, lo = _split3(jnp.concatenate(lf, axis=0))
    lhs = jnp.concatenate([hi, mid, lo, jnp.zeros((10, t), BF16)], axis=0)
    pref = _dot(lhs, (si <= ti).astype(BF16))
    pre_f = pref[0:1] + pref[2:3] + pref[4:5]
    pre_b = pref[1:2] + pref[3:4] + pref[5:6]
    b_end = [pre_f[:, t - 1:t], pre_b[:, t - 1:t]]
    b = [pre_f, b_end[1] - pre_b + lf[1]]
    s_t = _dot_nt(k, q)
    v_tb = jnp.transpose(v.astype(BF16))
    return li, b, b_end, s_t, v_tb, k


def _mlstm_local_finish(li, b, b_end, s_t, v_tb, k):
    t = s_t.shape[0]
    si = lax.broadcasted_iota(jnp.int32, (t, t), 0)
    ti = lax.broadcasted_iota(jnp.int32, (t, t), 1)
    v_t = v_tb.astype(F32)
    out = []
    for dirn in range(2):
        g = li[dirn] - b[dirn]
        g_col = jnp.transpose(jnp.broadcast_to(g, (128, t)))
        d_t = jnp.concatenate([g_col + b[dirn][:, j:j + 128] for j in range(0, t, 128)], axis=1)
        d_t = jnp.where((si <= ti) if dirn == 0 else (si >= ti), d_t, MASKED)
        m_loc = jnp.max(d_t, axis=0, keepdims=True)
        p_t = s_t * jnp.exp(d_t - m_loc)
        den = jnp.sum(p_t, axis=0, keepdims=True)
        nl_t = _dot(v_tb, p_t.astype(BF16))
        a = b_end[dirn] + g
        a_max = jnp.max(a, axis=1, keepdims=True)
        w = jnp.exp(a - a_max)
        u = _dot(jnp.concatenate([(v_t * w).astype(BF16),
                                  jnp.broadcast_to(w, (STATE_ROWS - M_DV, t)).astype(BF16)], axis=0), k)
        rows = jnp.concatenate([m_loc, den, b[dirn], jnp.broadcast_to(a_max, (1, t)),
                                jnp.broadcast_to(b_end[dirn], (1, t)),
                                jnp.zeros((ROWS_PER_DIR - 5, t), F32)], axis=0)
        out.append((nl_t, u, rows))
    return out


def _mlstm_state_start(dirn, q, u, rows, s_ref, m_ref):
    a_max, b_end = rows[3:4, 0:1], rows[4:5, 0:1]
    m_prev = m_ref[dirn]
    s_old = s_ref[dirn]
    inter = _dot_nt(s_old.astype(BF16), q)
    m_new = jnp.maximum(b_end + m_prev, a_max)
    s_ref[dirn] = jnp.exp(b_end + m_prev - m_new) * s_old + jnp.exp(a_max - m_new) * u
    m_ref[dirn] = m_new
    return inter, m_prev


def _mlstm_state_finish(inter, m_prev, nl_t, rows):
    m_loc, den_loc, b = rows[0:1], rows[1:2], rows[2:3]
    m_inter = b + m_prev
    m_t = jnp.maximum(m_inter, m_loc)
    w_inter = jnp.exp(m_inter - m_t)
    w_loc = jnp.exp(m_loc - m_t)
    den = w_inter * inter[M_DV:M_DV + 1] + w_loc * den_loc
    scale = 1.0 / jnp.maximum(jnp.abs(den), jnp.exp(-m_t))
    return (w_inter * scale) * inter[:M_DV] + (w_loc * scale) * nl_t


def _mlstm_kernel(q_ref, qt_ref, k_ref, kt_ref, v_ref, vt_ref, g_ref, gt_ref, o_ref, ot_ref,
                  nl_s, u_s, rows_s, ht_s, s_ref, m_ref):
    s_ref[...] = jnp.zeros_like(s_ref)
    m_ref[...] = jnp.zeros_like(m_ref)
    tail_cols = pl.ds(SEQ, TAIL)

    def chunk(c):
        r0 = pl.multiple_of(c * MT, MT)
        return pl.ds(r0, MT)

    def keep_local(c, cols, per_dir):
        for dirn, (nl_t, u, rows) in enumerate(per_dir):
            nl_s[dirn, :, cols] = nl_t
            u_s[dirn, c] = u
            rows_s[dirn, :, cols] = rows

    def local_pass(i, carry):
        group = [i * LOCAL_GROUP + j for j in range(LOCAL_GROUP)]
        started = [_mlstm_local_start(q_ref[chunk(c), :], k_ref[chunk(c), :], v_ref[chunk(c), :],
                                      g_ref[:, chunk(c)], False) for c in group]
        results = [_mlstm_local_finish(*st) for st in started]
        for c, per_dir in zip(group, results):
            keep_local(c, chunk(c), per_dir)
        return carry

    def state_steps(items):
        started = [_mlstm_state_start(dirn, q, u_s[dirn, c], rows_s[dirn, :, cols], s_ref, m_ref)
                   for dirn, q, c, cols in items]
        return [_mlstm_state_finish(inter, m_prev, nl_s[dirn, :, cols], rows_s[dirn, :, cols])
                for (inter, m_prev), (dirn, _, c, cols) in zip(started, items)]

    def visit_items(i):
        chunks = []
        for j in range(STATE_GROUP):
            step = i * STATE_GROUP + j
            chunks += [(0, step), (1, N_MCHUNK - 1 - step)]
        return [(dirn, q_ref[chunk(c), :], c, chunk(c)) for dirn, c in chunks]

    def first_visits(i, carry):
        items = visit_items(i)
        for (_, _, _, cols), h_t in zip(items, state_steps(items)):
            ht_s[:, cols] = h_t
        return carry

    def second_visits(i, carry):
        items = visit_items(i)
        for (_, _, _, cols), h_t in zip(items, state_steps(items)):
            o_ref[cols, :] = jnp.transpose((ht_s[:, cols] + h_t).astype(o_ref.dtype))
        return carry

    keep_local(N_MCHUNK, tail_cols, _mlstm_local_finish(
        *_mlstm_local_start(qt_ref[...], kt_ref[...], vt_ref[...], gt_ref[...], True)))
    lax.fori_loop(0, N_MCHUNK // LOCAL_GROUP, local_pass, 0)
    ht_s[:, tail_cols] = state_steps([(0, qt_ref[...], N_MCHUNK, tail_cols)])[0]
    half = N_MCHUNK // 2 // STATE_GROUP
    lax.fori_loop(0, half, first_visits, 0)
    lax.fori_loop(half, 2 * half, second_visits, 0)
    h_tail = ht_s[:, tail_cols] + state_steps([(1, qt_ref[...], N_MCHUNK, tail_cols)])[0]
    ot_ref[...] = jnp.transpose(h_tail.astype(ot_ref.dtype))


def _mlstm(q, k, xm, grow):
    tail_blk = N_REAL // TAIL
    real = lambda w: pl.BlockSpec((SEQ, w), lambda b, h: (b, h))
    tail = lambda w: pl.BlockSpec((TAIL, w), lambda b, h: (tail_blk + b, h))
    lp = SEQ + TAIL
    return pl.pallas_call(
        _mlstm_kernel,
        grid=(BATCH, M_HEADS),
        in_specs=[real(M_DK), tail(M_DK), real(M_DK), tail(M_DK), real(M_DV), tail(M_DV),
                  pl.BlockSpec((GATE_PAD, SEQ), lambda b, h: (h, b)),
                  pl.BlockSpec((GATE_PAD, TAIL), lambda b, h: (h, tail_blk + b))],
        out_specs=[pl.BlockSpec((SEQ, M_DV), lambda b, h: (b, h)),
                   pl.BlockSpec((TAIL, M_DV), lambda b, h: (b, h))],
        out_shape=[jax.ShapeDtypeStruct((N_REAL, M_INNER), BF16),
                   jax.ShapeDtypeStruct((BATCH * TAIL, M_INNER), BF16)],
        scratch_shapes=[pltpu.VMEM((2, M_DV, lp), F32),
                        pltpu.VMEM((2, N_MCHUNK + 1, STATE_ROWS, M_DK), F32),
                        pltpu.VMEM((2, ROWS_PER_DIR, lp), F32),
                        pltpu.VMEM((M_DV, lp), F32),
                        pltpu.VMEM((2, STATE_ROWS, M_DK), F32),
                        pltpu.VMEM((2, 1, 1), F32)],
        compiler_params=_params("parallel", "parallel"),
        name="mlstm",
    )(q, q, k, k, xm, xm, grow, grow)


PAIR = 2 * NA_DH


def _natten_kernel(q_ref, qt_ref, k_ref, kt_ref, v_ref, vt_ref, bias_ref, o_ref, ot_ref):
    lane = lax.broadcasted_iota(jnp.int32, (1, PAIR), 1)
    first = lane < NA_DH
    k_tail = kt_ref[...]
    v_tail = vt_ref[...]

    def attend(q, keys, vals, bias_t):
        n = q.shape[0]
        zero = jnp.zeros_like(q)
        qs = jnp.concatenate([jnp.where(first, q, zero), jnp.where(first, zero, q)], axis=0)
        s = _dot_nt(keys, qs) + bias_t
        e = jnp.exp2(s - jnp.max(s, axis=0, keepdims=True))
        o = _dot(jnp.transpose(e.astype(BF16)), vals)
        inv = jnp.broadcast_to(1.0 / jnp.sum(e, axis=0, keepdims=True), (PAIR, 2 * n))
        o = o * jnp.transpose(inv)
        return jnp.where(first, o[:n], o[n:])

    def row_block(r, carry):
        r0 = jnp.clip(r - NA_WIN_H // 2, 0, ROWS - NA_WIN_H)
        qrows = pl.ds(pl.multiple_of(r * GRID_W, GRID_W), GRID_W)
        krows = pl.ds(pl.multiple_of(r0 * GRID_W, GRID_W), NA_WIN_H * GRID_W)
        keys = jnp.concatenate([k_ref[krows, :], k_tail], axis=0)
        vals = jnp.concatenate([v_ref[krows, :], v_tail], axis=0)
        o = attend(q_ref[qrows, :], keys, vals, bias_ref[r - r0])
        o_ref[qrows, :] = o.astype(o_ref.dtype)
        return carry

    lax.fori_loop(0, ROWS, row_block, 0, unroll=8)
    tb = bias_ref[0, NA_WIN_H * GRID_W:, :]
    tail_bias = jnp.concatenate([jnp.broadcast_to(tb[:, 0:1], (TAIL, TAIL)),
                                 jnp.broadcast_to(tb[:, NA_DH:NA_DH + 1], (TAIL, TAIL))], axis=1)
    ot_ref[...] = attend(qt_ref[...], k_tail, v_tail, tail_bias).astype(ot_ref.dtype)


def _natten(qn, kn, vn, bias):
    tail_blk = N_REAL // TAIL
    real = pl.BlockSpec((SEQ, PAIR), lambda b, p: (b, p))
    tail = pl.BlockSpec((TAIL, PAIR), lambda b, p: (tail_blk + b, p))
    return pl.pallas_call(
        _natten_kernel,
        grid=(BATCH, NA_HEADS // 2),
        in_specs=[real, tail, real, tail, real, tail,
                  pl.BlockSpec((None, NA_WIN_H, NA_KEYS, PAIR), lambda b, p: (p, 0, 0, 0))],
        out_specs=[pl.BlockSpec((SEQ, PAIR), lambda b, p: (b, p)),
                   pl.BlockSpec((TAIL, PAIR), lambda b, p: (b, p))],
        out_shape=[jax.ShapeDtypeStruct((N_REAL, NA_INNER), BF16),
                   jax.ShapeDtypeStruct((BATCH * TAIL, NA_INNER), BF16)],
        compiler_params=_params("parallel", "parallel"),
        name="natten",
    )(qn, qn, kn, kn, vn, vn, bias)


def _natten_bias(rpb, meta_bias):
    qc = jnp.arange(GRID_W)
    kc = jnp.arange(GRID_W)
    win0 = jnp.clip(qc - NA_WIN_W // 2, 0, GRID_W - NA_WIN_W)
    ok = (kc[None, :] >= win0[:, None]) & (kc[None, :] < win0[:, None] + NA_WIN_W)
    dc = jnp.clip(kc[None, :] - qc[:, None], -(NA_WIN_W - 1), NA_WIN_W - 1) + NA_WIN_W - 1
    onehot = (dc[None] == jnp.arange(2 * NA_WIN_W - 1)[:, None, None]).astype(F32)
    t1 = jnp.einsum('hdj,jqk->hdqk', rpb.astype(F32), onehot, precision=lax.Precision.HIGHEST)
    t1 = jnp.where(ok[None, None], t1, MASKED)
    loc = jnp.stack([t1[:, NA_WIN_H - 1 - dl:2 * NA_WIN_H - 1 - dl] for dl in range(NA_WIN_H)], axis=1)
    loc = jnp.transpose(loc, (0, 1, 3, 2, 4))
    loc = loc.reshape(NA_HEADS, NA_WIN_H, GRID_W, NA_WIN_H * GRID_W)
    met = jnp.concatenate([jnp.full((NA_HEADS, NPAD), MASKED, F32), meta_bias.astype(F32)], axis=1)
    met = jnp.broadcast_to(met[:, None, None, :], (NA_HEADS, NA_WIN_H, GRID_W, TAIL))
    bias = jnp.concatenate([loc, met], axis=-1)
    bias = bias.reshape(NA_HEADS // 2, 2, NA_WIN_H, GRID_W, NA_KEYS)
    bias = jnp.transpose(bias, (0, 2, 4, 1, 3)).reshape(NA_HEADS // 2, NA_WIN_H, NA_KEYS, PAIR)
    return bias * LOG2E


def _merge_kernel(hsr_ref, hst_ref, ybr_ref, ybt_ref, xr_ref, xt_ref, xc_ref, og_ref, gab_ref,
                  ng_ref, sk_ref, wa_ref, wb_ref, wo_ref, o_ref):
    def body(hs_ref, yb_ref, h_ref):
        hs = hs_ref[...].astype(F32)
        parts = []
        for hd in range(M_HEADS):
            sl = hs[:, hd * M_DV:(hd + 1) * M_DV]
            parts.append(sl * lax.rsqrt(jnp.mean(sl * sl, axis=-1, keepdims=True) + EPS))
        hn = jnp.concatenate(parts, axis=1) * ng_ref[...]
        y_a = _sigmoid(og_ref[...].astype(F32)) * (hn + sk_ref[...] * xc_ref[...].astype(F32))
        gab = gab_ref[...].astype(F32)
        mix = (_sigmoid(gab[:, :D_MODEL]) * _dot(y_a.astype(BF16), wa_ref[...])
               + _sigmoid(gab[:, D_MODEL:]) * _dot(yb_ref[...], wb_ref[...]))
        o_ref[...] = h_ref[...] + _dot(mix.astype(BF16), wo_ref[...])

    _real_or_tail(body, hsr_ref, hst_ref, ybr_ref, ybt_ref, xr_ref, xt_ref)


def _merge(hs_real, hs_tail, yb_real, yb_tail, x2d, tailh, xc, og, gab, ng, sk, wa, wb, wo):
    return pl.pallas_call(
        _merge_kernel,
        grid=(N_TILES,),
        in_specs=[_real_spec(M_INNER), _tail_spec(M_INNER), _real_spec(NA_INNER), _tail_spec(NA_INNER),
                  _real_spec(D_MODEL), _tail_spec(D_MODEL),
                  _row_spec(M_INNER), _row_spec(M_INNER), _row_spec(2 * D_MODEL)]
                 + [_const_spec(a.shape) for a in (ng, sk, wa, wb, wo)],
        out_specs=_row_spec(D_MODEL),
        out_shape=jax.ShapeDtypeStruct((NT, D_MODEL), F32),
        compiler_params=_params("parallel"),
        name="merge",
    )(hs_real, hs_tail, yb_real, yb_tail, x2d, tailh, xc, og, gab, ng, sk, wa, wb, wo)


def _ffn_kernel(h_ref, g2_ref, w1_ref, w2_ref, or_ref, ot_ref):
    def body(o_ref):
        h = h_ref[...]
        xn = (h * lax.rsqrt(jnp.mean(h * h, axis=-1, keepdims=True) + EPS) * g2_ref[...]).astype(BF16)
        z = jnp.maximum(_dot(xn, w1_ref[...]), 0.0)
        o_ref[...] = h + _dot((z * z).astype(BF16), w2_ref[...])

    _real_or_tail(body, or_ref, ot_ref)


def _ffn(h1, g2, w1, w2):
    return pl.pallas_call(
        _ffn_kernel,
        grid=(N_TILES,),
        in_specs=[_row_spec(D_MODEL), _const_spec(g2.shape), _const_spec(w1.shape), _const_spec(w2.shape)],
        out_specs=[_real_spec(D_MODEL), _tail_spec(D_MODEL)],
        out_shape=[jax.ShapeDtypeStruct((N_REAL, D_MODEL), F32),
                   jax.ShapeDtypeStruct((BATCH * TAIL, D_MODEL), F32)],
        compiler_params=_params("arbitrary"),
        name="ffn",
    )(h1, g2, w1, w2)


def kernel(x, meta_tokens, norm1_g, w_in, mlstm_conv_w, mlstm_conv_b, mlstm_wq, mlstm_wk, mlstm_gate_b, mlstm_norm_g, mlstm_skip, na_q_norm_g, na_k_norm_g, na_rpb, na_meta_bias, w_branch_a, w_branch_b, w_out, norm2_g, w_ff1, w_ff2):
    off_mo = M_INNER
    off_mg = off_mo + M_INNER
    off_q = off_mg + 4 * M_HEADS
    off_v = off_q + 2 * NA_INNER
    off_g = off_v + NA_INNER

    x2d = x.astype(F32).reshape(N_REAL, D_MODEL)
    tail = jnp.concatenate([jnp.zeros((NPAD, D_MODEL), F32), meta_tokens.astype(F32)], axis=0)
    tailh = jnp.tile(tail, (BATCH, 1))

    wb16 = w_in.astype(BF16)
    pad_gates = lambda a: jnp.pad(
        jnp.swapaxes(a.reshape(a.shape[0], 4, M_HEADS), 1, 2),
        ((0, 0), (0, 0), (0, GATE_PAD - 4))).reshape(a.shape[0], M_HEADS * GATE_PAD)
    wg = jnp.pad(pad_gates(wb16[:, off_mg:off_q]), ((0, 0), (0, 128 - M_HEADS * GATE_PAD)))
    gb = jnp.pad(pad_gates(mlstm_gate_b.astype(F32).reshape(1, 4 * M_HEADS)),
                 ((0, 0), (0, 128 - M_HEADS * GATE_PAD)))
    bd =jnp.kron(jnp.eye(NA_HEADS, dtype=F32), jnp.full((NA_DH, NA_DH), 1.0 / NA_DH, F32)).astype(BF16)
    qg = jnp.tile(na_q_norm_g.astype(F32), NA_HEADS)[None, :] * (NA_DH ** -0.5 * LOG2E)
    kg = jnp.tile(na_k_norm_g.astype(F32), NA_HEADS)[None, :]

    xb = x.astype(F32)
    xedge = jnp.concatenate([xb[:, SEQ - 1, :], xb[:, 0, :]], axis=0)
    wqk = jnp.concatenate([mlstm_wq, mlstm_wk], axis=-1).astype(BF16)
    vm, og, grow, qn, kn, vn, gab, xc, qm, km = _inproj(
        x2d, tailh, xedge, norm1_g.astype(F32)[None, :], wb16[:, :off_mo], wb16[:, off_mo:off_mg], wg,
        wb16[:, off_q:off_g], wb16[:, off_g:], gb, qg, kg, bd,
        mlstm_conv_w.astype(F32).reshape(3, M_INNER), mlstm_conv_b.astype(F32)[None, :], wqk)

    hs_real, hs_tail = _mlstm(qm, km, vm, grow)
    yb_real, yb_tail = _natten(qn, kn, vn, _natten_bias(na_rpb, na_meta_bias))

    h1 = _merge(hs_real, hs_tail, yb_real, yb_tail, x2d, tailh, xc, og, gab,
                mlstm_norm_g.astype(F32).reshape(1, M_INNER), mlstm_skip.astype(F32)[None, :],
                w_branch_a.astype(BF16), w_branch_b.astype(BF16), w_out.astype(BF16))
    out_real, _ = _ffn(h1, norm2_g.astype(F32)[None, :], w_ff1.astype(BF16), w_ff2.astype(BF16))
    return out_real.reshape(BATCH, SEQ, D_MODEL)
```

```python
import functools

import jax
import jax.numpy as jnp
from jax import lax
from jax.experimental import pallas as pl
from jax.experimental.pallas import tpu as pltpu

D_MODEL = 1024
BATCH = 4
SEQ = 4096
N_META = 16
GRID_W = 64
ROWS = SEQ // GRID_W
M_HEADS = 4
M_DV = 256
M_DK = 128
M_INNER = M_HEADS * M_DV
NA_HEADS = 8
NA_DH = 64
NA_INNER = NA_HEADS * NA_DH
NA_WIN_H = 8
NA_WIN_W = 16
D_FF = 4 * D_MODEL
EPS = 1e-6
NEG_LOG_GATE = -1e9
MASKED = -1e30
LOG2E = 1.4426950408889634

TAIL = 128
NPAD = TAIL - N_META
N_REAL = BATCH * SEQ
NT = N_REAL + BATCH * TAIL
TM = 512
N_TILES = NT // TM
N_REAL_TILES = N_REAL // TM

VMEM_LIMIT = 56 * 1024 * 1024

F32 = jnp.float32
BF16 = jnp.bfloat16


def _dot(a, b):
    return jnp.dot(a, b, preferred_element_type=F32)


def _dot_nt(a, b):
    return lax.dot_general(a, b, (((1,), (1,)), ((), ())), preferred_element_type=F32)


def _dot_tn(a, b):
    return lax.dot_general(a, b, (((0,), (0,)), ((), ())), preferred_element_type=F32)


def _sigmoid(z):
    return 0.5 * jnp.tanh(0.5 * z) + 0.5


def _log_sigmoid(z):
    return jnp.minimum(z, 0.0) - jnp.log1p(jnp.exp(-jnp.abs(z)))


def _const_spec(shape):
    nd = len(shape)
    return pl.BlockSpec(shape, lambda *_: (0,) * nd, pipeline_mode=pl.Buffered(1))


def _params(*sem):
    return pltpu.CompilerParams(dimension_semantics=sem, vmem_limit_bytes=VMEM_LIMIT)


GATE_PAD = 8


HALO = 8
HALO_PAD = 16
TILES_PER_SEQ = SEQ // TM


def _inproj_kernel(x_ref, tail_ref, xprev_ref, xnext_ref, xedge_ref, g1_ref, wx_ref, wo_ref, wg_ref,
                   wqkv_ref, wgab_ref, gb_ref, qg_ref, kg_ref, bd_ref, cw_ref, cb_ref, wqk_ref,
                   vm_ref, og_ref, grow_ref, q_ref, k_ref, v_ref, gab_ref, xc_ref, qm_ref, km_ref):
    i = pl.program_id(0)

    def norm1(h):
        ms = jnp.mean(h * h, axis=-1, keepdims=True)
        return (h * lax.rsqrt(ms + EPS) * g1_ref[...]).astype(BF16)

    def body(h_ref, is_tail):
        xn = norm1(h_ref[...])
        if is_tail:
            halo = jnp.concatenate([xedge_ref[...], jnp.zeros((HALO_PAD - 2 * BATCH, D_MODEL), F32)], axis=0)
        else:
            first = i % TILES_PER_SEQ == 0
            last = i % TILES_PER_SEQ == TILES_PER_SEQ - 1
            prev = jnp.where(first, tail_ref[TAIL - 1:TAIL, :], xprev_ref[HALO - 1:HALO, :])
            nxt = jnp.where(last, tail_ref[0:1, :], xnext_ref[0:1, :])
            halo = jnp.concatenate([prev, nxt, jnp.zeros((HALO_PAD - 2, D_MODEL), F32)], axis=0)
        xm_ext = _dot(jnp.concatenate([xn, norm1(halo)], axis=0), wx_ref[...])
        og = _dot(xn, wo_ref[...])
        gates = _dot(xn, wg_ref[...]) + gb_ref[...]
        gab = _dot(xn, wgab_ref[...])
        qkv = _dot(xn, wqkv_ref[...])
        xm = xm_ext[:TM]
        rows = lax.broadcasted_iota(jnp.int32, (TM, M_INNER), 0)
        x_m1 = pltpu.roll(xm, 1, axis=0)
        x_p1 = pltpu.roll(xm, TM - 1, axis=0)
        if is_tail:
            for b in range(BATCH):
                x_m1 = jnp.where(rows == b * TAIL, xm_ext[TM + b:TM + b + 1], x_m1)
                x_p1 = jnp.where(rows == b * TAIL + TAIL - 1, xm_ext[TM + BATCH + b:TM + BATCH + b + 1], x_p1)
        else:
            x_m1 = jnp.where(rows == 0, xm_ext[TM:TM + 1], x_m1)
            x_p1 = jnp.where(rows == TM - 1, xm_ext[TM + 1:TM + 2], x_p1)
        vm_ref[...] = xm.astype(BF16)
        z = cw_ref[0:1, :] * x_m1 + cw_ref[1:2, :] * xm + cw_ref[2:3, :] * x_p1 + cb_ref[...]
        xc = z * _sigmoid(z)
        xc_ref[...] = xc.astype(BF16)
        for hd in range(M_HEADS):
            qk = _dot(xc[:, hd * M_DV:(hd + 1) * M_DV].astype(BF16), wqk_ref[hd])
            qm_ref[:, hd * M_DK:(hd + 1) * M_DK] = qk[:, :M_DK].astype(BF16)
            km_ref[:, hd * M_DK:(hd + 1) * M_DK] = (qk[:, M_DK:] * (M_DK ** -0.5)).astype(BF16)
        og_ref[...] = og.astype(BF16)
        grow_ref[...] = jnp.transpose(gates)[:M_HEADS * GATE_PAD, :]
        gab_ref[...] = gab.astype(BF16)
        uq = qkv[:, :NA_INNER]
        uk = qkv[:, NA_INNER:2 * NA_INNER]
        msq = _dot((uq * uq).astype(BF16), bd_ref[...])
        msk = _dot((uk * uk).astype(BF16), bd_ref[...])
        q_ref[...] = (uq * lax.rsqrt(msq + EPS) * qg_ref[...]).astype(BF16)
        k_ref[...] = (uk * lax.rsqrt(msk + EPS) * kg_ref[...]).astype(BF16)
        v_ref[...] = qkv[:, 2 * NA_INNER:].astype(BF16)

    pl.when(i < N_REAL_TILES)(lambda: body(x_ref, False))
    pl.when(i == N_REAL_TILES)(lambda: body(tail_ref, True))


def _real_or_tail(body, *ref_pairs):
    i = pl.program_id(0)
    pl.when(i < N_REAL_TILES)(lambda: body(*ref_pairs[0::2]))
    pl.when(i == N_REAL_TILES)(lambda: body(*ref_pairs[1::2]))


def _real_spec(width):
    return pl.BlockSpec((TM, width), lambda i: (jnp.minimum(i, N_REAL_TILES - 1), 0))


def _tail_spec(width):
    return pl.BlockSpec((TM, width), lambda i: (0, 0))


def _row_spec(width):
    return pl.BlockSpec((TM, width), lambda i: (i, 0))


def _inproj(x2d, tailh, xedge, g1, wx, wo, wg, wqkv, wgab, gb, qg, kg, bd, cw, cb, wqk):
    halo_blocks = N_REAL // HALO
    per_tile = TM // HALO
    consts = (g1, wx, wo, wg, wqkv, wgab, gb, qg, kg, bd, cw, cb, wqk)
    return pl.pallas_call(
        _inproj_kernel,
        grid=(N_TILES,),
        in_specs=[_real_spec(D_MODEL), _tail_spec(D_MODEL),
                  pl.BlockSpec((HALO, D_MODEL), lambda i: (jnp.clip(i * per_tile - 1, 0, halo_blocks - 1), 0)),
                  pl.BlockSpec((HALO, D_MODEL), lambda i: (jnp.clip((i + 1) * per_tile, 0, halo_blocks - 1), 0)),
                  _const_spec(xedge.shape)]
                 + [_const_spec(a.shape) for a in consts],
        out_specs=[_row_spec(M_INNER), _row_spec(M_INNER),
                   pl.BlockSpec((M_HEADS * GATE_PAD, TM), lambda i: (0, i)),
                   _row_spec(NA_INNER), _row_spec(NA_INNER), _row_spec(NA_INNER), _row_spec(2 * D_MODEL),
                   _row_spec(M_INNER), _row_spec(M_HEADS * M_DK), _row_spec(M_HEADS * M_DK)],
        out_shape=[jax.ShapeDtypeStruct((NT, M_INNER), BF16),
                   jax.ShapeDtypeStruct((NT, M_INNER), BF16),
                   jax.ShapeDtypeStruct((M_HEADS * GATE_PAD, NT), F32),
                   jax.ShapeDtypeStruct((NT, NA_INNER), BF16),
                   jax.ShapeDtypeStruct((NT, NA_INNER), BF16),
                   jax.ShapeDtypeStruct((NT, NA_INNER), BF16),
                   jax.ShapeDtypeStruct((NT, 2 * D_MODEL), BF16),
                   jax.ShapeDtypeStruct((NT, M_INNER), BF16),
                   jax.ShapeDtypeStruct((NT, M_HEADS * M_DK), BF16),
                   jax.ShapeDtypeStruct((NT, M_HEADS * M_DK), BF16)],
        compiler_params=_params("parallel"),
        name="inproj",
    )(x2d, tailh, x2d, x2d, xedge, *consts)


MT = 256
N_MCHUNK = SEQ // MT
STATE_ROWS = M_DV + 16
ROWS_PER_DIR = 8
LOCAL_GROUP = 8
STATE_GROUP = 4


def _split3(x):
    hi = x.astype(BF16)
    r1 = x - hi.astype(F32)
    mid = r1.astype(BF16)
    lo = (r1 - mid.astype(F32)).astype(BF16)
    return hi, mid, lo


def _mlstm_local_start(q, k, v, g8, is_tail):
    t = q.shape[0]
    li = [g8[0:1], g8[2:3]]
    lf = [_log_sigmoid(g8[1:2]), _log_sigmoid(g8[3:4])]
    if is_tail:
        pad = lax.broadcasted_iota(jnp.int32, (1, t), 1) < NPAD
        li = [jnp.where(pad, NEG_LOG_GATE, x) for x in li]
        lf = [jnp.where(pad, 0.0, x) for x in lf]
    si = lax.broadcasted_iota(jnp.int32, (t, t), 0)
    ti = lax.broadcasted_iota(jnp.int32, (t, t), 1)
    hi, mid, lo = _split3(jnp.concatenate(lf, axis=0))
    lhs = jnp.concatenate([hi, mid, lo, jnp.zeros((10, t), BF16)], axis=0)
    pref = _dot(lhs, (si <= ti).astype(BF16))
    pre_f = pref[0:1] + pref[2:3] + pref[4:5]
    pre_b = pref[1:2] + pref[3:4] + pref[5:6]
    b_end = [pre_f[:, t - 1:t], pre_b[:, t - 1:t]]
    b = [pre_f, b_end[1] - pre_b + lf[1]]
    s_t = _dot_nt(k, q)
    v_tb = jnp.transpose(v.astype(BF16))
    return li, b, b_end, s_t, v_tb, k


def _mlstm_local_finish(li, b, b_end, s_t, v_tb, k):
    t = s_t.shape[0]
    si = lax.broadcasted_iota(jnp.int32, (t, t), 0)
    ti = lax.broadcasted_iota(jnp.int32, (t, t), 1)
    v_t = v_tb.astype(F32)
    out = []
    for dirn in range(2):
        g = li[dirn] - b[dirn]
        g_col = jnp.transpose(jnp.broadcast_to(g, (128, t)))
        d_t = jnp.concatenate([g_col + b[dirn][:, j:j + 128] for j in range(0, t, 128)], axis=1)
        d_t = jnp.where((si <= ti) if dirn == 0 else (si >= ti), d_t, MASKED)
        m_loc = jnp.max(d_t, axis=0, keepdims=True)
        p_t = s_t * jnp.exp(d_t - m_loc)
        den = jnp.sum(p_t, axis=0, keepdims=True)
        nl_t = _dot(v_tb, p_t.astype(BF16))
        a = b_end[dirn] + g
        a_max = jnp.max(a, axis=1, keepdims=True)
        w = jnp.exp(a - a_max)
        u = _dot(jnp.concatenate([(v_t * w).astype(BF16),
                                  jnp.broadcast_to(w, (STATE_ROWS - M_DV, t)).astype(BF16)], axis=0), k)
        rows = jnp.concatenate([m_loc, den, b[dirn], jnp.broadcast_to(a_max, (1, t)),
                                jnp.broadcast_to(b_end[dirn], (1, t)),
                                jnp.zeros((ROWS_PER_DIR - 5, t), F32)], axis=0)
        out.append((nl_t, u, rows))
    return out


def _mlstm_state_start(dirn, q, u, rows, s_ref, m_ref):
    a_max, b_end = rows[3:4, 0:1], rows[4:5, 0:1]
    m_prev = m_ref[dirn]
    s_old = s_ref[dirn]
    inter = _dot_nt(s_old.astype(BF16), q)
    m_new = jnp.maximum(b_end + m_prev, a_max)
    s_ref[dirn] = jnp.exp(b_end + m_prev - m_new) * s_old + jnp.exp(a_max - m_new) * u
    m_ref[dirn] = m_new
    return inter, m_prev


def _mlstm_state_finish(inter, m_prev, nl_t, rows):
    m_loc, den_loc, b = rows[0:1], rows[1:2], rows[2:3]
    m_inter = b + m_prev
    m_t = jnp.maximum(m_inter, m_loc)
    w_inter = jnp.exp(m_inter - m_t)
    w_loc = jnp.exp(m_loc - m_t)
    den = w_inter * inter[M_DV:M_DV + 1] + w_loc * den_loc
    scale = 1.0 / jnp.maximum(jnp.abs(den), jnp.exp(-m_t))
    return (w_inter * scale) * inter[:M_DV] + (w_loc * scale) * nl_t


def _mlstm_kernel(q_ref, qt_ref, k_ref, kt_ref, v_ref, vt_ref, g_ref, gt_ref, o_ref, ot_ref,
                  nl_s, u_s, rows_s, ht_s, s_ref, m_ref):
    s_ref[...] = jnp.zeros_like(s_ref)
    m_ref[...] = jnp.zeros_like(m_ref)
    tail_cols = pl.ds(SEQ, TAIL)

    def chunk(c):
        r0 = pl.multiple_of(c * MT, MT)
        return pl.ds(r0, MT)

    def keep_local(c, cols, per_dir):
        for dirn, (nl_t, u, rows) in enumerate(per_dir):
            nl_s[dirn, :, cols] = nl_t
            u_s[dirn, c] = u
            rows_s[dirn, :, cols] = rows

    def local_pass(i, carry):
        group = [i * LOCAL_GROUP + j for j in range(LOCAL_GROUP)]
        started = [_mlstm_local_start(q_ref[chunk(c), :], k_ref[chunk(c), :], v_ref[chunk(c), :],
                                      g_ref[:, chunk(c)], False) for c in group]
        results = [_mlstm_local_finish(*st) for st in started]
        for c, per_dir in zip(group, results):
            keep_local(c, chunk(c), per_dir)
        return carry

    def state_steps(items):
        started = [_mlstm_state_start(dirn, q, u_s[dirn, c], rows_s[dirn, :, cols], s_ref, m_ref)
                   for dirn, q, c, cols in items]
        return [_mlstm_state_finish(inter, m_prev, nl_s[dirn, :, cols], rows_s[dirn, :, cols])
                for (inter, m_prev), (dirn, _, c, cols) in zip(started, items)]

    def visit_items(i):
        chunks = []
        for j in range(STATE_GROUP):
            step = i * STATE_GROUP + j
            chunks += [(0, step), (1, N_MCHUNK - 1 - step)]
        return [(dirn, q_ref[chunk(c), :], c, chunk(c)) for dirn, c in chunks]

    def first_visits(i, carry):
        items = visit_items(i)
        for (_, _, _, cols), h_t in zip(items, state_steps(items)):
            ht_s[:, cols] = h_t
        return carry

    def second_visits(i, carry):
        items = visit_items(i)
        for (_, _, _, cols), h_t in zip(items, state_steps(items)):
            o_ref[cols, :] = jnp.transpose((ht_s[:, cols] + h_t).astype(o_ref.dtype))
        return carry

    keep_local(N_MCHUNK, tail_cols, _mlstm_local_finish(
        *_mlstm_local_start(qt_ref[...], kt_ref[...], vt_ref[...], gt_ref[...], True)))
    lax.fori_loop(0, N_MCHUNK // LOCAL_GROUP, local_pass, 0)
    ht_s[:, tail_cols] = state_steps([(0, qt_ref[...], N_MCHUNK, tail_cols)])[0]
    half = N_MCHUNK // 2 // STATE_GROUP
    lax.fori_loop(0, half, first_visits, 0)
    lax.fori_loop(half, 2 * half, second_visits, 0)
    h_tail = ht_s[:, tail_cols] + state_steps([(1, qt_ref[...], N_MCHUNK, tail_cols)])[0]
    ot_ref[...] = jnp.transpose(h_tail.astype(ot_ref.dtype))


def _mlstm(q, k, xm, grow):
    tail_blk = N_REAL // TAIL
    real = lambda w: pl.BlockSpec((SEQ, w), lambda b, h: (b, h))
    tail = lambda w: pl.BlockSpec((TAIL, w), lambda b, h: (tail_blk + b, h))
    lp = SEQ + TAIL
    return pl.pallas_call(
        _mlstm_kernel,
        grid=(BATCH, M_HEADS),
        in_specs=[real(M_DK), tail(M_DK), real(M_DK), tail(M_DK), real(M_DV), tail(M_DV),
                  pl.BlockSpec((GATE_PAD, SEQ), lambda b, h: (h, b)),
                  pl.BlockSpec((GATE_PAD, TAIL), lambda b, h: (h, tail_blk + b))],
        out_specs=[pl.BlockSpec((SEQ, M_DV), lambda b, h: (b, h)),
                   pl.BlockSpec((TAIL, M_DV), lambda b, h: (b, h))],
        out_shape=[jax.ShapeDtypeStruct((N_REAL, M_INNER), BF16),
                   jax.ShapeDtypeStruct((BATCH * TAIL, M_INNER), BF16)],
        scratch_shapes=[pltpu.VMEM((2, M_DV, lp), F32),
                        pltpu.VMEM((2, N_MCHUNK + 1, STATE_ROWS, M_DK), F32),
                        pltpu.VMEM((2, ROWS_PER_DIR, lp), F32),
                        pltpu.VMEM((M_DV, lp), F32),
                        pltpu.VMEM((2, STATE_ROWS, M_DK), F32),
                        pltpu.VMEM((2, 1, 1), F32)],
        compiler_params=_params("parallel", "parallel"),
        name="mlstm",
    )(q, q, k, k, xm, xm, grow, grow)


PAIR = 2 * NA_DH


def _natten_kernel(q_ref, qt_ref, k_ref, kt_ref, v_ref, vt_ref, bias_ref, o_ref, ot_ref):
    lane = lax.broadcasted_iota(jnp.int32, (1, PAIR), 1)
    first = lane < NA_DH
    k_tail = kt_ref[...]
    v_tail = vt_ref[...]

    def attend(q, keys, vals, bias_t):
        n = q.shape[0]
        zero = jnp.zeros_like(q)
        qs = jnp.concatenate([jnp.where(first, q, zero), jnp.where(first, zero, q)], axis=0)
        s = _dot_nt(keys, qs) + bias_t
        e = jnp.exp2(s - jnp.max(s, axis=0, keepdims=True))
        o = _dot(jnp.transpose(e.astype(BF16)), vals)
        inv = jnp.broadcast_to(1.0 / jnp.sum(e, axis=0, keepdims=True), (PAIR, 2 * n))
        o = o * jnp.transpose(inv)
        return jnp.where(first, o[:n], o[n:])

    def row_block(r, carry):
        r0 = jnp.clip(r - NA_WIN_H // 2, 0, ROWS - NA_WIN_H)
        qrows = pl.ds(pl.multiple_of(r * GRID_W, GRID_W), GRID_W)
        krows = pl.ds(pl.multiple_of(r0 * GRID_W, GRID_W), NA_WIN_H * GRID_W)
        keys = jnp.concatenate([k_ref[krows, :], k_tail], axis=0)
        vals = jnp.concatenate([v_ref[krows, :], v_tail], axis=0)
        o = attend(q_ref[qrows, :], keys, vals, bias_ref[r - r0])
        o_ref[qrows, :] = o.astype(o_ref.dtype)
        return carry

    lax.fori_loop(0, ROWS, row_block, 0, unroll=8)
    tb = bias_ref[0, NA_WIN_H * GRID_W:, :]
    tail_bias = jnp.concatenate([jnp.broadcast_to(tb[:, 0:1], (TAIL, TAIL)),
                                 jnp.broadcast_to(tb[:, NA_DH:NA_DH + 1], (TAIL, TAIL))], axis=1)
    ot_ref[...] = attend(qt_ref[...], k_tail, v_tail, tail_bias).astype(ot_ref.dtype)


def _natten(qn, kn, vn, bias):
    tail_blk = N_REAL // TAIL
    real = pl.BlockSpec((SEQ, PAIR), lambda b, p: (b, p))
    tail = pl.BlockSpec((TAIL, PAIR), lambda b, p: (tail_blk + b, p))
    return pl.pallas_call(
        _natten_kernel,
        grid=(BATCH, NA_HEADS // 2),
        in_specs=[real, tail, real, tail, real, tail,
                  pl.BlockSpec((None, NA_WIN_H, NA_WIN_H * GRID_W + TAIL, PAIR), lambda b, p: (p, 0, 0, 0))],
        out_specs=[pl.BlockSpec((SEQ, PAIR), lambda b, p: (b, p)),
                   pl.BlockSpec((TAIL, PAIR), lambda b, p: (b, p))],
        out_shape=[jax.ShapeDtypeStruct((N_REAL, NA_INNER), BF16),
                   jax.ShapeDtypeStruct((BATCH * TAIL, NA_INNER), BF16)],
        compiler_params=_params("parallel", "parallel"),
        name="natten",
    )(qn, qn, kn, kn, vn, vn, bias)


def _natten_bias(rpb, meta_bias):
    qc = jnp.arange(GRID_W)
    kc = jnp.arange(GRID_W)
    win0 = jnp.clip(qc - NA_WIN_W // 2, 0, GRID_W - NA_WIN_W)
    ok = (kc[:, None] >= win0[None, :]) & (kc[:, None] < win0[None, :] + NA_WIN_W)
    dc = jnp.clip(kc[:, None] - qc[None, :], -(NA_WIN_W - 1), NA_WIN_W - 1) + NA_WIN_W - 1
    onehot = (dc[None] == jnp.arange(2 * NA_WIN_W - 1)[:, None, None]).astype(F32)
    t1 = jnp.einsum('hdj,jkq->dkhq', rpb.astype(F32), onehot, precision=lax.Precision.HIGHEST)
    t1 = jnp.where(ok[None, :, None, :], t1, MASKED) * LOG2E
    t1 = jnp.transpose(t1.reshape(2 * NA_WIN_H - 1, GRID_W, NA_HEADS // 2, PAIR), (2, 0, 1, 3))
    met = jnp.concatenate([jnp.full((NA_HEADS, NPAD), MASKED, F32), meta_bias.astype(F32)], axis=1) * LOG2E
    met = jnp.broadcast_to(met.reshape(NA_HEADS // 2, 2, TAIL, 1), (NA_HEADS // 2, 2, TAIL, NA_DH))
    met = jnp.transpose(met, (0, 2, 1, 3)).reshape(NA_HEADS // 2, TAIL, PAIR)
    rows = [jnp.concatenate([t1[:, NA_WIN_H - 1 - dl:2 * NA_WIN_H - 1 - dl].reshape(-1, NA_WIN_H * GRID_W, PAIR),
                             met], axis=1) for dl in range(NA_WIN_H)]
    return jnp.stack(rows, axis=1)


def _merge_kernel(hsr_ref, hst_ref, ybr_ref, ybt_ref, xr_ref, xt_ref, xc_ref, og_ref, gab_ref,
                  ng_ref, sk_ref, wa_ref, wb_ref, wo_ref, o_ref):
    def body(hs_ref, yb_ref, h_ref):
        hs = hs_ref[...].astype(F32)
        parts = []
        for hd in range(M_HEADS):
            sl = hs[:, hd * M_DV:(hd + 1) * M_DV]
            parts.append(sl * lax.rsqrt(jnp.mean(sl * sl, axis=-1, keepdims=True) + EPS))
        hn = jnp.concatenate(parts, axis=1) * ng_ref[...]
        y_a = _sigmoid(og_ref[...].astype(F32)) * (hn + sk_ref[...] * xc_ref[...].astype(F32))
        gab = gab_ref[...].astype(F32)
        mix = (_sigmoid(gab[:, :D_MODEL]) * _dot(y_a.astype(BF16), wa_ref[...])
               + _sigmoid(gab[:, D_MODEL:]) * _dot(yb_ref[...], wb_ref[...]))
        o_ref[...] = h_ref[...] + _dot(mix.astype(BF16), wo_ref[...])

    _real_or_tail(body, hsr_ref, hst_ref, ybr_ref, ybt_ref, xr_ref, xt_ref)


def _merge(hs_real, hs_tail, yb_real, yb_tail, x2d, tailh, xc, og, gab, ng, sk, wa, wb, wo):
    return pl.pallas_call(
        _merge_kernel,
        grid=(N_TILES,),
        in_specs=[_real_spec(M_INNER), _tail_spec(M_INNER), _real_spec(NA_INNER), _tail_spec(NA_INNER),
                  _real_spec(D_MODEL), _tail_spec(D_MODEL),
                  _row_spec(M_INNER), _row_spec(M_INNER), _row_spec(2 * D_MODEL)]
                 + [_const_spec(a.shape) for a in (ng, sk, wa, wb, wo)],
        out_specs=_row_spec(D_MODEL),
        out_shape=jax.ShapeDtypeStruct((NT, D_MODEL), F32),
        compiler_params=_params("parallel"),
        name="merge",
    )(hs_real, hs_tail, yb_real, yb_tail, x2d, tailh, xc, og, gab, ng, sk, wa, wb, wo)


def _ffn_kernel(h_ref, g2_ref, w1_ref, w2_ref, or_ref, ot_ref):
    def body(o_ref):
        h = h_ref[...]
        xn = (h * lax.rsqrt(jnp.mean(h * h, axis=-1, keepdims=True) + EPS) * g2_ref[...]).astype(BF16)
        z = jnp.maximum(_dot(xn, w1_ref[...]), 0.0)
        o_ref[...] = h + _dot((z * z).astype(BF16), w2_ref[...])

    _real_or_tail(body, or_ref, ot_ref)


def _ffn(h1, g2, w1, w2):
    return pl.pallas_call(
        _ffn_kernel,
        grid=(N_TILES,),
        in_specs=[_row_spec(D_MODEL), _const_spec(g2.shape), _const_spec(w1.shape), _const_spec(w2.shape)],
        out_specs=[_real_spec(D_MODEL), _tail_spec(D_MODEL)],
        out_shape=[jax.ShapeDtypeStruct((N_REAL, D_MODEL), F32),
                   jax.ShapeDtypeStruct((BATCH * TAIL, D_MODEL), F32)],
        compiler_params=_params("arbitrary"),
        name="ffn",
    )(h1, g2, w1, w2)


def kernel(x, meta_tokens, norm1_g, w_in, mlstm_conv_w, mlstm_conv_b, mlstm_wq, mlstm_wk, mlstm_gate_b, mlstm_norm_g, mlstm_skip, na_q_norm_g, na_k_norm_g, na_rpb, na_meta_bias, w_branch_a, w_branch_b, w_out, norm2_g, w_ff1, w_ff2):
    off_mo = M_INNER
    off_mg = off_mo + M_INNER
    off_q = off_mg + 4 * M_HEADS
    off_v = off_q + 2 * NA_INNER
    off_g = off_v + NA_INNER

    x2d = x.astype(F32).reshape(N_REAL, D_MODEL)
    tail = jnp.concatenate([jnp.zeros((NPAD, D_MODEL), F32), meta_tokens.astype(F32)], axis=0)
    tailh = jnp.tile(tail, (BATCH, 1))

    w_piece = lambda lo, hi: w_in[:, lo:hi].astype(BF16)
    pad_gates = lambda a: jnp.pad(
        jnp.swapaxes(a.reshape(a.shape[0], 4, M_HEADS), 1, 2),
        ((0, 0), (0, 0), (0, GATE_PAD - 4))).reshape(a.shape[0], M_HEADS * GATE_PAD)
    wg = jnp.pad(pad_gates(w_piece(off_mg, off_q)), ((0, 0), (0, 128 - M_HEADS * GATE_PAD)))
    gb = jnp.pad(pad_gates(mlstm_gate_b.astype(F32).reshape(1, 4 * M_HEADS)),
                 ((0, 0), (0, 128 - M_HEADS * GATE_PAD)))
    bd =jnp.kron(jnp.eye(NA_HEADS, dtype=F32), jnp.full((NA_DH, NA_DH), 1.0 / NA_DH, F32)).astype(BF16)
    qg = jnp.tile(na_q_norm_g.astype(F32), NA_HEADS)[None, :] * (NA_DH ** -0.5 * LOG2E)
    kg = jnp.tile(na_k_norm_g.astype(F32), NA_HEADS)[None, :]

    xb = x.astype(F32)
    xedge = jnp.concatenate([xb[:, SEQ - 1, :], xb[:, 0, :]], axis=0)
    wqk = jnp.concatenate([mlstm_wq, mlstm_wk], axis=-1).astype(BF16)
    vm, og, grow, qn, kn, vn, gab, xc, qm, km = _inproj(
        x2d, tailh, xedge, norm1_g.astype(F32)[None, :], w_piece(0, off_mo), w_piece(off_mo, off_mg), wg,
        w_piece(off_q, off_g), w_piece(off_g, off_g + 2 * D_MODEL), gb, qg, kg, bd,
        mlstm_conv_w.astype(F32).reshape(3, M_INNER), mlstm_conv_b.astype(F32)[None, :], wqk)

    hs_real, hs_tail = _mlstm(qm, km, vm, grow)
    yb_real, yb_tail = _natten(qn, kn, vn, _natten_bias(na_rpb, na_meta_bias))

    h1 = _merge(hs_real, hs_tail, yb_real, yb_tail, x2d, tailh, xc, og, gab,
                mlstm_norm_g.astype(F32).reshape(1, M_INNER), mlstm_skip.astype(F32)[None, :],
                w_branch_a.astype(BF16), w_branch_b.astype(BF16), w_out.astype(BF16))
    out_real, _ = _ffn(h1, norm2_g.astype(F32)[None, :], w_ff1.astype(BF16), w_ff2.astype(BF16))
    return out_real.reshape(BATCH, SEQ, D_MODEL)
```

```python
import functools

import jax
import jax.numpy as jnp
from jax import lax
from jax.experimental import pallas as pl
from jax.experimental.pallas import tpu as pltpu

D_MODEL = 1024
BATCH = 4
SEQ = 4096
N_META = 16
GRID_W = 64
ROWS = SEQ // GRID_W
M_HEADS = 4
M_DV = 256
M_DK = 128
M_INNER = M_HEADS * M_DV
NA_HEADS = 8
NA_DH = 64
NA_INNER = NA_HEADS * NA_DH
NA_WIN_H = 8
NA_WIN_W = 16
D_FF = 4 * D_MODEL
EPS = 1e-6
NEG_LOG_GATE = -1e9
MASKED = -1e30
LOG2E = 1.4426950408889634

TAIL = 128
NPAD = TAIL - N_META
N_REAL = BATCH * SEQ
NT = N_REAL + BATCH * TAIL
TM = 512
N_TILES = NT // TM
N_REAL_TILES = N_REAL // TM

VMEM_LIMIT = 56 * 1024 * 1024

F32 = jnp.float32
BF16 = jnp.bfloat16


def _dot(a, b):
    return jnp.dot(a, b, preferred_element_type=F32)


def _dot_nt(a, b):
    return lax.dot_general(a, b, (((1,), (1,)), ((), ())), preferred_element_type=F32)


def _dot_tn(a, b):
    return lax.dot_general(a, b, (((0,), (0,)), ((), ())), preferred_element_type=F32)


def _sigmoid(z):
    return 0.5 * jnp.tanh(0.5 * z) + 0.5


def _log_sigmoid(z):
    return jnp.minimum(z, 0.0) - jnp.log1p(jnp.exp(-jnp.abs(z)))


def _const_spec(shape):
    nd = len(shape)
    return pl.BlockSpec(shape, lambda *_: (0,) * nd, pipeline_mode=pl.Buffered(1))


def _params(*sem):
    return pltpu.CompilerParams(dimension_semantics=sem, vmem_limit_bytes=VMEM_LIMIT)


GATE_PAD = 8
OFF_MX = 0
OFF_MO = OFF_MX + M_INNER
OFF_MG = OFF_MO + M_INNER
OFF_Q = OFF_MG + 4 * M_HEADS
OFF_G = OFF_Q + 3 * NA_INNER
D_IN_PROJ = OFF_G + 2 * D_MODEL


HALO = 8
HALO_PAD = 16
TILES_PER_SEQ = SEQ // TM


def _inproj_kernel(x_ref, tail_ref, xprev_ref, xnext_ref, xedge_ref, g1_ref, w_ref, wg_ref,
                   gb_ref, qg_ref, kg_ref, bd_ref, cw_ref, cb_ref, wqk_ref,
                   vm_ref, og_ref, grow_ref, q_ref, k_ref, v_ref, gab_ref, xc_ref, qm_ref, km_ref,
                   wqkv_s, wgab_s):
    i = pl.program_id(0)

    @pl.when(i == 0)
    def _():
        wqkv_s[...] = w_ref[:, OFF_Q:OFF_G]
        wgab_s[...] = w_ref[:, OFF_G:D_IN_PROJ]

    def norm1(h):
        ms = jnp.mean(h * h, axis=-1, keepdims=True)
        return (h * lax.rsqrt(ms + EPS) * g1_ref[...]).astype(BF16)

    def body(h_ref, is_tail):
        xn = norm1(h_ref[...])
        if is_tail:
            halo = jnp.concatenate([xedge_ref[...], jnp.zeros((HALO_PAD - 2 * BATCH, D_MODEL), F32)], axis=0)
        else:
            first = i % TILES_PER_SEQ == 0
            last = i % TILES_PER_SEQ == TILES_PER_SEQ - 1
            prev = jnp.where(first, tail_ref[TAIL - 1:TAIL, :], xprev_ref[HALO - 1:HALO, :])
            nxt = jnp.where(last, tail_ref[0:1, :], xnext_ref[0:1, :])
            halo = jnp.concatenate([prev, nxt, jnp.zeros((HALO_PAD - 2, D_MODEL), F32)], axis=0)
        xm_ext = _dot(jnp.concatenate([xn, norm1(halo)], axis=0), w_ref[:, OFF_MX:OFF_MO])
        og = _dot(xn, w_ref[:, OFF_MO:OFF_MG])
        gates = _dot(xn, wg_ref[...]) + gb_ref[...]
        gab = _dot(xn, wgab_s[...])
        qkv = _dot(xn, wqkv_s[...])
        xm = xm_ext[:TM]
        rows = lax.broadcasted_iota(jnp.int32, (TM, M_INNER), 0)
        x_m1 = pltpu.roll(xm, 1, axis=0)
        x_p1 = pltpu.roll(xm, TM - 1, axis=0)
        if is_tail:
            for b in range(BATCH):
                x_m1 = jnp.where(rows == b * TAIL, xm_ext[TM + b:TM + b + 1], x_m1)
                x_p1 = jnp.where(rows == b * TAIL + TAIL - 1, xm_ext[TM + BATCH + b:TM + BATCH + b + 1], x_p1)
        else:
            x_m1 = jnp.where(rows == 0, xm_ext[TM:TM + 1], x_m1)
            x_p1 = jnp.where(rows == TM - 1, xm_ext[TM + 1:TM + 2], x_p1)
        vm_ref[...] = xm.astype(BF16)
        z = cw_ref[0:1, :] * x_m1 + cw_ref[1:2, :] * xm + cw_ref[2:3, :] * x_p1 + cb_ref[...]
        xc = z * _sigmoid(z)
        xc_ref[...] = xc.astype(BF16)
        for hd in range(M_HEADS):
            qk = _dot(xc[:, hd * M_DV:(hd + 1) * M_DV].astype(BF16), wqk_ref[hd])
            qm_ref[:, hd * M_DK:(hd + 1) * M_DK] = qk[:, :M_DK].astype(BF16)
            km_ref[:, hd * M_DK:(hd + 1) * M_DK] = (qk[:, M_DK:] * (M_DK ** -0.5)).astype(BF16)
        og_ref[...] = og.astype(BF16)
        grow_ref[...] = jnp.transpose(gates)[:M_HEADS * GATE_PAD, :]
        gab_ref[...] = gab.astype(BF16)
        uq = qkv[:, :NA_INNER]
        uk = qkv[:, NA_INNER:2 * NA_INNER]
        msq = _dot((uq * uq).astype(BF16), bd_ref[...])
        msk = _dot((uk * uk).astype(BF16), bd_ref[...])
        q_ref[...] = (uq * lax.rsqrt(msq + EPS) * qg_ref[...]).astype(BF16)
        k_ref[...] = (uk * lax.rsqrt(msk + EPS) * kg_ref[...]).astype(BF16)
        v_ref[...] = qkv[:, 2 * NA_INNER:].astype(BF16)

    pl.when(i < N_REAL_TILES)(lambda: body(x_ref, False))
    pl.when(i == N_REAL_TILES)(lambda: body(tail_ref, True))


def _real_or_tail(body, *ref_pairs):
    i = pl.program_id(0)
    pl.when(i < N_REAL_TILES)(lambda: body(*ref_pairs[0::2]))
    pl.when(i == N_REAL_TILES)(lambda: body(*ref_pairs[1::2]))


def _real_spec(width):
    return pl.BlockSpec((TM, width), lambda i: (jnp.minimum(i, N_REAL_TILES - 1), 0))


def _tail_spec(width):
    return pl.BlockSpec((TM, width), lambda i: (0, 0))


def _row_spec(width):
    return pl.BlockSpec((TM, width), lambda i: (i, 0))


def _inproj(x2d, tailh, xedge, g1, w_all, wg, gb, qg, kg, bd, cw, cb, wqk):
    halo_blocks = N_REAL // HALO
    per_tile = TM // HALO
    consts = (g1, w_all, wg, gb, qg, kg, bd, cw, cb, wqk)
    return pl.pallas_call(
        _inproj_kernel,
        grid=(N_TILES,),
        in_specs=[_real_spec(D_MODEL), _tail_spec(D_MODEL),
                  pl.BlockSpec((HALO, D_MODEL), lambda i: (jnp.clip(i * per_tile - 1, 0, halo_blocks - 1), 0)),
                  pl.BlockSpec((HALO, D_MODEL), lambda i: (jnp.clip((i + 1) * per_tile, 0, halo_blocks - 1), 0)),
                  _const_spec(xedge.shape)]
                 + [_const_spec(a.shape) for a in consts],
        out_specs=[_row_spec(M_INNER), _row_spec(M_INNER),
                   pl.BlockSpec((M_HEADS * GATE_PAD, TM), lambda i: (0, i)),
                   _row_spec(NA_INNER), _row_spec(NA_INNER), _row_spec(NA_INNER), _row_spec(2 * D_MODEL),
                   _row_spec(M_INNER), _row_spec(M_HEADS * M_DK), _row_spec(M_HEADS * M_DK)],
        out_shape=[jax.ShapeDtypeStruct((NT, M_INNER), BF16),
                   jax.ShapeDtypeStruct((NT, M_INNER), BF16),
                   jax.ShapeDtypeStruct((M_HEADS * GATE_PAD, NT), F32),
                   jax.ShapeDtypeStruct((NT, NA_INNER), BF16),
                   jax.ShapeDtypeStruct((NT, NA_INNER), BF16),
                   jax.ShapeDtypeStruct((NT, NA_INNER), BF16),
                   jax.ShapeDtypeStruct((NT, 2 * D_MODEL), BF16),
                   jax.ShapeDtypeStruct((NT, M_INNER), BF16),
                   jax.ShapeDtypeStruct((NT, M_HEADS * M_DK), BF16),
                   jax.ShapeDtypeStruct((NT, M_HEADS * M_DK), BF16)],
        scratch_shapes=[pltpu.VMEM((D_MODEL, OFF_G - OFF_Q), BF16),
                        pltpu.VMEM((D_MODEL, D_IN_PROJ - OFF_G), BF16)],
        compiler_params=_params("arbitrary"),
        name="inproj",
    )(x2d, tailh, x2d, x2d, xedge, *consts)


MT = 256
N_MCHUNK = SEQ // MT
STATE_ROWS = M_DV + 16
ROWS_PER_DIR = 8
LOCAL_GROUP = 8
STATE_GROUP = 4


def _split3(x):
    hi = x.astype(BF16)
    r1 = x - hi.astype(F32)
    mid = r1.astype(BF16)
    lo = (r1 - mid.astype(F32)).astype(BF16)
    return hi, mid, lo


def _mlstm_local_start(q, k, v, g8, is_tail):
    t = q.shape[0]
    li = [g8[0:1], g8[2:3]]
    lf = [_log_sigmoid(g8[1:2]), _log_sigmoid(g8[3:4])]
    if is_tail:
        pad = lax.broadcasted_iota(jnp.int32, (1, t), 1) < NPAD
        li = [jnp.where(pad, NEG_LOG_GATE, x) for x in li]
        lf = [jnp.where(pad, 0.0, x) for x in lf]
    si = lax.broadcasted_iota(jnp.int32, (t, t), 0)
    ti = lax.broadcasted_iota(jnp.int32, (t, t), 1)
    hi, mid, lo = _split3(jnp.concatenate(lf, axis=0))
    lhs = jnp.concatenate([hi, mid, lo, jnp.zeros((10, t), BF16)], axis=0)
    pref = _dot(lhs, (si <= ti).astype(BF16))
    pre_f = pref[0:1] + pref[2:3] + pref[4:5]
    pre_b = pref[1:2] + pref[3:4] + pref[5:6]
    b_end = [pre_f[:, t - 1:t], pre_b[:, t - 1:t]]
    b = [pre_f, b_end[1] - pre_b + lf[1]]
    s_t = _dot_nt(k, q)
    v_tb = jnp.transpose(v.astype(BF16))
    return li, b, b_end, s_t, v_tb, k


def _mlstm_local_finish(li, b, b_end, s_t, v_tb, k):
    t = s_t.shape[0]
    si = lax.broadcasted_iota(jnp.int32, (t, t), 0)
    ti = lax.broadcasted_iota(jnp.int32, (t, t), 1)
    kf = k.astype(F32)
    v_ext = jnp.concatenate([v_tb, jnp.ones((STATE_ROWS - M_DV, t), BF16)], axis=0)
    out = []
    for dirn in range(2):
        g2 = (li[dirn] - b[dirn]) * LOG2E
        g_col = jnp.transpose(jnp.broadcast_to(g2, (128, t)))
        g_st = jnp.where((si <= ti) if dirn == 0 else (si >= ti),
                         jnp.concatenate([g_col] * (t // 128), axis=1), MASKED)
        g_max = jnp.max(g_st, axis=0, keepdims=True)
        p_t = s_t * jnp.exp2(g_st - g_max)
        den = jnp.sum(p_t, axis=0, keepdims=True)
        nl_t = _dot(v_tb, p_t.astype(BF16))
        m_loc = b[dirn] + g_max * (1.0 / LOG2E)
        top = jnp.max(g2, axis=1, keepdims=True)
        kw = (kf * jnp.exp2(g_col - top)).astype(BF16)
        u = _dot(v_ext, kw)
        a_max = b_end[dirn] + top * (1.0 / LOG2E)
        rows = jnp.concatenate([m_loc, den, b[dirn], jnp.broadcast_to(a_max, (1, t)),
                                jnp.broadcast_to(b_end[dirn], (1, t)),
                                jnp.zeros((ROWS_PER_DIR - 5, t), F32)], axis=0)
        out.append((nl_t, u, rows))
    return out


def _mlstm_state_start(dirn, q, u, rows, s_ref, m_ref):
    a_max, b_end = rows[3:4, 0:1], rows[4:5, 0:1]
    m_prev = m_ref[dirn]
    s_old = s_ref[dirn]
    inter = _dot_nt(s_old.astype(BF16), q)
    m_new = jnp.maximum(b_end + m_prev, a_max)
    s_ref[dirn] = jnp.exp(b_end + m_prev - m_new) * s_old + jnp.exp(a_max - m_new) * u
    m_ref[dirn] = m_new
    return inter, m_prev


def _mlstm_state_finish(inter, m_prev, nl_t, rows):
    m_loc, den_loc, b = rows[0:1], rows[1:2], rows[2:3]
    m_inter = b + m_prev
    m_t = jnp.maximum(m_inter, m_loc)
    w_inter = jnp.exp(m_inter - m_t)
    w_loc = jnp.exp(m_loc - m_t)
    den = w_inter * inter[M_DV:M_DV + 1] + w_loc * den_loc
    scale = 1.0 / jnp.maximum(jnp.abs(den), jnp.exp(-m_t))
    return (w_inter * scale) * inter[:M_DV] + (w_loc * scale) * nl_t


def _mlstm_kernel(q_ref, qt_ref, k_ref, kt_ref, v_ref, vt_ref, g_ref, gt_ref, o_ref, ot_ref,
                  nl_s, u_s, rows_s, ht_s, s_ref, m_ref):
    s_ref[...] = jnp.zeros_like(s_ref)
    m_ref[...] = jnp.zeros_like(m_ref)
    tail_cols = pl.ds(SEQ, TAIL)

    def chunk(c):
        r0 = pl.multiple_of(c * MT, MT)
        return pl.ds(r0, MT)

    def keep_local(c, cols, per_dir):
        for dirn, (nl_t, u, rows) in enumerate(per_dir):
            nl_s[dirn, :, cols] = nl_t
            u_s[dirn, c] = u
            rows_s[dirn, :, cols] = rows

    def local_pass(i, carry):
        group = [i * LOCAL_GROUP + j for j in range(LOCAL_GROUP)]
        started = [_mlstm_local_start(q_ref[chunk(c), :], k_ref[chunk(c), :], v_ref[chunk(c), :],
                                      g_ref[:, chunk(c)], False) for c in group]
        results = [_mlstm_local_finish(*st) for st in started]
        for c, per_dir in zip(group, results):
            keep_local(c, chunk(c), per_dir)
        return carry

    def state_steps(items):
        started = [_mlstm_state_start(dirn, q, u_s[dirn, c], rows_s[dirn, :, cols], s_ref, m_ref)
                   for dirn, q, c, cols in items]
        return [_mlstm_state_finish(inter, m_prev, nl_s[dirn, :, cols], rows_s[dirn, :, cols])
                for (inter, m_prev), (dirn, _, c, cols) in zip(started, items)]

    def visit_items(i):
        chunks = []
        for j in range(STATE_GROUP):
            step = i * STATE_GROUP + j
            chunks += [(0, step), (1, N_MCHUNK - 1 - step)]
        return [(dirn, q_ref[chunk(c), :], c, chunk(c)) for dirn, c in chunks]

    def first_visits(i, carry):
        items = visit_items(i)
        for (_, _, _, cols), h_t in zip(items, state_steps(items)):
            ht_s[:, cols] = h_t
        return carry

    def second_visits(i, carry):
        items = visit_items(i)
        for (_, _, _, cols), h_t in zip(items, state_steps(items)):
            o_ref[cols, :] = jnp.transpose((ht_s[:, cols] + h_t).astype(o_ref.dtype))
        return carry

    keep_local(N_MCHUNK, tail_cols, _mlstm_local_finish(
        *_mlstm_local_start(qt_ref[...], kt_ref[...], vt_ref[...], gt_ref[...], True)))
    lax.fori_loop(0, N_MCHUNK // LOCAL_GROUP, local_pass, 0)
    ht_s[:, tail_cols] = state_steps([(0, qt_ref[...], N_MCHUNK, tail_cols)])[0]
    half = N_MCHUNK // 2 // STATE_GROUP
    lax.fori_loop(0, half, first_visits, 0)
    lax.fori_loop(half, 2 * half, second_visits, 0)
    h_tail = ht_s[:, tail_cols] + state_steps([(1, qt_ref[...], N_MCHUNK, tail_cols)])[0]
    ot_ref[...] = jnp.transpose(h_tail.astype(ot_ref.dtype))


def _mlstm(q, k, xm, grow):
    tail_blk = N_REAL // TAIL
    real = lambda w: pl.BlockSpec((SEQ, w), lambda b, h: (b, h))
    tail = lambda w: pl.BlockSpec((TAIL, w), lambda b, h: (tail_blk + b, h))
    lp = SEQ + TAIL
    return pl.pallas_call(
        _mlstm_kernel,
        grid=(BATCH, M_HEADS),
        in_specs=[real(M_DK), tail(M_DK), real(M_DK), tail(M_DK), real(M_DV), tail(M_DV),
                  pl.BlockSpec((GATE_PAD, SEQ), lambda b, h: (h, b)),
                  pl.BlockSpec((GATE_PAD, TAIL), lambda b, h: (h, tail_blk + b))],
        out_specs=[pl.BlockSpec((SEQ, M_DV), lambda b, h: (b, h)),
                   pl.BlockSpec((TAIL, M_DV), lambda b, h: (b, h))],
        out_shape=[jax.ShapeDtypeStruct((N_REAL, M_INNER), BF16),
                   jax.ShapeDtypeStruct((BATCH * TAIL, M_INNER), BF16)],
        scratch_shapes=[pltpu.VMEM((2, M_DV, lp), F32),
                        pltpu.VMEM((2, N_MCHUNK + 1, STATE_ROWS, M_DK), F32),
                        pltpu.VMEM((2, ROWS_PER_DIR, lp), F32),
                        pltpu.VMEM((M_DV, lp), F32),
                        pltpu.VMEM((2, STATE_ROWS, M_DK), F32),
                        pltpu.VMEM((2, 1, 1), F32)],
        compiler_params=_params("parallel", "parallel"),
        name="mlstm",
    )(q, q, k, k, xm, xm, grow, grow)


PAIR = 2 * NA_DH


def _natten_kernel(q_ref, qt_ref, k_ref, kt_ref, v_ref, vt_ref, bias_ref, o_ref, ot_ref):
    lane = lax.broadcasted_iota(jnp.int32, (1, PAIR), 1)
    first = lane < NA_DH
    k_tail = kt_ref[...]
    v_tail = vt_ref[...]

    def attend(q, keys, vals, bias_t):
        n = q.shape[0]
        zero = jnp.zeros_like(q)
        qs = jnp.concatenate([jnp.where(first, q, zero), jnp.where(first, zero, q)], axis=0)
        s = _dot_nt(keys, qs) + bias_t
        e = jnp.exp2(s - jnp.max(s, axis=0, keepdims=True))
        o = _dot(jnp.transpose(e.astype(BF16)), vals)
        inv = jnp.broadcast_to(1.0 / jnp.sum(e, axis=0, keepdims=True), (PAIR, 2 * n))
        o = o * jnp.transpose(inv)
        return jnp.where(first, o[:n], o[n:])

    def row_block(r, carry):
        r0 = jnp.clip(r - NA_WIN_H // 2, 0, ROWS - NA_WIN_H)
        qrows = pl.ds(pl.multiple_of(r * GRID_W, GRID_W), GRID_W)
        krows = pl.ds(pl.multiple_of(r0 * GRID_W, GRID_W), NA_WIN_H * GRID_W)
        keys = jnp.concatenate([k_ref[krows, :], k_tail], axis=0)
        vals = jnp.concatenate([v_ref[krows, :], v_tail], axis=0)
        o = attend(q_ref[qrows, :], keys, vals, bias_ref[r - r0])
        o_ref[qrows, :] = o.astype(o_ref.dtype)
        return carry

    lax.fori_loop(0, ROWS, row_block, 0, unroll=8)
    tb = bias_ref[0, NA_WIN_H * GRID_W:, :]
    tail_bias = jnp.concatenate([jnp.broadcast_to(tb[:, 0:1], (TAIL, TAIL)),
                                 jnp.broadcast_to(tb[:, NA_DH:NA_DH + 1], (TAIL, TAIL))], axis=1)
    ot_ref[...] = attend(qt_ref[...], k_tail, v_tail, tail_bias).astype(ot_ref.dtype)


def _natten(qn, kn, vn, bias):
    tail_blk = N_REAL // TAIL
    real = pl.BlockSpec((SEQ, PAIR), lambda b, p: (b, p))
    tail = pl.BlockSpec((TAIL, PAIR), lambda b, p: (tail_blk + b, p))
    return pl.pallas_call(
        _natten_kernel,
        grid=(BATCH, NA_HEADS // 2),
        in_specs=[real, tail, real, tail, real, tail,
                  pl.BlockSpec((None, NA_WIN_H, NA_WIN_H * GRID_W + TAIL, PAIR), lambda b, p: (p, 0, 0, 0))],
        out_specs=[pl.BlockSpec((SEQ, PAIR), lambda b, p: (b, p)),
                   pl.BlockSpec((TAIL, PAIR), lambda b, p: (b, p))],
        out_shape=[jax.ShapeDtypeStruct((N_REAL, NA_INNER), BF16),
                   jax.ShapeDtypeStruct((BATCH * TAIL, NA_INNER), BF16)],
        compiler_params=_params("parallel", "parallel"),
        name="natten",
    )(qn, qn, kn, kn, vn, vn, bias)


def _natten_bias(rpb, meta_bias):
    qc = jnp.arange(GRID_W)
    kc = jnp.arange(GRID_W)
    win0 = jnp.clip(qc - NA_WIN_W // 2, 0, GRID_W - NA_WIN_W)
    ok = (kc[:, None] >= win0[None, :]) & (kc[:, None] < win0[None, :] + NA_WIN_W)
    dc = jnp.clip(kc[:, None] - qc[None, :], -(NA_WIN_W - 1), NA_WIN_W - 1) + NA_WIN_W - 1
    onehot = (dc[None] == jnp.arange(2 * NA_WIN_W - 1)[:, None, None]).astype(F32)
    t1 = jnp.einsum('hdj,jkq->dkhq', rpb.astype(F32), onehot, precision=lax.Precision.HIGHEST)
    t1 = jnp.where(ok[None, :, None, :], t1, MASKED) * LOG2E
    t1 = jnp.transpose(t1.reshape(2 * NA_WIN_H - 1, GRID_W, NA_HEADS // 2, PAIR), (2, 0, 1, 3))
    met = jnp.concatenate([jnp.full((NA_HEADS, NPAD), MASKED, F32), meta_bias.astype(F32)], axis=1) * LOG2E
    met = jnp.broadcast_to(met.reshape(NA_HEADS // 2, 2, TAIL, 1), (NA_HEADS // 2, 2, TAIL, NA_DH))
    met = jnp.transpose(met, (0, 2, 1, 3)).reshape(NA_HEADS // 2, TAIL, PAIR)
    rows = [jnp.concatenate([t1[:, NA_WIN_H - 1 - dl:2 * NA_WIN_H - 1 - dl].reshape(-1, NA_WIN_H * GRID_W, PAIR),
                             met], axis=1) for dl in range(NA_WIN_H)]
    return jnp.stack(rows, axis=1)


def _merge_kernel(hsr_ref, hst_ref, ybr_ref, ybt_ref, xr_ref, xt_ref, xc_ref, og_ref, gab_ref,
                  ng_ref, sk_ref, wa_ref, wb_ref, wo_ref, o_ref):
    def body(hs_ref, yb_ref, h_ref):
        hs = hs_ref[...].astype(F32)
        parts = []
        for hd in range(M_HEADS):
            sl = hs[:, hd * M_DV:(hd + 1) * M_DV]
            parts.append(sl * lax.rsqrt(jnp.mean(sl * sl, axis=-1, keepdims=True) + EPS))
        hn = jnp.concatenate(parts, axis=1) * ng_ref[...]
        y_a = _sigmoid(og_ref[...].astype(F32)) * (hn + sk_ref[...] * xc_ref[...].astype(F32))
        gab = gab_ref[...].astype(F32)
        mix = (_sigmoid(gab[:, :D_MODEL]) * _dot(y_a.astype(BF16), wa_ref[...])
               + _sigmoid(gab[:, D_MODEL:]) * _dot(yb_ref[...], wb_ref[...]))
        o_ref[...] = h_ref[...] + _dot(mix.astype(BF16), wo_ref[...])

    _real_or_tail(body, hsr_ref, hst_ref, ybr_ref, ybt_ref, xr_ref, xt_ref)


def _merge(hs_real, hs_tail, yb_real, yb_tail, x2d, tailh, xc, og, gab, ng, sk, wa, wb, wo):
    return pl.pallas_call(
        _merge_kernel,
        grid=(N_TILES,),
        in_specs=[_real_spec(M_INNER), _tail_spec(M_INNER), _real_spec(NA_INNER), _tail_spec(NA_INNER),
                  _real_spec(D_MODEL), _tail_spec(D_MODEL),
                  _row_spec(M_INNER), _row_spec(M_INNER), _row_spec(2 * D_MODEL)]
                 + [_const_spec(a.shape) for a in (ng, sk, wa, wb, wo)],
        out_specs=_row_spec(D_MODEL),
        out_shape=jax.ShapeDtypeStruct((NT, D_MODEL), F32),
        compiler_params=_params("parallel"),
        name="merge",
    )(hs_real, hs_tail, yb_real, yb_tail, x2d, tailh, xc, og, gab, ng, sk, wa, wb, wo)


def _ffn_kernel(h_ref, g2_ref, w1_ref, w2_ref, or_ref, ot_ref):
    def body(o_ref):
        h = h_ref[...]
        xn = (h * lax.rsqrt(jnp.mean(h * h, axis=-1, keepdims=True) + EPS) * g2_ref[...]).astype(BF16)
        z = jnp.maximum(_dot(xn, w1_ref[...]), 0.0)
        o_ref[...] = h + _dot((z * z).astype(BF16), w2_ref[...])

    _real_or_tail(body, or_ref, ot_ref)


def _ffn(h1, g2, w1, w2):
    return pl.pallas_call(
        _ffn_kernel,
        grid=(N_TILES,),
        in_specs=[_row_spec(D_MODEL), _const_spec(g2.shape), _const_spec(w1.shape), _const_spec(w2.shape)],
        out_specs=[_real_spec(D_MODEL), _tail_spec(D_MODEL)],
        out_shape=[jax.ShapeDtypeStruct((N_REAL, D_MODEL), F32),
                   jax.ShapeDtypeStruct((BATCH * TAIL, D_MODEL), F32)],
        compiler_params=_params("arbitrary"),
        name="ffn",
    )(h1, g2, w1, w2)


def kernel(x, meta_tokens, norm1_g, w_in, mlstm_conv_w, mlstm_conv_b, mlstm_wq, mlstm_wk, mlstm_gate_b, mlstm_norm_g, mlstm_skip, na_q_norm_g, na_k_norm_g, na_rpb, na_meta_bias, w_branch_a, w_branch_b, w_out, norm2_g, w_ff1, w_ff2):
    x2d = x.astype(F32).reshape(N_REAL, D_MODEL)
    tail = jnp.concatenate([jnp.zeros((NPAD, D_MODEL), F32), meta_tokens.astype(F32)], axis=0)
    tailh = jnp.tile(tail, (BATCH, 1))

    w_all = w_in.astype(BF16)
    pad_gates = lambda a: jnp.pad(
        jnp.swapaxes(a.reshape(a.shape[0], 4, M_HEADS), 1, 2),
        ((0, 0), (0, 0), (0, GATE_PAD - 4))).reshape(a.shape[0], M_HEADS * GATE_PAD)
    wg = jnp.pad(pad_gates(w_all[:, OFF_MG:OFF_Q]), ((0, 0), (0, 128 - M_HEADS * GATE_PAD)))
    gb = jnp.pad(pad_gates(mlstm_gate_b.astype(F32).reshape(1, 4 * M_HEADS)),
                 ((0, 0), (0, 128 - M_HEADS * GATE_PAD)))
    bd =jnp.kron(jnp.eye(NA_HEADS, dtype=F32), jnp.full((NA_DH, NA_DH), 1.0 / NA_DH, F32)).astype(BF16)
    qg = jnp.tile(na_q_norm_g.astype(F32), NA_HEADS)[None, :] * (NA_DH ** -0.5 * LOG2E)
    kg = jnp.tile(na_k_norm_g.astype(F32), NA_HEADS)[None, :]

    xb = x.astype(F32)
    xedge = jnp.concatenate([xb[:, SEQ - 1, :], xb[:, 0, :]], axis=0)
    wqk = jnp.concatenate([mlstm_wq, mlstm_wk], axis=-1).astype(BF16)
    vm, og, grow, qn, kn, vn, gab, xc, qm, km = _inproj(
        x2d, tailh, xedge, norm1_g.astype(F32)[None, :], w_all, wg, gb, qg, kg, bd,
        mlstm_conv_w.astype(F32).reshape(3, M_INNER), mlstm_conv_b.astype(F32)[None, :], wqk)

    hs_real, hs_tail = _mlstm(qm, km, vm, grow)
    yb_real, yb_tail = _natten(qn, kn, vn, _natten_bias(na_rpb, na_meta_bias))

    h1 = _merge(hs_real, hs_tail, yb_real, yb_tail, x2d, tailh, xc, og, gab,
                mlstm_norm_g.astype(F32).reshape(1, M_INNER), mlstm_skip.astype(F32)[None, :],
                w_branch_a.astype(BF16), w_branch_b.astype(BF16), w_out.astype(BF16))
    out_real, _ = _ffn(h1, norm2_g.astype(F32)[None, :], w_ff1.astype(BF16), w_ff2.astype(BF16))
    return out_real.reshape(BATCH, SEQ, D_MODEL)
```

```python
import jax
import jax.numpy as jnp
from jax import lax
from jax.experimental import pallas as pl
from jax.experimental.pallas import tpu as pltpu

D_MODEL = 1024
BATCH = 4
SEQ = 4096
N_META = 16
GRID_W = 64
ROWS = SEQ // GRID_W
M_HEADS = 4
M_DV = 256
M_DK = 128
M_INNER = M_HEADS * M_DV
NA_HEADS = 8
NA_DH = 64
NA_INNER = NA_HEADS * NA_DH
NA_WIN_H = 8
NA_WIN_W = 16
D_FF = 4 * D_MODEL
EPS = 1e-6
NEG_LOG_GATE = -1e9
MASKED = -1e30
LOG2E = 1.4426950408889634

TAIL = 128
NPAD = TAIL - N_META
N_REAL = BATCH * SEQ
NT = N_REAL + BATCH * TAIL
TM = 512
N_TILES = NT // TM
N_REAL_TILES = N_REAL // TM

LANES = 128
BF16_ROWS = 16
VMEM_LIMIT = 56 * 1024 * 1024

F32 = jnp.float32
BF16 = jnp.bfloat16


def _dot(a, b):
    return jnp.dot(a, b, preferred_element_type=F32)


def _dot_nt(a, b):
    return lax.dot_general(a, b, (((1,), (1,)), ((), ())), preferred_element_type=F32)


def _sigmoid(z):
    return 0.5 * jnp.tanh(0.5 * z) + 0.5


def _log_sigmoid(z):
    return jnp.minimum(z, 0.0) - jnp.log1p(jnp.exp(-jnp.abs(z)))


def _const_spec(shape):
    nd = len(shape)
    return pl.BlockSpec(shape, lambda *_: (0,) * nd, pipeline_mode=pl.Buffered(1))


def _params(*sem):
    return pltpu.CompilerParams(dimension_semantics=sem, vmem_limit_bytes=VMEM_LIMIT)


GATE_PAD = 8
OFF_MX = 0
OFF_MO = OFF_MX + M_INNER
OFF_MG = OFF_MO + M_INNER
OFF_Q = OFF_MG + 4 * M_HEADS
OFF_G = OFF_Q + 3 * NA_INNER
D_IN_PROJ = OFF_G + 2 * D_MODEL


HALO = 8
HALO_PAD = BF16_ROWS
TILES_PER_SEQ = SEQ // TM
NORM_GROUP = 256


def _inproj_kernel(x_ref, tail_ref, xprev_ref, xnext_ref, xedge_ref, g1_ref, w_ref, wg_ref,
                   gb_ref, qg_ref, kg_ref, bd_ref, cw_ref, cb_ref, wqk_ref,
                   vm_ref, og_ref, grow_ref, q_ref, k_ref, v_ref, gab_ref, xc_ref, qm_ref, km_ref,
                   wqkv_s, wgab_s):
    i = pl.program_id(0)

    @pl.when(i == 0)
    def _():
        wqkv_s[...] = w_ref[:, OFF_Q:OFF_G]
        wgab_s[...] = w_ref[:, OFF_G:D_IN_PROJ]

    def norm1(h):
        ms = jnp.mean(h * h, axis=-1, keepdims=True)
        return (h * lax.rsqrt(ms + EPS) * g1_ref[...]).astype(BF16)

    def body(h_ref, is_tail):
        xn = norm1(h_ref[...])
        if is_tail:
            halo = jnp.concatenate([xedge_ref[...], jnp.zeros((HALO_PAD - 2 * BATCH, D_MODEL), F32)], axis=0)
        else:
            first = i % TILES_PER_SEQ == 0
            last = i % TILES_PER_SEQ == TILES_PER_SEQ - 1
            prev = jnp.where(first, tail_ref[TAIL - 1:TAIL, :], xprev_ref[HALO - 1:HALO, :])
            nxt = jnp.where(last, tail_ref[0:1, :], xnext_ref[0:1, :])
            halo = jnp.concatenate([prev, nxt, jnp.zeros((HALO_PAD - 2, D_MODEL), F32)], axis=0)
        xm_ext = _dot(jnp.concatenate([xn, norm1(halo)], axis=0), w_ref[:, OFF_MX:OFF_MO])
        og = _dot(xn, w_ref[:, OFF_MO:OFF_MG])
        gates = _dot(xn, wg_ref[...]) + gb_ref[...]
        gab = _dot(xn, wgab_s[...])
        qkv = _dot(xn, wqkv_s[...])
        xm = xm_ext[:TM]
        rows = lax.broadcasted_iota(jnp.int32, (TM, M_INNER), 0)
        x_m1 = pltpu.roll(xm, 1, axis=0)
        x_p1 = pltpu.roll(xm, TM - 1, axis=0)
        if is_tail:
            for b in range(BATCH):
                x_m1 = jnp.where(rows == b * TAIL, xm_ext[TM + b:TM + b + 1], x_m1)
                x_p1 = jnp.where(rows == b * TAIL + TAIL - 1, xm_ext[TM + BATCH + b:TM + BATCH + b + 1], x_p1)
        else:
            x_m1 = jnp.where(rows == 0, xm_ext[TM:TM + 1], x_m1)
            x_p1 = jnp.where(rows == TM - 1, xm_ext[TM + 1:TM + 2], x_p1)
        vm_ref[...] = xm.astype(BF16)
        z = cw_ref[0:1, :] * x_m1 + cw_ref[1:2, :] * xm + cw_ref[2:3, :] * x_p1 + cb_ref[...]
        xc = z * _sigmoid(z)
        xc_ref[...] = xc.astype(BF16)
        for hd in range(M_HEADS):
            qk = _dot(xc[:, hd * M_DV:(hd + 1) * M_DV].astype(BF16), wqk_ref[hd])
            qm_ref[:, hd * M_DK:(hd + 1) * M_DK] = qk[:, :M_DK].astype(BF16)
            km_ref[:, hd * M_DK:(hd + 1) * M_DK] = (qk[:, M_DK:] * (M_DK ** -0.5)).astype(BF16)
        og_ref[...] = og.astype(BF16)
        grow_ref[...] = jnp.transpose(gates)[:M_HEADS * GATE_PAD, :]
        gab_ref[...] = gab.astype(BF16)
        uq = qkv[:, :NA_INNER]
        uk = qkv[:, NA_INNER:2 * NA_INNER]
        def head_mean(sq):
            return jnp.concatenate([_dot(sq[:, j:j + NORM_GROUP].astype(BF16), bd_ref[...])
                                    for j in range(0, NA_INNER, NORM_GROUP)], axis=1)

        msq = head_mean(uq * uq)
        msk = head_mean(uk * uk)
        q_ref[...] = (uq * lax.rsqrt(msq + EPS) * qg_ref[...]).astype(BF16)
        k_ref[...] = (uk * lax.rsqrt(msk + EPS) * kg_ref[...]).astype(BF16)
        v_ref[...] = qkv[:, 2 * NA_INNER:].astype(BF16)

    pl.when(i < N_REAL_TILES)(lambda: body(x_ref, False))
    pl.when(i == N_REAL_TILES)(lambda: body(tail_ref, True))


def _real_or_tail(body, *ref_pairs):
    i = pl.program_id(0)
    pl.when(i < N_REAL_TILES)(lambda: body(*ref_pairs[0::2]))
    pl.when(i == N_REAL_TILES)(lambda: body(*ref_pairs[1::2]))


def _real_spec(width):
    return pl.BlockSpec((TM, width), lambda i: (jnp.minimum(i, N_REAL_TILES - 1), 0))


def _tail_spec(width):
    return pl.BlockSpec((TM, width), lambda i: (0, 0))


def _row_spec(width):
    return pl.BlockSpec((TM, width), lambda i: (i, 0))


def _inproj(x2d, tailh, xedge, g1, w_all, wg, gb, qg, kg, bd, cw, cb, wqk):
    halo_blocks = N_REAL // HALO
    per_tile = TM // HALO
    consts = (g1, w_all, wg, gb, qg, kg, bd, cw, cb, wqk)
    return pl.pallas_call(
        _inproj_kernel,
        grid=(N_TILES,),
        in_specs=[_real_spec(D_MODEL), _tail_spec(D_MODEL),
                  pl.BlockSpec((HALO, D_MODEL), lambda i: (jnp.clip(i * per_tile - 1, 0, halo_blocks - 1), 0)),
                  pl.BlockSpec((HALO, D_MODEL), lambda i: (jnp.clip((i + 1) * per_tile, 0, halo_blocks - 1), 0)),
                  _const_spec(xedge.shape)]
                 + [_const_spec(a.shape) for a in consts],
        out_specs=[_row_spec(M_INNER), _row_spec(M_INNER),
                   pl.BlockSpec((M_HEADS * GATE_PAD, TM), lambda i: (0, i)),
                   _row_spec(NA_INNER), _row_spec(NA_INNER), _row_spec(NA_INNER), _row_spec(2 * D_MODEL),
                   _row_spec(M_INNER), _row_spec(M_HEADS * M_DK), _row_spec(M_HEADS * M_DK)],
        out_shape=[jax.ShapeDtypeStruct((NT, M_INNER), BF16),
                   jax.ShapeDtypeStruct((NT, M_INNER), BF16),
                   jax.ShapeDtypeStruct((M_HEADS * GATE_PAD, NT), F32),
                   jax.ShapeDtypeStruct((NT, NA_INNER), BF16),
                   jax.ShapeDtypeStruct((NT, NA_INNER), BF16),
                   jax.ShapeDtypeStruct((NT, NA_INNER), BF16),
                   jax.ShapeDtypeStruct((NT, 2 * D_MODEL), BF16),
                   jax.ShapeDtypeStruct((NT, M_INNER), BF16),
                   jax.ShapeDtypeStruct((NT, M_HEADS * M_DK), BF16),
                   jax.ShapeDtypeStruct((NT, M_HEADS * M_DK), BF16)],
        scratch_shapes=[pltpu.VMEM((D_MODEL, OFF_G - OFF_Q), BF16),
                        pltpu.VMEM((D_MODEL, D_IN_PROJ - OFF_G), BF16)],
        compiler_params=_params("arbitrary"),
        name="inproj",
    )(x2d, tailh, x2d, x2d, xedge, *consts)


MT = 256
N_MCHUNK = SEQ // MT
STATE_ROWS = M_DV + BF16_ROWS
ROWS_PER_DIR = 8
LOCAL_GROUP = 8
STATE_GROUP = 4


def _split3(x):
    hi = x.astype(BF16)
    r1 = x - hi.astype(F32)
    mid = r1.astype(BF16)
    lo = (r1 - mid.astype(F32)).astype(BF16)
    return hi, mid, lo


def _mlstm_local_start(q, k, v, g8, is_tail):
    t = q.shape[0]
    li = [g8[0:1], g8[2:3]]
    lf = [_log_sigmoid(g8[1:2]), _log_sigmoid(g8[3:4])]
    if is_tail:
        pad = lax.broadcasted_iota(jnp.int32, (1, t), 1) < NPAD
        li = [jnp.where(pad, NEG_LOG_GATE, x) for x in li]
        lf = [jnp.where(pad, 0.0, x) for x in lf]
    si = lax.broadcasted_iota(jnp.int32, (t, t), 0)
    ti = lax.broadcasted_iota(jnp.int32, (t, t), 1)
    hi, mid, lo = _split3(jnp.concatenate(lf, axis=0))
    lhs = jnp.concatenate([hi, mid, lo, jnp.zeros((BF16_ROWS - 6, t), BF16)], axis=0)
    pref = _dot(lhs, (si <= ti).astype(BF16))
    pre_f = pref[0:1] + pref[2:3] + pref[4:5]
    pre_b = pref[1:2] + pref[3:4] + pref[5:6]
    b_end = [pre_f[:, t - 1:t], pre_b[:, t - 1:t]]
    b = [pre_f, b_end[1] - pre_b + lf[1]]
    s_t = _dot_nt(k, q)
    v_tb = jnp.transpose(v.astype(BF16))
    return li, b, b_end, s_t, v_tb, k


def _mlstm_local_finish(li, b, b_end, s_t, v_tb, k):
    t = s_t.shape[0]
    si = lax.broadcasted_iota(jnp.int32, (t, t), 0)
    ti = lax.broadcasted_iota(jnp.int32, (t, t), 1)
    kf = k.astype(F32)
    v_ext = jnp.concatenate([v_tb, jnp.ones((STATE_ROWS - M_DV, t), BF16)], axis=0)
    out = []
    for dirn in range(2):
        g2 = (li[dirn] - b[dirn]) * LOG2E
        g_col = jnp.transpose(jnp.broadcast_to(g2, (LANES, t)))
        g_st = jnp.where((si <= ti) if dirn == 0 else (si >= ti),
                         jnp.concatenate([g_col] * (t // LANES), axis=1), MASKED)
        g_max = jnp.max(g_st, axis=0, keepdims=True)
        p_t = s_t * jnp.exp2(g_st - g_max)
        den = jnp.sum(p_t, axis=0, keepdims=True)
        nl_t = _dot(v_tb, p_t.astype(BF16))
        m_loc = b[dirn] + g_max * (1.0 / LOG2E)
        top = jnp.max(g2, axis=1, keepdims=True)
        kw = (kf * jnp.exp2(g_col - top)).astype(BF16)
        u = _dot(v_ext, kw)
        a_max = b_end[dirn] + top * (1.0 / LOG2E)
        rows = jnp.concatenate([m_loc, den, b[dirn], jnp.broadcast_to(a_max, (1, t)),
                                jnp.broadcast_to(b_end[dirn], (1, t)),
                                jnp.zeros((ROWS_PER_DIR - 5, t), F32)], axis=0)
        out.append((nl_t, u, rows))
    return out


def _mlstm_state_start(dirn, q, u, rows, s_ref, m_ref):
    a_max, b_end = rows[3:4, 0:1], rows[4:5, 0:1]
    m_prev = m_ref[dirn]
    s_old = s_ref[dirn]
    inter = _dot_nt(s_old.astype(BF16), q)
    m_new = jnp.maximum(b_end + m_prev, a_max)
    s_ref[dirn] = jnp.exp(b_end + m_prev - m_new) * s_old + jnp.exp(a_max - m_new) * u
    m_ref[dirn] = m_new
    return inter, m_prev


def _mlstm_state_finish(inter, m_prev, nl_t, rows):
    m_loc, den_loc, b = rows[0:1], rows[1:2], rows[2:3]
    m_inter = b + m_prev
    m_t = jnp.maximum(m_inter, m_loc)
    w_inter = jnp.exp(m_inter - m_t)
    w_loc = jnp.exp(m_loc - m_t)
    den = w_inter * inter[M_DV:M_DV + 1] + w_loc * den_loc
    scale = 1.0 / jnp.maximum(jnp.abs(den), jnp.exp(-m_t))
    return (w_inter * scale) * inter[:M_DV] + (w_loc * scale) * nl_t


def _mlstm_kernel(q_ref, qt_ref, k_ref, kt_ref, v_ref, vt_ref, g_ref, gt_ref, o_ref, ot_ref,
                  nl_s, u_s, rows_s, ht_s, s_ref, m_ref):
    s_ref[...] = jnp.zeros_like(s_ref)
    m_ref[...] = jnp.zeros_like(m_ref)
    tail_cols = pl.ds(SEQ, TAIL)

    def chunk(c):
        r0 = pl.multiple_of(c * MT, MT)
        return pl.ds(r0, MT)

    def keep_local(c, cols, per_dir):
        for dirn, (nl_t, u, rows) in enumerate(per_dir):
            nl_s[dirn, :, cols] = nl_t
            u_s[dirn, c] = u
            rows_s[dirn, :, cols] = rows

    def local_pass(i, carry):
        group = [i * LOCAL_GROUP + j for j in range(LOCAL_GROUP)]
        started = [_mlstm_local_start(q_ref[chunk(c), :], k_ref[chunk(c), :], v_ref[chunk(c), :],
                                      g_ref[:, chunk(c)], False) for c in group]
        results = [_mlstm_local_finish(*st) for st in started]
        for c, per_dir in zip(group, results):
            keep_local(c, chunk(c), per_dir)
        return carry

    def state_steps(items):
        started = [_mlstm_state_start(dirn, q, u_s[dirn, c], rows_s[dirn, :, cols], s_ref, m_ref)
                   for dirn, q, c, cols in items]
        return [_mlstm_state_finish(inter, m_prev, nl_s[dirn, :, cols], rows_s[dirn, :, cols])
                for (inter, m_prev), (dirn, _, c, cols) in zip(started, items)]

    def visit_items(i):
        chunks = []
        for j in range(STATE_GROUP):
            step = i * STATE_GROUP + j
            chunks += [(0, step), (1, N_MCHUNK - 1 - step)]
        return [(dirn, q_ref[chunk(c), :], c, chunk(c)) for dirn, c in chunks]

    def first_visits(i, carry):
        items = visit_items(i)
        for (_, _, _, cols), h_t in zip(items, state_steps(items)):
            ht_s[:, cols] = h_t
        return carry

    def second_visits(i, carry):
        items = visit_items(i)
        for (_, _, _, cols), h_t in zip(items, state_steps(items)):
            o_ref[cols, :] = jnp.transpose((ht_s[:, cols] + h_t).astype(o_ref.dtype))
        return carry

    keep_local(N_MCHUNK, tail_cols, _mlstm_local_finish(
        *_mlstm_local_start(qt_ref[...], kt_ref[...], vt_ref[...], gt_ref[...], True)))
    lax.fori_loop(0, N_MCHUNK // LOCAL_GROUP, local_pass, 0)
    ht_s[:, tail_cols] = state_steps([(0, qt_ref[...], N_MCHUNK, tail_cols)])[0]
    half = N_MCHUNK // 2 // STATE_GROUP
    lax.fori_loop(0, half, first_visits, 0)
    lax.fori_loop(half, 2 * half, second_visits, 0)
    h_tail = ht_s[:, tail_cols] + state_steps([(1, qt_ref[...], N_MCHUNK, tail_cols)])[0]
    ot_ref[...] = jnp.transpose(h_tail.astype(ot_ref.dtype))


def _mlstm(q, k, xm, grow):
    tail_blk = N_REAL // TAIL
    real = lambda w: pl.BlockSpec((SEQ, w), lambda b, h: (b, h))
    tail = lambda w: pl.BlockSpec((TAIL, w), lambda b, h: (tail_blk + b, h))
    lp = SEQ + TAIL
    return pl.pallas_call(
        _mlstm_kernel,
        grid=(BATCH, M_HEADS),
        in_specs=[real(M_DK), tail(M_DK), real(M_DK), tail(M_DK), real(M_DV), tail(M_DV),
                  pl.BlockSpec((GATE_PAD, SEQ), lambda b, h: (h, b)),
                  pl.BlockSpec((GATE_PAD, TAIL), lambda b, h: (h, tail_blk + b))],
        out_specs=[pl.BlockSpec((SEQ, M_DV), lambda b, h: (b, h)),
                   pl.BlockSpec((TAIL, M_DV), lambda b, h: (b, h))],
        out_shape=[jax.ShapeDtypeStruct((N_REAL, M_INNER), BF16),
                   jax.ShapeDtypeStruct((BATCH * TAIL, M_INNER), BF16)],
        scratch_shapes=[pltpu.VMEM((2, M_DV, lp), F32),
                        pltpu.VMEM((2, N_MCHUNK + 1, STATE_ROWS, M_DK), F32),
                        pltpu.VMEM((2, ROWS_PER_DIR, lp), F32),
                        pltpu.VMEM((M_DV, lp), F32),
                        pltpu.VMEM((2, STATE_ROWS, M_DK), F32),
                        pltpu.VMEM((2, 1, 1), F32)],
        compiler_params=_params("parallel", "parallel"),
        name="mlstm",
    )(q, q, k, k, xm, xm, grow, grow)


PAIR = 2 * NA_DH


def _natten_kernel(q_ref, qt_ref, k_ref, kt_ref, v_ref, vt_ref, bias_ref, o_ref, ot_ref):
    lane = lax.broadcasted_iota(jnp.int32, (1, PAIR), 1)
    first = lane < NA_DH
    k_tail = kt_ref[...]
    v_tail = vt_ref[...]

    def attend(q, keys, vals, bias_t):
        n = q.shape[0]
        zero = jnp.zeros_like(q)
        qs = jnp.concatenate([jnp.where(first, q, zero), jnp.where(first, zero, q)], axis=0)
        s = _dot_nt(keys, qs) + bias_t
        e = jnp.exp2(s - jnp.max(s, axis=0, keepdims=True))
        o = _dot(jnp.transpose(e.astype(BF16)), vals)
        inv = jnp.broadcast_to(1.0 / jnp.sum(e, axis=0, keepdims=True), (PAIR, 2 * n))
        o = o * jnp.transpose(inv)
        return jnp.where(first, o[:n], o[n:])

    def row_block(r, carry):
        r0 = jnp.clip(r - NA_WIN_H // 2, 0, ROWS - NA_WIN_H)
        qrows = pl.ds(pl.multiple_of(r * GRID_W, GRID_W), GRID_W)
        krows = pl.ds(pl.multiple_of(r0 * GRID_W, GRID_W), NA_WIN_H * GRID_W)
        keys = jnp.concatenate([k_ref[krows, :], k_tail], axis=0)
        vals = jnp.concatenate([v_ref[krows, :], v_tail], axis=0)
        o = attend(q_ref[qrows, :], keys, vals, bias_ref[r - r0])
        o_ref[qrows, :] = o.astype(o_ref.dtype)
        return carry

    lax.fori_loop(0, ROWS, row_block, 0, unroll=8)
    tb = bias_ref[0, NA_WIN_H * GRID_W:, :]
    tail_bias = jnp.concatenate([jnp.broadcast_to(tb[:, 0:1], (TAIL, TAIL)),
                                 jnp.broadcast_to(tb[:, NA_DH:NA_DH + 1], (TAIL, TAIL))], axis=1)
    ot_ref[...] = attend(qt_ref[...], k_tail, v_tail, tail_bias).astype(ot_ref.dtype)


def _natten(qn, kn, vn, bias):
    tail_blk = N_REAL // TAIL
    real = pl.BlockSpec((SEQ, PAIR), lambda b, p: (b, p))
    tail = pl.BlockSpec((TAIL, PAIR), lambda b, p: (tail_blk + b, p))
    return pl.pallas_call(
        _natten_kernel,
        grid=(BATCH, NA_HEADS // 2),
        in_specs=[real, tail, real, tail, real, tail,
                  pl.BlockSpec((None, NA_WIN_H, NA_WIN_H * GRID_W + TAIL, PAIR), lambda b, p: (p, 0, 0, 0))],
        out_specs=[pl.BlockSpec((SEQ, PAIR), lambda b, p: (b, p)),
                   pl.BlockSpec((TAIL, PAIR), lambda b, p: (b, p))],
        out_shape=[jax.ShapeDtypeStruct((N_REAL, NA_INNER), BF16),
                   jax.ShapeDtypeStruct((BATCH * TAIL, NA_INNER), BF16)],
        compiler_params=_params("parallel", "parallel"),
        name="natten",
    )(qn, qn, kn, kn, vn, vn, bias)


def _natten_bias(rpb, meta_bias):
    qc = jnp.arange(GRID_W)
    kc = jnp.arange(GRID_W)
    win0 = jnp.clip(qc - NA_WIN_W // 2, 0, GRID_W - NA_WIN_W)
    ok = (kc[:, None] >= win0[None, :]) & (kc[:, None] < win0[None, :] + NA_WIN_W)
    dc = jnp.clip(kc[:, None] - qc[None, :], -(NA_WIN_W - 1), NA_WIN_W - 1) + NA_WIN_W - 1
    onehot = (dc[None] == jnp.arange(2 * NA_WIN_W - 1)[:, None, None]).astype(F32)
    t1 = jnp.einsum('hdj,jkq->dkhq', rpb.astype(F32), onehot, precision=lax.Precision.HIGHEST)
    t1 = jnp.where(ok[None, :, None, :], t1, MASKED) * LOG2E
    t1 = jnp.transpose(t1.reshape(2 * NA_WIN_H - 1, GRID_W, NA_HEADS // 2, PAIR), (2, 0, 1, 3))
    met = jnp.concatenate([jnp.full((NA_HEADS, NPAD), MASKED, F32), meta_bias.astype(F32)], axis=1) * LOG2E
    met = jnp.broadcast_to(met.reshape(NA_HEADS // 2, 2, TAIL, 1), (NA_HEADS // 2, 2, TAIL, NA_DH))
    met = jnp.transpose(met, (0, 2, 1, 3)).reshape(NA_HEADS // 2, TAIL, PAIR)
    rows = [jnp.concatenate([t1[:, NA_WIN_H - 1 - dl:2 * NA_WIN_H - 1 - dl].reshape(-1, NA_WIN_H * GRID_W, PAIR),
                             met], axis=1) for dl in range(NA_WIN_H)]
    return jnp.stack(rows, axis=1)


def _merge_kernel(hsr_ref, hst_ref, ybr_ref, ybt_ref, xr_ref, xt_ref, xc_ref, og_ref, gab_ref,
                  ng_ref, sk_ref, wa_ref, wb_ref, wo_ref, o_ref):
    def body(hs_ref, yb_ref, h_ref):
        hs = hs_ref[...].astype(F32)
        parts = []
        for hd in range(M_HEADS):
            sl = hs[:, hd * M_DV:(hd + 1) * M_DV]
            parts.append(sl * lax.rsqrt(jnp.mean(sl * sl, axis=-1, keepdims=True) + EPS))
        hn = jnp.concatenate(parts, axis=1) * ng_ref[...]
        y_a = _sigmoid(og_ref[...].astype(F32)) * (hn + sk_ref[...] * xc_ref[...].astype(F32))
        gab = gab_ref[...].astype(F32)
        mix = (_sigmoid(gab[:, :D_MODEL]) * _dot(y_a.astype(BF16), wa_ref[...])
               + _sigmoid(gab[:, D_MODEL:]) * _dot(yb_ref[...], wb_ref[...]))
        o_ref[...] = h_ref[...] + _dot(mix.astype(BF16), wo_ref[...])

    _real_or_tail(body, hsr_ref, hst_ref, ybr_ref, ybt_ref, xr_ref, xt_ref)


def _merge(hs_real, hs_tail, yb_real, yb_tail, x2d, tailh, xc, og, gab, ng, sk, wa, wb, wo):
    return pl.pallas_call(
        _merge_kernel,
        grid=(N_TILES,),
        in_specs=[_real_spec(M_INNER), _tail_spec(M_INNER), _real_spec(NA_INNER), _tail_spec(NA_INNER),
                  _real_spec(D_MODEL), _tail_spec(D_MODEL),
                  _row_spec(M_INNER), _row_spec(M_INNER), _row_spec(2 * D_MODEL)]
                 + [_const_spec(a.shape) for a in (ng, sk, wa, wb, wo)],
        out_specs=_row_spec(D_MODEL),
        out_shape=jax.ShapeDtypeStruct((NT, D_MODEL), F32),
        compiler_params=_params("parallel"),
        name="merge",
    )(hs_real, hs_tail, yb_real, yb_tail, x2d, tailh, xc, og, gab, ng, sk, wa, wb, wo)


def _ffn_kernel(h_ref, g2_ref, w1_ref, w2_ref, or_ref, ot_ref):
    def body(o_ref):
        h = h_ref[...]
        xn = (h * lax.rsqrt(jnp.mean(h * h, axis=-1, keepdims=True) + EPS) * g2_ref[...]).astype(BF16)
        z = jnp.maximum(_dot(xn, w1_ref[...]), 0.0)
        o_ref[...] = h + _dot((z * z).astype(BF16), w2_ref[...])

    _real_or_tail(body, or_ref, ot_ref)


def _ffn(h1, g2, w1, w2):
    return pl.pallas_call(
        _ffn_kernel,
        grid=(N_TILES,),
        in_specs=[_row_spec(D_MODEL), _const_spec(g2.shape), _const_spec(w1.shape), _const_spec(w2.shape)],
        out_specs=[_real_spec(D_MODEL), _tail_spec(D_MODEL)],
        out_shape=[jax.ShapeDtypeStruct((N_REAL, D_MODEL), F32),
                   jax.ShapeDtypeStruct((BATCH * TAIL, D_MODEL), F32)],
        compiler_params=_params("arbitrary"),
        name="ffn",
    )(h1, g2, w1, w2)


def kernel(x, meta_tokens, norm1_g, w_in, mlstm_conv_w, mlstm_conv_b, mlstm_wq, mlstm_wk, mlstm_gate_b, mlstm_norm_g, mlstm_skip, na_q_norm_g, na_k_norm_g, na_rpb, na_meta_bias, w_branch_a, w_branch_b, w_out, norm2_g, w_ff1, w_ff2):
    x2d = x.astype(F32).reshape(N_REAL, D_MODEL)
    tail = jnp.concatenate([jnp.zeros((NPAD, D_MODEL), F32), meta_tokens.astype(F32)], axis=0)
    tailh = jnp.tile(tail, (BATCH, 1))

    w_all = w_in.astype(BF16)
    pad_gates = lambda a: jnp.pad(
        jnp.swapaxes(a.reshape(a.shape[0], 4, M_HEADS), 1, 2),
        ((0, 0), (0, 0), (0, GATE_PAD - 4))).reshape(a.shape[0], M_HEADS * GATE_PAD)
    wg = jnp.pad(pad_gates(w_all[:, OFF_MG:OFF_Q]), ((0, 0), (0, LANES - M_HEADS * GATE_PAD)))
    gb = jnp.pad(pad_gates(mlstm_gate_b.astype(F32).reshape(1, 4 * M_HEADS)),
                 ((0, 0), (0, LANES - M_HEADS * GATE_PAD)))
    bd = jnp.kron(jnp.eye(NORM_GROUP // NA_DH, dtype=F32),
                  jnp.full((NA_DH, NA_DH), 1.0 / NA_DH, F32)).astype(BF16)
    qg = jnp.tile(na_q_norm_g.astype(F32), NA_HEADS)[None, :] * (NA_DH ** -0.5 * LOG2E)
    kg = jnp.tile(na_k_norm_g.astype(F32), NA_HEADS)[None, :]

    xb = x.astype(F32)
    xedge = jnp.concatenate([xb[:, SEQ - 1, :], xb[:, 0, :]], axis=0)
    wqk = jnp.concatenate([mlstm_wq, mlstm_wk], axis=-1).astype(BF16)
    vm, og, grow, qn, kn, vn, gab, xc, qm, km = _inproj(
        x2d, tailh, xedge, norm1_g.astype(F32)[None, :], w_all, wg, gb, qg, kg, bd,
        mlstm_conv_w.astype(F32).reshape(3, M_INNER), mlstm_conv_b.astype(F32)[None, :], wqk)

    hs_real, hs_tail = _mlstm(qm, km, vm, grow)
    yb_real, yb_tail = _natten(qn, kn, vn, _natten_bias(na_rpb, na_meta_bias))

    h1 = _merge(hs_real, hs_tail, yb_real, yb_tail, x2d, tailh, xc, og, gab,
                mlstm_norm_g.astype(F32).reshape(1, M_INNER), mlstm_skip.astype(F32)[None, :],
                w_branch_a.astype(BF16), w_branch_b.astype(BF16), w_out.astype(BF16))
    out_real, _ = _ffn(h1, norm2_g.astype(F32)[None, :], w_ff1.astype(BF16), w_ff2.astype(BF16))
    return out_real.reshape(BATCH, SEQ, D_MODEL)
```

```python
import jax
import jax.numpy as jnp
from jax import lax
from jax.experimental import pallas as pl
from jax.experimental.pallas import tpu as pltpu

D_MODEL = 1024
BATCH = 4
SEQ = 4096
N_META = 16
GRID_W = 64
ROWS = SEQ // GRID_W
M_HEADS = 4
M_DV = 256
M_DK = 128
M_INNER = M_HEADS * M_DV
NA_HEADS = 8
NA_DH = 64
NA_INNER = NA_HEADS * NA_DH
NA_WIN_H = 8
NA_WIN_W = 16
D_FF = 4 * D_MODEL
EPS = 1e-6
NEG_LOG_GATE = -1e9
MASKED = -1e30
LOG2E = 1.4426950408889634

TAIL = 128
NPAD = TAIL - N_META
N_REAL = BATCH * SEQ
NT = N_REAL + BATCH * TAIL
TM = 512
N_TILES = NT // TM
N_REAL_TILES = N_REAL // TM

LANES = 128
BF16_ROWS = 16
VMEM_LIMIT = 56 * 1024 * 1024

F32 = jnp.float32
BF16 = jnp.bfloat16


def _dot(a, b):
    return jnp.dot(a, b, preferred_element_type=F32)


def _dot_nt(a, b):
    return lax.dot_general(a, b, (((1,), (1,)), ((), ())), preferred_element_type=F32)


def _sigmoid(z):
    return 0.5 * jnp.tanh(0.5 * z) + 0.5


def _log_sigmoid(z):
    return jnp.minimum(z, 0.0) - jnp.log1p(jnp.exp(-jnp.abs(z)))


def _const_spec(shape):
    nd = len(shape)
    return pl.BlockSpec(shape, lambda *_: (0,) * nd, pipeline_mode=pl.Buffered(1))


def _params(*sem):
    return pltpu.CompilerParams(dimension_semantics=sem, vmem_limit_bytes=VMEM_LIMIT)


GATE_PAD = 8
OFF_MX = 0
OFF_MO = OFF_MX + M_INNER
OFF_MG = OFF_MO + M_INNER
OFF_Q = OFF_MG + 4 * M_HEADS
OFF_G = OFF_Q + 3 * NA_INNER
D_IN_PROJ = OFF_G + 2 * D_MODEL


HALO = 8
HALO_PAD = BF16_ROWS
TILES_PER_SEQ = SEQ // TM
NORM_GROUP = 256


def _inproj_kernel(x_ref, tail_ref, xprev_ref, xnext_ref, xedge_ref, g1_ref, w_ref, wg_ref,
                   gb_ref, qg_ref, kg_ref, bd_ref, cw_ref, cb_ref, wqk_ref,
                   vm_ref, og_ref, grow_ref, q_ref, k_ref, v_ref, gab_ref, xc_ref, qm_ref, km_ref,
                   wqkv_s, wgab_s):
    i = pl.program_id(0)

    @pl.when(i == 0)
    def _():
        wqkv_s[...] = w_ref[:, OFF_Q:OFF_G]
        wgab_s[...] = w_ref[:, OFF_G:D_IN_PROJ]

    def norm1(h):
        ms = jnp.mean(h * h, axis=-1, keepdims=True)
        return (h * lax.rsqrt(ms + EPS) * g1_ref[...]).astype(BF16)

    def body(h_ref, is_tail):
        xn = norm1(h_ref[...])
        if is_tail:
            halo = jnp.concatenate([xedge_ref[...], jnp.zeros((HALO_PAD - 2 * BATCH, D_MODEL), F32)], axis=0)
        else:
            first = i % TILES_PER_SEQ == 0
            last = i % TILES_PER_SEQ == TILES_PER_SEQ - 1
            prev = jnp.where(first, tail_ref[TAIL - 1:TAIL, :], xprev_ref[HALO - 1:HALO, :])
            nxt = jnp.where(last, tail_ref[0:1, :], xnext_ref[0:1, :])
            halo = jnp.concatenate([prev, nxt, jnp.zeros((HALO_PAD - 2, D_MODEL), F32)], axis=0)
        xm_ext = _dot(jnp.concatenate([xn, norm1(halo)], axis=0), w_ref[:, OFF_MX:OFF_MO])
        og = _dot(xn, w_ref[:, OFF_MO:OFF_MG])
        gates = _dot(xn, wg_ref[...]) + gb_ref[...]
        gab = _dot(xn, wgab_s[...])
        qkv = _dot(xn, wqkv_s[...])
        xm = xm_ext[:TM]
        rows = lax.broadcasted_iota(jnp.int32, (TM, M_INNER), 0)
        x_m1 = pltpu.roll(xm, 1, axis=0)
        x_p1 = pltpu.roll(xm, TM - 1, axis=0)
        if is_tail:
            for b in range(BATCH):
                x_m1 = jnp.where(rows == b * TAIL, xm_ext[TM + b:TM + b + 1], x_m1)
                x_p1 = jnp.where(rows == b * TAIL + TAIL - 1, xm_ext[TM + BATCH + b:TM + BATCH + b + 1], x_p1)
        else:
            x_m1 = jnp.where(rows == 0, xm_ext[TM:TM + 1], x_m1)
            x_p1 = jnp.where(rows == TM - 1, xm_ext[TM + 1:TM + 2], x_p1)
        vm_ref[...] = xm.astype(BF16)
        z = cw_ref[0:1, :] * x_m1 + cw_ref[1:2, :] * xm + cw_ref[2:3, :] * x_p1 + cb_ref[...]
        xc = z * _sigmoid(z)
        xc_ref[...] = xc.astype(BF16)
        for hd in range(M_HEADS):
            qk = _dot(xc[:, hd * M_DV:(hd + 1) * M_DV].astype(BF16), wqk_ref[hd])
            qm_ref[:, hd * M_DK:(hd + 1) * M_DK] = qk[:, :M_DK].astype(BF16)
            km_ref[:, hd * M_DK:(hd + 1) * M_DK] = (qk[:, M_DK:] * (M_DK ** -0.5)).astype(BF16)
        og_ref[...] = og.astype(BF16)
        grow_ref[...] = jnp.transpose(gates)[:M_HEADS * GATE_PAD, :]
        gab_ref[...] = gab.astype(BF16)
        uq = qkv[:, :NA_INNER]
        uk = qkv[:, NA_INNER:2 * NA_INNER]
        def head_mean(sq):
            return jnp.concatenate([_dot(sq[:, j:j + NORM_GROUP].astype(BF16), bd_ref[...])
                                    for j in range(0, NA_INNER, NORM_GROUP)], axis=1)

        msq = head_mean(uq * uq)
        msk = head_mean(uk * uk)
        q_ref[...] = (uq * lax.rsqrt(msq + EPS) * qg_ref[...]).astype(BF16)
        k_ref[...] = (uk * lax.rsqrt(msk + EPS) * kg_ref[...]).astype(BF16)
        v_ref[...] = qkv[:, 2 * NA_INNER:].astype(BF16)

    pl.when(i < N_REAL_TILES)(lambda: body(x_ref, False))
    pl.when(i == N_REAL_TILES)(lambda: body(tail_ref, True))


def _real_or_tail(body, *ref_pairs):
    i = pl.program_id(0)
    pl.when(i < N_REAL_TILES)(lambda: body(*ref_pairs[0::2]))
    pl.when(i == N_REAL_TILES)(lambda: body(*ref_pairs[1::2]))


def _real_spec(width):
    return pl.BlockSpec((TM, width), lambda i: (jnp.minimum(i, N_REAL_TILES - 1), 0))


def _tail_spec(width):
    return pl.BlockSpec((TM, width), lambda i: (0, 0))


def _row_spec(width):
    return pl.BlockSpec((TM, width), lambda i: (i, 0))


def _inproj(x2d, tailh, xedge, g1, w_all, wg, gb, qg, kg, bd, cw, cb, wqk):
    halo_blocks = N_REAL // HALO
    per_tile = TM // HALO
    consts = (g1, w_all, wg, gb, qg, kg, bd, cw, cb, wqk)
    return pl.pallas_call(
        _inproj_kernel,
        grid=(N_TILES,),
        in_specs=[_real_spec(D_MODEL), _tail_spec(D_MODEL),
                  pl.BlockSpec((HALO, D_MODEL), lambda i: (jnp.clip(i * per_tile - 1, 0, halo_blocks - 1), 0)),
                  pl.BlockSpec((HALO, D_MODEL), lambda i: (jnp.clip((i + 1) * per_tile, 0, halo_blocks - 1), 0)),
                  _const_spec(xedge.shape)]
                 + [_const_spec(a.shape) for a in consts],
        out_specs=[_row_spec(M_INNER), _row_spec(M_INNER),
                   pl.BlockSpec((M_HEADS * GATE_PAD, TM), lambda i: (0, i)),
                   _row_spec(NA_INNER), _row_spec(NA_INNER), _row_spec(NA_INNER), _row_spec(2 * D_MODEL),
                   _row_spec(M_INNER), _row_spec(M_HEADS * M_DK), _row_spec(M_HEADS * M_DK)],
        out_shape=[jax.ShapeDtypeStruct((NT, M_INNER), BF16),
                   jax.ShapeDtypeStruct((NT, M_INNER), BF16),
                   jax.ShapeDtypeStruct((M_HEADS * GATE_PAD, NT), F32),
                   jax.ShapeDtypeStruct((NT, NA_INNER), BF16),
                   jax.ShapeDtypeStruct((NT, NA_INNER), BF16),
                   jax.ShapeDtypeStruct((NT, NA_INNER), BF16),
                   jax.ShapeDtypeStruct((NT, 2 * D_MODEL), BF16),
                   jax.ShapeDtypeStruct((NT, M_INNER), BF16),
                   jax.ShapeDtypeStruct((NT, M_HEADS * M_DK), BF16),
                   jax.ShapeDtypeStruct((NT, M_HEADS * M_DK), BF16)],
        scratch_shapes=[pltpu.VMEM((D_MODEL, OFF_G - OFF_Q), BF16),
                        pltpu.VMEM((D_MODEL, D_IN_PROJ - OFF_G), BF16)],
        compiler_params=_params("arbitrary"),
        name="inproj",
    )(x2d, tailh, x2d, x2d, xedge, *consts)


MT = 256
N_MCHUNK = SEQ // MT
STATE_ROWS = M_DV + BF16_ROWS
ROWS_PER_DIR = 8
LOCAL_GROUP = 16
STATE_GROUP = 8


def _split3(x):
    hi = x.astype(BF16)
    r1 = x - hi.astype(F32)
    mid = r1.astype(BF16)
    lo = (r1 - mid.astype(F32)).astype(BF16)
    return hi, mid, lo


def _mlstm_local_start(q, k, v, g8, is_tail):
    t = q.shape[0]
    li = [g8[0:1], g8[2:3]]
    lf = [_log_sigmoid(g8[1:2]), _log_sigmoid(g8[3:4])]
    if is_tail:
        pad = lax.broadcasted_iota(jnp.int32, (1, t), 1) < NPAD
        li = [jnp.where(pad, NEG_LOG_GATE, x) for x in li]
        lf = [jnp.where(pad, 0.0, x) for x in lf]
    si = lax.broadcasted_iota(jnp.int32, (t, t), 0)
    ti = lax.broadcasted_iota(jnp.int32, (t, t), 1)
    hi, mid, lo = _split3(jnp.concatenate(lf, axis=0))
    lhs = jnp.concatenate([hi, mid, lo, jnp.zeros((BF16_ROWS - 6, t), BF16)], axis=0)
    pref = _dot(lhs, (si <= ti).astype(BF16))
    pre_f = pref[0:1] + pref[2:3] + pref[4:5]
    pre_b = pref[1:2] + pref[3:4] + pref[5:6]
    b_end = [pre_f[:, t - 1:t], pre_b[:, t - 1:t]]
    b = [pre_f, b_end[1] - pre_b + lf[1]]
    s_t = _dot_nt(k, q)
    v_tb = jnp.transpose(v.astype(BF16))
    return li, b, b_end, s_t, v_tb, k


def _mlstm_local_finish(li, b, b_end, s_t, v_tb, k):
    t = s_t.shape[0]
    si = lax.broadcasted_iota(jnp.int32, (t, t), 0)
    ti = lax.broadcasted_iota(jnp.int32, (t, t), 1)
    kf = k.astype(F32)
    v_ext = jnp.concatenate([v_tb, jnp.ones((STATE_ROWS - M_DV, t), BF16)], axis=0)
    out = []
    for dirn in range(2):
        g2 = (li[dirn] - b[dirn]) * LOG2E
        g_col = jnp.transpose(jnp.broadcast_to(g2, (LANES, t)))
        g_st = jnp.where((si <= ti) if dirn == 0 else (si >= ti),
                         jnp.concatenate([g_col] * (t // LANES), axis=1), MASKED)
        g_max = jnp.max(g_st, axis=0, keepdims=True)
        p_t = s_t * jnp.exp2(g_st - g_max)
        den = jnp.sum(p_t, axis=0, keepdims=True)
        nl_t = _dot(v_tb, p_t.astype(BF16))
        m_loc = b[dirn] + g_max * (1.0 / LOG2E)
        top = jnp.max(g2, axis=1, keepdims=True)
        kw = (kf * jnp.exp2(g_col - top)).astype(BF16)
        u = _dot(v_ext, kw)
        a_max = b_end[dirn] + top * (1.0 / LOG2E)
        rows = jnp.concatenate([m_loc, den, b[dirn], jnp.broadcast_to(a_max, (1, t)),
                                jnp.broadcast_to(b_end[dirn], (1, t)),
                                jnp.zeros((ROWS_PER_DIR - 5, t), F32)], axis=0)
        out.append((nl_t, u, rows))
    return out


def _mlstm_state_start(dirn, q, u, rows, s_ref, m_ref):
    a_max, b_end = rows[3:4, 0:1], rows[4:5, 0:1]
    m_prev = m_ref[dirn]
    s_old = s_ref[dirn]
    inter = _dot_nt(s_old.astype(BF16), q)
    m_new = jnp.maximum(b_end + m_prev, a_max)
    s_ref[dirn] = jnp.exp(b_end + m_prev - m_new) * s_old + jnp.exp(a_max - m_new) * u
    m_ref[dirn] = m_new
    return inter, m_prev


def _mlstm_state_finish(inter, m_prev, nl_t, rows):
    m_loc, den_loc, b = rows[0:1], rows[1:2], rows[2:3]
    m_inter = b + m_prev
    m_t = jnp.maximum(m_inter, m_loc)
    w_inter = jnp.exp(m_inter - m_t)
    w_loc = jnp.exp(m_loc - m_t)
    den = w_inter * inter[M_DV:M_DV + 1] + w_loc * den_loc
    scale = 1.0 / jnp.maximum(jnp.abs(den), jnp.exp(-m_t))
    return (w_inter * scale) * inter[:M_DV] + (w_loc * scale) * nl_t


def _mlstm_kernel(q_ref, qt_ref, k_ref, kt_ref, v_ref, vt_ref, g_ref, gt_ref, o_ref, ot_ref,
                  nl_s, u_s, rows_s, ht_s, s_ref, m_ref):
    s_ref[...] = jnp.zeros_like(s_ref)
    m_ref[...] = jnp.zeros_like(m_ref)
    tail_cols = pl.ds(SEQ, TAIL)

    def chunk(c):
        r0 = pl.multiple_of(c * MT, MT)
        return pl.ds(r0, MT)

    def keep_local(c, cols, per_dir):
        for dirn, (nl_t, u, rows) in enumerate(per_dir):
            nl_s[dirn, :, cols] = nl_t
            u_s[dirn, c] = u
            rows_s[dirn, :, cols] = rows

    def local_pass(i, carry):
        group = [i * LOCAL_GROUP + j for j in range(LOCAL_GROUP)]
        started = [_mlstm_local_start(q_ref[chunk(c), :], k_ref[chunk(c), :], v_ref[chunk(c), :],
                                      g_ref[:, chunk(c)], False) for c in group]
        results = [_mlstm_local_finish(*st) for st in started]
        for c, per_dir in zip(group, results):
            keep_local(c, chunk(c), per_dir)
        return carry

    def state_steps(items):
        started = [_mlstm_state_start(dirn, q, u_s[dirn, c], rows_s[dirn, :, cols], s_ref, m_ref)
                   for dirn, q, c, cols in items]
        return [_mlstm_state_finish(inter, m_prev, nl_s[dirn, :, cols], rows_s[dirn, :, cols])
                for (inter, m_prev), (dirn, _, c, cols) in zip(started, items)]

    def visit_items(i):
        chunks = []
        for j in range(STATE_GROUP):
            step = i * STATE_GROUP + j
            chunks += [(0, step), (1, N_MCHUNK - 1 - step)]
        return [(dirn, q_ref[chunk(c), :], c, chunk(c)) for dirn, c in chunks]

    def first_visits(i, carry):
        items = visit_items(i)
        for (_, _, _, cols), h_t in zip(items, state_steps(items)):
            ht_s[:, cols] = h_t
        return carry

    def second_visits(i, carry):
        items = visit_items(i)
        for (_, _, _, cols), h_t in zip(items, state_steps(items)):
            o_ref[cols, :] = jnp.transpose((ht_s[:, cols] + h_t).astype(o_ref.dtype))
        return carry

    keep_local(N_MCHUNK, tail_cols, _mlstm_local_finish(
        *_mlstm_local_start(qt_ref[...], kt_ref[...], vt_ref[...], gt_ref[...], True)))
    lax.fori_loop(0, N_MCHUNK // LOCAL_GROUP, local_pass, 0)
    ht_s[:, tail_cols] = state_steps([(0, qt_ref[...], N_MCHUNK, tail_cols)])[0]
    half = N_MCHUNK // 2 // STATE_GROUP
    lax.fori_loop(0, half, first_visits, 0)
    lax.fori_loop(half, 2 * half, second_visits, 0)
    h_tail = ht_s[:, tail_cols] + state_steps([(1, qt_ref[...], N_MCHUNK, tail_cols)])[0]
    ot_ref[...] = jnp.transpose(h_tail.astype(ot_ref.dtype))


def _mlstm(q, k, xm, grow):
    tail_blk = N_REAL // TAIL
    real = lambda w: pl.BlockSpec((SEQ, w), lambda b, h: (b, h))
    tail = lambda w: pl.BlockSpec((TAIL, w), lambda b, h: (tail_blk + b, h))
    lp = SEQ + TAIL
    return pl.pallas_call(
        _mlstm_kernel,
        grid=(BATCH, M_HEADS),
        in_specs=[real(M_DK), tail(M_DK), real(M_DK), tail(M_DK), real(M_DV), tail(M_DV),
                  pl.BlockSpec((GATE_PAD, SEQ), lambda b, h: (h, b)),
                  pl.BlockSpec((GATE_PAD, TAIL), lambda b, h: (h, tail_blk + b))],
        out_specs=[pl.BlockSpec((SEQ, M_DV), lambda b, h: (b, h)),
                   pl.BlockSpec((TAIL, M_DV), lambda b, h: (b, h))],
        out_shape=[jax.ShapeDtypeStruct((N_REAL, M_INNER), BF16),
                   jax.ShapeDtypeStruct((BATCH * TAIL, M_INNER), BF16)],
        scratch_shapes=[pltpu.VMEM((2, M_DV, lp), F32),
                        pltpu.VMEM((2, N_MCHUNK + 1, STATE_ROWS, M_DK), F32),
                        pltpu.VMEM((2, ROWS_PER_DIR, lp), F32),
                        pltpu.VMEM((M_DV, lp), F32),
                        pltpu.VMEM((2, STATE_ROWS, M_DK), F32),
                        pltpu.VMEM((2, 1, 1), F32)],
        compiler_params=_params("parallel", "parallel"),
        name="mlstm",
    )(q, q, k, k, xm, xm, grow, grow)


PAIR = 2 * NA_DH


def _natten_kernel(q_ref, qt_ref, k_ref, kt_ref, v_ref, vt_ref, bias_ref, o_ref, ot_ref):
    lane = lax.broadcasted_iota(jnp.int32, (1, PAIR), 1)
    first = lane < NA_DH
    k_tail = kt_ref[...]
    v_tail = vt_ref[...]

    def attend(q, keys, vals, bias_t):
        n = q.shape[0]
        zero = jnp.zeros_like(q)
        qs = jnp.concatenate([jnp.where(first, q, zero), jnp.where(first, zero, q)], axis=0)
        s = _dot_nt(keys, qs) + bias_t
        e = jnp.exp2(s - jnp.max(s, axis=0, keepdims=True))
        o = _dot(jnp.transpose(e.astype(BF16)), vals)
        inv = jnp.broadcast_to(1.0 / jnp.sum(e, axis=0, keepdims=True), (PAIR, 2 * n))
        o = o * jnp.transpose(inv)
        return jnp.where(first, o[:n], o[n:])

    def row_block(r, carry):
        r0 = jnp.clip(r - NA_WIN_H // 2, 0, ROWS - NA_WIN_H)
        qrows = pl.ds(pl.multiple_of(r * GRID_W, GRID_W), GRID_W)
        krows = pl.ds(pl.multiple_of(r0 * GRID_W, GRID_W), NA_WIN_H * GRID_W)
        keys = jnp.concatenate([k_ref[krows, :], k_tail], axis=0)
        vals = jnp.concatenate([v_ref[krows, :], v_tail], axis=0)
        o = attend(q_ref[qrows, :], keys, vals, bias_ref[r - r0])
        o_ref[qrows, :] = o.astype(o_ref.dtype)
        return carry

    lax.fori_loop(0, ROWS, row_block, 0, unroll=64)
    tb = bias_ref[0, NA_WIN_H * GRID_W:, :]
    tail_bias = jnp.concatenate([jnp.broadcast_to(tb[:, 0:1], (TAIL, TAIL)),
                                 jnp.broadcast_to(tb[:, NA_DH:NA_DH + 1], (TAIL, TAIL))], axis=1)
    ot_ref[...] = attend(qt_ref[...], k_tail, v_tail, tail_bias).astype(ot_ref.dtype)


def _natten(qn, kn, vn, bias):
    tail_blk = N_REAL // TAIL
    real = pl.BlockSpec((SEQ, PAIR), lambda b, p: (b, p))
    tail = pl.BlockSpec((TAIL, PAIR), lambda b, p: (tail_blk + b, p))
    return pl.pallas_call(
        _natten_kernel,
        grid=(BATCH, NA_HEADS // 2),
        in_specs=[real, tail, real, tail, real, tail,
                  pl.BlockSpec((None, NA_WIN_H, NA_WIN_H * GRID_W + TAIL, PAIR), lambda b, p: (p, 0, 0, 0))],
        out_specs=[pl.BlockSpec((SEQ, PAIR), lambda b, p: (b, p)),
                   pl.BlockSpec((TAIL, PAIR), lambda b, p: (b, p))],
        out_shape=[jax.ShapeDtypeStruct((N_REAL, NA_INNER), BF16),
                   jax.ShapeDtypeStruct((BATCH * TAIL, NA_INNER), BF16)],
        compiler_params=_params("parallel", "parallel"),
        name="natten",
    )(qn, qn, kn, kn, vn, vn, bias)


def _natten_bias(rpb, meta_bias):
    qc = jnp.arange(GRID_W)
    kc = jnp.arange(GRID_W)
    win0 = jnp.clip(qc - NA_WIN_W // 2, 0, GRID_W - NA_WIN_W)
    ok = (kc[:, None] >= win0[None, :]) & (kc[:, None] < win0[None, :] + NA_WIN_W)
    dc = jnp.clip(kc[:, None] - qc[None, :], -(NA_WIN_W - 1), NA_WIN_W - 1) + NA_WIN_W - 1
    onehot = (dc[None] == jnp.arange(2 * NA_WIN_W - 1)[:, None, None]).astype(F32)
    t1 = jnp.einsum('hdj,jkq->dkhq', rpb.astype(F32), onehot, precision=lax.Precision.HIGHEST)
    t1 = jnp.where(ok[None, :, None, :], t1, MASKED) * LOG2E
    t1 = jnp.transpose(t1.reshape(2 * NA_WIN_H - 1, GRID_W, NA_HEADS // 2, PAIR), (2, 0, 1, 3))
    met = jnp.concatenate([jnp.full((NA_HEADS, NPAD), MASKED, F32), meta_bias.astype(F32)], axis=1) * LOG2E
    met = jnp.broadcast_to(met.reshape(NA_HEADS // 2, 2, TAIL, 1), (NA_HEADS // 2, 2, TAIL, NA_DH))
    met = jnp.transpose(met, (0, 2, 1, 3)).reshape(NA_HEADS // 2, TAIL, PAIR)
    rows = [jnp.concatenate([t1[:, NA_WIN_H - 1 - dl:2 * NA_WIN_H - 1 - dl].reshape(-1, NA_WIN_H * GRID_W, PAIR),
                             met], axis=1) for dl in range(NA_WIN_H)]
    return jnp.stack(rows, axis=1)


def _merge_kernel(hsr_ref, hst_ref, ybr_ref, ybt_ref, xr_ref, xt_ref, xc_ref, og_ref, gab_ref,
                  ng_ref, sk_ref, wa_ref, wb_ref, wo_ref, o_ref):
    def body(hs_ref, yb_ref, h_ref):
        hs = hs_ref[...].astype(F32)
        parts = []
        for hd in range(M_HEADS):
            sl = hs[:, hd * M_DV:(hd + 1) * M_DV]
            parts.append(sl * lax.rsqrt(jnp.mean(sl * sl, axis=-1, keepdims=True) + EPS))
        hn = jnp.concatenate(parts, axis=1) * ng_ref[...]
        y_a = _sigmoid(og_ref[...].astype(F32)) * (hn + sk_ref[...] * xc_ref[...].astype(F32))
        gab = gab_ref[...].astype(F32)
        mix = (_sigmoid(gab[:, :D_MODEL]) * _dot(y_a.astype(BF16), wa_ref[...])
               + _sigmoid(gab[:, D_MODEL:]) * _dot(yb_ref[...], wb_ref[...]))
        o_ref[...] = h_ref[...] + _dot(mix.astype(BF16), wo_ref[...])

    _real_or_tail(body, hsr_ref, hst_ref, ybr_ref, ybt_ref, xr_ref, xt_ref)


def _merge(hs_real, hs_tail, yb_real, yb_tail, x2d, tailh, xc, og, gab, ng, sk, wa, wb, wo):
    return pl.pallas_call(
        _merge_kernel,
        grid=(N_TILES,),
        in_specs=[_real_spec(M_INNER), _tail_spec(M_INNER), _real_spec(NA_INNER), _tail_spec(NA_INNER),
                  _real_spec(D_MODEL), _tail_spec(D_MODEL),
                  _row_spec(M_INNER), _row_spec(M_INNER), _row_spec(2 * D_MODEL)]
                 + [_const_spec(a.shape) for a in (ng, sk, wa, wb, wo)],
        out_specs=_row_spec(D_MODEL),
        out_shape=jax.ShapeDtypeStruct((NT, D_MODEL), F32),
        compiler_params=_params("parallel"),
        name="merge",
    )(hs_real, hs_tail, yb_real, yb_tail, x2d, tailh, xc, og, gab, ng, sk, wa, wb, wo)


def _ffn_kernel(h_ref, g2_ref, w1_ref, w2_ref, or_ref, ot_ref):
    def body(o_ref):
        h = h_ref[...]
        xn = (h * lax.rsqrt(jnp.mean(h * h, axis=-1, keepdims=True) + EPS) * g2_ref[...]).astype(BF16)
        z = jnp.maximum(_dot(xn, w1_ref[...]), 0.0)
        o_ref[...] = h + _dot((z * z).astype(BF16), w2_ref[...])

    _real_or_tail(body, or_ref, ot_ref)


def _ffn(h1, g2, w1, w2):
    return pl.pallas_call(
        _ffn_kernel,
        grid=(N_TILES,),
        in_specs=[_row_spec(D_MODEL), _const_spec(g2.shape), _const_spec(w1.shape), _const_spec(w2.shape)],
        out_specs=[_real_spec(D_MODEL), _tail_spec(D_MODEL)],
        out_shape=[jax.ShapeDtypeStruct((N_REAL, D_MODEL), F32),
                   jax.ShapeDtypeStruct((BATCH * TAIL, D_MODEL), F32)],
        compiler_params=_params("arbitrary"),
        name="ffn",
    )(h1, g2, w1, w2)


def kernel(x, meta_tokens, norm1_g, w_in, mlstm_conv_w, mlstm_conv_b, mlstm_wq, mlstm_wk, mlstm_gate_b, mlstm_norm_g, mlstm_skip, na_q_norm_g, na_k_norm_g, na_rpb, na_meta_bias, w_branch_a, w_branch_b, w_out, norm2_g, w_ff1, w_ff2):
    x2d = x.astype(F32).reshape(N_REAL, D_MODEL)
    tail = jnp.concatenate([jnp.zeros((NPAD, D_MODEL), F32), meta_tokens.astype(F32)], axis=0)
    tailh = jnp.tile(tail, (BATCH, 1))

    w_all = w_in.astype(BF16)
    pad_gates = lambda a: jnp.pad(
        jnp.swapaxes(a.reshape(a.shape[0], 4, M_HEADS), 1, 2),
        ((0, 0), (0, 0), (0, GATE_PAD - 4))).reshape(a.shape[0], M_HEADS * GATE_PAD)
    wg = jnp.pad(pad_gates(w_all[:, OFF_MG:OFF_Q]), ((0, 0), (0, LANES - M_HEADS * GATE_PAD)))
    gb = jnp.pad(pad_gates(mlstm_gate_b.astype(F32).reshape(1, 4 * M_HEADS)),
                 ((0, 0), (0, LANES - M_HEADS * GATE_PAD)))
    bd = jnp.kron(jnp.eye(NORM_GROUP // NA_DH, dtype=F32),
                  jnp.full((NA_DH, NA_DH), 1.0 / NA_DH, F32)).astype(BF16)
    qg = jnp.tile(na_q_norm_g.astype(F32), NA_HEADS)[None, :] * (NA_DH ** -0.5 * LOG2E)
    kg = jnp.tile(na_k_norm_g.astype(F32), NA_HEADS)[None, :]

    xb = x.astype(F32)
    xedge = jnp.concatenate([xb[:, SEQ - 1, :], xb[:, 0, :]], axis=0)
    wqk = jnp.concatenate([mlstm_wq, mlstm_wk], axis=-1).astype(BF16)
    vm, og, grow, qn, kn, vn, gab, xc, qm, km = _inproj(
        x2d, tailh, xedge, norm1_g.astype(F32)[None, :], w_all, wg, gb, qg, kg, bd,
        mlstm_conv_w.astype(F32).reshape(3, M_INNER), mlstm_conv_b.astype(F32)[None, :], wqk)

    hs_real, hs_tail = _mlstm(qm, km, vm, grow)
    yb_real, yb_tail = _natten(qn, kn, vn, _natten_bias(na_rpb, na_meta_bias))

    h1 = _merge(hs_real, hs_tail, yb_real, yb_tail, x2d, tailh, xc, og, gab,
                mlstm_norm_g.astype(F32).reshape(1, M_INNER), mlstm_skip.astype(F32)[None, :],
                w_branch_a.astype(BF16), w_branch_b.astype(BF16), w_out.astype(BF16))
    out_real, _ = _ffn(h1, norm2_g.astype(F32)[None, :], w_ff1.astype(BF16), w_ff2.astype(BF16))
    return out_real.reshape(BATCH, SEQ, D_MODEL)
```

```python
import jax
import jax.numpy as jnp
from jax import lax
from jax.experimental import pallas as pl
from jax.experimental.pallas import tpu as pltpu

D_MODEL = 1024
BATCH = 4
SEQ = 4096
N_META = 16
GRID_W = 64
ROWS = SEQ // GRID_W
M_HEADS = 4
M_DV = 256
M_DK = 128
M_INNER = M_HEADS * M_DV
NA_HEADS = 8
NA_DH = 64
NA_INNER = NA_HEADS * NA_DH
NA_WIN_H = 8
NA_WIN_W = 16
D_FF = 4 * D_MODEL
EPS = 1e-6
NEG_LOG_GATE = -1e9
MASKED = -1e30
LOG2E = 1.4426950408889634

TAIL = 128
NPAD = TAIL - N_META
N_REAL = BATCH * SEQ
NT = N_REAL + BATCH * TAIL
TM = 512
N_TILES = NT // TM
N_REAL_TILES = N_REAL // TM

LANES = 128
BF16_ROWS = 16
VMEM_LIMIT = 56 * 1024 * 1024

F32 = jnp.float32
BF16 = jnp.bfloat16


def _dot(a, b):
    return jnp.dot(a, b, preferred_element_type=F32)


def _dot_nt(a, b):
    return lax.dot_general(a, b, (((1,), (1,)), ((), ())), preferred_element_type=F32)


def _sigmoid(z):
    return 0.5 * jnp.tanh(0.5 * z) + 0.5


def _log_sigmoid(z):
    return jnp.minimum(z, 0.0) - jnp.log1p(jnp.exp(-jnp.abs(z)))


def _const_spec(shape):
    nd = len(shape)
    return pl.BlockSpec(shape, lambda *_: (0,) * nd, pipeline_mode=pl.Buffered(1))


def _params(*sem):
    return pltpu.CompilerParams(dimension_semantics=sem, vmem_limit_bytes=VMEM_LIMIT)


GATE_PAD = 8
OFF_MX = 0
OFF_MO = OFF_MX + M_INNER
OFF_MG = OFF_MO + M_INNER
OFF_Q = OFF_MG + 4 * M_HEADS
OFF_G = OFF_Q + 3 * NA_INNER
D_IN_PROJ = OFF_G + 2 * D_MODEL


HALO = 8
HALO_PAD = BF16_ROWS
TILES_PER_SEQ = SEQ // TM
NORM_GROUP = 256


def _inproj_kernel(x_ref, tail_ref, xprev_ref, xnext_ref, xedge_ref, g1_ref, w_ref, wg_ref,
                   gb_ref, qg_ref, kg_ref, bd_ref, cw_ref, cb_ref, wqk_ref,
                   vm_ref, og_ref, grow_ref, q_ref, k_ref, v_ref, gab_ref, xc_ref, qm_ref, km_ref,
                   wqkv_s, wgab_s):
    i = pl.program_id(0)

    @pl.when(i == 0)
    def _():
        wqkv_s[...] = w_ref[:, OFF_Q:OFF_G]
        wgab_s[...] = w_ref[:, OFF_G:D_IN_PROJ]

    def norm1(h):
        ms = jnp.mean(h * h, axis=-1, keepdims=True)
        return (h * lax.rsqrt(ms + EPS) * g1_ref[...]).astype(BF16)

    def body(h_ref, is_tail):
        xn = norm1(h_ref[...])
        if is_tail:
            halo = jnp.concatenate([xedge_ref[...], jnp.zeros((HALO_PAD - 2 * BATCH, D_MODEL), F32)], axis=0)
        else:
            first = i % TILES_PER_SEQ == 0
            last = i % TILES_PER_SEQ == TILES_PER_SEQ - 1
            prev = jnp.where(first, tail_ref[TAIL - 1:TAIL, :], xprev_ref[HALO - 1:HALO, :])
            nxt = jnp.where(last, tail_ref[0:1, :], xnext_ref[0:1, :])
            halo = jnp.concatenate([prev, nxt, jnp.zeros((HALO_PAD - 2, D_MODEL), F32)], axis=0)
        xm_ext = _dot(jnp.concatenate([xn, norm1(halo)], axis=0), w_ref[:, OFF_MX:OFF_MO])
        og = _dot(xn, w_ref[:, OFF_MO:OFF_MG])
        gates = _dot(xn, wg_ref[...]) + gb_ref[...]
        gab = _dot(xn, wgab_s[...])
        qkv = _dot(xn, wqkv_s[...])
        xm = xm_ext[:TM]
        rows = lax.broadcasted_iota(jnp.int32, (TM, M_INNER), 0)
        x_m1 = pltpu.roll(xm, 1, axis=0)
        x_p1 = pltpu.roll(xm, TM - 1, axis=0)
        if is_tail:
            for b in range(BATCH):
                x_m1 = jnp.where(rows == b * TAIL, xm_ext[TM + b:TM + b + 1], x_m1)
                x_p1 = jnp.where(rows == b * TAIL + TAIL - 1, xm_ext[TM + BATCH + b:TM + BATCH + b + 1], x_p1)
        else:
            x_m1 = jnp.where(rows == 0, xm_ext[TM:TM + 1], x_m1)
            x_p1 = jnp.where(rows == TM - 1, xm_ext[TM + 1:TM + 2], x_p1)
        vm_ref[...] = xm.astype(BF16)
        z = cw_ref[0:1, :] * x_m1 + cw_ref[1:2, :] * xm + cw_ref[2:3, :] * x_p1 + cb_ref[...]
        xc = z * _sigmoid(z)
        xc_ref[...] = xc.astype(BF16)
        for hd in range(M_HEADS):
            qk = _dot(xc[:, hd * M_DV:(hd + 1) * M_DV].astype(BF16), wqk_ref[hd])
            qm_ref[:, hd * M_DK:(hd + 1) * M_DK] = qk[:, :M_DK].astype(BF16)
            km_ref[:, hd * M_DK:(hd + 1) * M_DK] = (qk[:, M_DK:] * (M_DK ** -0.5)).astype(BF16)
        og_ref[...] = og.astype(BF16)
        grow_ref[...] = jnp.transpose(gates)[:M_HEADS * GATE_PAD, :]
        gab_ref[...] = gab.astype(BF16)
        uq = qkv[:, :NA_INNER]
        uk = qkv[:, NA_INNER:2 * NA_INNER]
        def head_mean(sq):
            return jnp.concatenate([_dot(sq[:, j:j + NORM_GROUP].astype(BF16), bd_ref[...])
                                    for j in range(0, NA_INNER, NORM_GROUP)], axis=1)

        msq = head_mean(uq * uq)
        msk = head_mean(uk * uk)
        q_ref[...] = (uq * lax.rsqrt(msq + EPS) * qg_ref[...]).astype(BF16)
        k_ref[...] = (uk * lax.rsqrt(msk + EPS) * kg_ref[...]).astype(BF16)
        v_ref[...] = qkv[:, 2 * NA_INNER:].astype(BF16)

    pl.when(i < N_REAL_TILES)(lambda: body(x_ref, False))
    pl.when(i == N_REAL_TILES)(lambda: body(tail_ref, True))


def _real_or_tail(body, *ref_pairs):
    i = pl.program_id(0)
    pl.when(i < N_REAL_TILES)(lambda: body(*ref_pairs[0::2]))
    pl.when(i == N_REAL_TILES)(lambda: body(*ref_pairs[1::2]))


def _real_spec(width):
    return pl.BlockSpec((TM, width), lambda i: (jnp.minimum(i, N_REAL_TILES - 1), 0))


def _tail_spec(width):
    return pl.BlockSpec((TM, width), lambda i: (0, 0))


def _row_spec(width):
    return pl.BlockSpec((TM, width), lambda i: (i, 0))


def _inproj(x2d, tailh, xedge, g1, w_all, wg, gb, qg, kg, bd, cw, cb, wqk):
    halo_blocks = N_REAL // HALO
    per_tile = TM // HALO
    consts = (g1, w_all, wg, gb, qg, kg, bd, cw, cb, wqk)
    return pl.pallas_call(
        _inproj_kernel,
        grid=(N_TILES,),
        in_specs=[_real_spec(D_MODEL), _tail_spec(D_MODEL),
                  pl.BlockSpec((HALO, D_MODEL), lambda i: (jnp.clip(i * per_tile - 1, 0, halo_blocks - 1), 0)),
                  pl.BlockSpec((HALO, D_MODEL), lambda i: (jnp.clip((i + 1) * per_tile, 0, halo_blocks - 1), 0)),
                  _const_spec(xedge.shape)]
                 + [_const_spec(a.shape) for a in consts],
        out_specs=[_row_spec(M_INNER), _row_spec(M_INNER),
                   pl.BlockSpec((M_HEADS * GATE_PAD, TM), lambda i: (0, i)),
                   _row_spec(NA_INNER), _row_spec(NA_INNER), _row_spec(NA_INNER), _row_spec(2 * D_MODEL),
                   _row_spec(M_INNER), _row_spec(M_HEADS * M_DK), _row_spec(M_HEADS * M_DK)],
        out_shape=[jax.ShapeDtypeStruct((NT, M_INNER), BF16),
                   jax.ShapeDtypeStruct((NT, M_INNER), BF16),
                   jax.ShapeDtypeStruct((M_HEADS * GATE_PAD, NT), F32),
                   jax.ShapeDtypeStruct((NT, NA_INNER), BF16),
                   jax.ShapeDtypeStruct((NT, NA_INNER), BF16),
                   jax.ShapeDtypeStruct((NT, NA_INNER), BF16),
                   jax.ShapeDtypeStruct((NT, 2 * D_MODEL), BF16),
                   jax.ShapeDtypeStruct((NT, M_INNER), BF16),
                   jax.ShapeDtypeStruct((NT, M_HEADS * M_DK), BF16),
                   jax.ShapeDtypeStruct((NT, M_HEADS * M_DK), BF16)],
        scratch_shapes=[pltpu.VMEM((D_MODEL, OFF_G - OFF_Q), BF16),
                        pltpu.VMEM((D_MODEL, D_IN_PROJ - OFF_G), BF16)],
        compiler_params=_params("arbitrary"),
        name="inproj",
    )(x2d, tailh, x2d, x2d, xedge, *consts)


MT = 256
N_MCHUNK = SEQ // MT
STATE_ROWS = M_DV + BF16_ROWS
ROWS_PER_DIR = 8
LOCAL_GROUP = 16
STATE_GROUP = 8


def _split3(x):
    hi = x.astype(BF16)
    r1 = x - hi.astype(F32)
    mid = r1.astype(BF16)
    lo = (r1 - mid.astype(F32)).astype(BF16)
    return hi, mid, lo


def _mlstm_local_start(q, k, v, g8, is_tail):
    t = q.shape[0]
    li = [g8[0:1], g8[2:3]]
    lf = [_log_sigmoid(g8[1:2]), _log_sigmoid(g8[3:4])]
    if is_tail:
        pad = lax.broadcasted_iota(jnp.int32, (1, t), 1) < NPAD
        li = [jnp.where(pad, NEG_LOG_GATE, x) for x in li]
        lf = [jnp.where(pad, 0.0, x) for x in lf]
    si = lax.broadcasted_iota(jnp.int32, (t, t), 0)
    ti = lax.broadcasted_iota(jnp.int32, (t, t), 1)
    hi, mid, lo = _split3(jnp.concatenate(lf, axis=0))
    lhs = jnp.concatenate([hi, mid, lo, jnp.zeros((BF16_ROWS - 6, t), BF16)], axis=0)
    pref = _dot(lhs, (si <= ti).astype(BF16))
    pre_f = pref[0:1] + pref[2:3] + pref[4:5]
    pre_b = pref[1:2] + pref[3:4] + pref[5:6]
    b_end = [pre_f[:, t - 1:t], pre_b[:, t - 1:t]]
    b = [pre_f, b_end[1] - pre_b + lf[1]]
    s_t = _dot_nt(k, q)
    v_tb = jnp.transpose(v.astype(BF16))
    return li, b, b_end, s_t, v_tb, k


def _mlstm_local_finish(li, b, b_end, s_t, v_tb, k):
    t = s_t.shape[0]
    si = lax.broadcasted_iota(jnp.int32, (t, t), 0)
    ti = lax.broadcasted_iota(jnp.int32, (t, t), 1)
    kf = k.astype(F32)
    v_ext = jnp.concatenate([v_tb, jnp.ones((STATE_ROWS - M_DV, t), BF16)], axis=0)
    out = []
    for dirn in range(2):
        g2 = (li[dirn] - b[dirn]) * LOG2E
        g_col = jnp.transpose(jnp.broadcast_to(g2, (LANES, t)))
        g_st = jnp.where((si <= ti) if dirn == 0 else (si >= ti),
                         jnp.concatenate([g_col] * (t // LANES), axis=1), MASKED)
        g_max = jnp.max(g_st, axis=0, keepdims=True)
        p_t = s_t * jnp.exp2(g_st - g_max)
        den = jnp.sum(p_t, axis=0, keepdims=True)
        nl_t = _dot(v_tb, p_t.astype(BF16))
        m_loc = b[dirn] + g_max * (1.0 / LOG2E)
        top = jnp.max(g2, axis=1, keepdims=True)
        kw = (kf * jnp.exp2(g_col - top)).astype(BF16)
        u = _dot(v_ext, kw)
        a_max = b_end[dirn] + top * (1.0 / LOG2E)
        rows = jnp.concatenate([m_loc, den, b[dirn], jnp.broadcast_to(a_max, (1, t)),
                                jnp.broadcast_to(b_end[dirn], (1, t)),
                                jnp.zeros((ROWS_PER_DIR - 5, t), F32)], axis=0)
        out.append((nl_t, u, rows))
    return out


def _mlstm_state_start(dirn, q, u, rows, s_ref, m_ref):
    a_max, b_end = rows[3:4, 0:1], rows[4:5, 0:1]
    m_prev = m_ref[dirn]
    s_old = s_ref[dirn]
    inter = _dot_nt(s_old.astype(BF16), q)
    m_new = jnp.maximum(b_end + m_prev, a_max)
    s_ref[dirn] = jnp.exp(b_end + m_prev - m_new) * s_old + jnp.exp(a_max - m_new) * u
    m_ref[dirn] = m_new
    return inter, m_prev


def _mlstm_state_finish(inter, m_prev, nl_t, rows):
    m_loc, den_loc, b = rows[0:1], rows[1:2], rows[2:3]
    m_inter = b + m_prev
    m_t = jnp.maximum(m_inter, m_loc)
    w_inter = jnp.exp(m_inter - m_t)
    w_loc = jnp.exp(m_loc - m_t)
    den = w_inter * inter[M_DV:M_DV + 1] + w_loc * den_loc
    scale = 1.0 / jnp.maximum(jnp.abs(den), jnp.exp(-m_t))
    return (w_inter * scale) * inter[:M_DV] + (w_loc * scale) * nl_t


def _mlstm_kernel(q_ref, qt_ref, k_ref, kt_ref, v_ref, vt_ref, g_ref, gt_ref, o_ref, ot_ref,
                  nl_s, u_s, rows_s, ht_s, s_ref, m_ref):
    s_ref[...] = jnp.zeros_like(s_ref)
    m_ref[...] = jnp.zeros_like(m_ref)
    tail_cols = pl.ds(SEQ, TAIL)

    def chunk(c):
        r0 = pl.multiple_of(c * MT, MT)
        return pl.ds(r0, MT)

    def keep_local(c, cols, per_dir):
        for dirn, (nl_t, u, rows) in enumerate(per_dir):
            nl_s[dirn, :, cols] = nl_t
            u_s[dirn, c] = u
            rows_s[dirn, :, cols] = rows

    def local_pass(i, carry):
        group = [i * LOCAL_GROUP + j for j in range(LOCAL_GROUP)]
        started = [_mlstm_local_start(q_ref[chunk(c), :], k_ref[chunk(c), :], v_ref[chunk(c), :],
                                      g_ref[:, chunk(c)], False) for c in group]
        results = [_mlstm_local_finish(*st) for st in started]
        for c, per_dir in zip(group, results):
            keep_local(c, chunk(c), per_dir)
        return carry

    def state_steps(items):
        started = [_mlstm_state_start(dirn, q, u_s[dirn, c], rows_s[dirn, :, cols], s_ref, m_ref)
                   for dirn, q, c, cols in items]
        return [_mlstm_state_finish(inter, m_prev, nl_s[dirn, :, cols], rows_s[dirn, :, cols])
                for (inter, m_prev), (dirn, _, c, cols) in zip(started, items)]

    def visit_items(i):
        chunks = []
        for j in range(STATE_GROUP):
            step = i * STATE_GROUP + j
            chunks += [(0, step), (1, N_MCHUNK - 1 - step)]
        return [(dirn, q_ref[chunk(c), :], c, chunk(c)) for dirn, c in chunks]

    def first_visits(i, carry):
        items = visit_items(i)
        for (_, _, _, cols), h_t in zip(items, state_steps(items)):
            ht_s[:, cols] = h_t
        return carry

    def second_visits(i, carry):
        items = visit_items(i)
        for (_, _, _, cols), h_t in zip(items, state_steps(items)):
            o_ref[cols, :] = jnp.transpose((ht_s[:, cols] + h_t).astype(o_ref.dtype))
        return carry

    keep_local(N_MCHUNK, tail_cols, _mlstm_local_finish(
        *_mlstm_local_start(qt_ref[...], kt_ref[...], vt_ref[...], gt_ref[...], True)))
    lax.fori_loop(0, N_MCHUNK // LOCAL_GROUP, local_pass, 0)
    ht_s[:, tail_cols] = state_steps([(0, qt_ref[...], N_MCHUNK, tail_cols)])[0]
    half = N_MCHUNK // 2 // STATE_GROUP
    lax.fori_loop(0, half, first_visits, 0)
    lax.fori_loop(half, 2 * half, second_visits, 0)
    h_tail = ht_s[:, tail_cols] + state_steps([(1, qt_ref[...], N_MCHUNK, tail_cols)])[0]
    ot_ref[...] = jnp.transpose(h_tail.astype(ot_ref.dtype))


def _mlstm(q, k, xm, grow):
    tail_blk = N_REAL // TAIL
    real = lambda w: pl.BlockSpec((SEQ, w), lambda b, h: (b, h))
    tail = lambda w: pl.BlockSpec((TAIL, w), lambda b, h: (tail_blk + b, h))
    lp = SEQ + TAIL
    return pl.pallas_call(
        _mlstm_kernel,
        grid=(BATCH, M_HEADS),
        in_specs=[real(M_DK), tail(M_DK), real(M_DK), tail(M_DK), real(M_DV), tail(M_DV),
                  pl.BlockSpec((GATE_PAD, SEQ), lambda b, h: (h, b)),
                  pl.BlockSpec((GATE_PAD, TAIL), lambda b, h: (h, tail_blk + b))],
        out_specs=[pl.BlockSpec((SEQ, M_DV), lambda b, h: (b, h)),
                   pl.BlockSpec((TAIL, M_DV), lambda b, h: (b, h))],
        out_shape=[jax.ShapeDtypeStruct((N_REAL, M_INNER), BF16),
                   jax.ShapeDtypeStruct((BATCH * TAIL, M_INNER), BF16)],
        scratch_shapes=[pltpu.VMEM((2, M_DV, lp), F32),
                        pltpu.VMEM((2, N_MCHUNK + 1, STATE_ROWS, M_DK), F32),
                        pltpu.VMEM((2, ROWS_PER_DIR, lp), F32),
                        pltpu.VMEM((M_DV, lp), F32),
                        pltpu.VMEM((2, STATE_ROWS, M_DK), F32),
                        pltpu.VMEM((2, 1, 1), F32)],
        compiler_params=_params("parallel", "parallel"),
        name="mlstm",
    )(q, q, k, k, xm, xm, grow, grow)


PAIR = 2 * NA_DH


def _natten_kernel(q_ref, qt_ref, k_ref, kt_ref, v_ref, vt_ref, bias_ref, o_ref, ot_ref):
    lane = lax.broadcasted_iota(jnp.int32, (1, PAIR), 1)
    first = lane < NA_DH
    k_tail = kt_ref[...]
    v_tail = vt_ref[...]

    def attend(q, keys, vals, bias_t):
        n = q.shape[0]
        zero = jnp.zeros_like(q)
        qs = jnp.concatenate([jnp.where(first, q, zero), jnp.where(first, zero, q)], axis=0)
        s = _dot_nt(keys, qs) + bias_t
        e = jnp.exp2(s - jnp.max(s, axis=0, keepdims=True))
        o = _dot(jnp.transpose(e.astype(BF16)), vals)
        inv = jnp.broadcast_to(1.0 / jnp.sum(e, axis=0, keepdims=True), (PAIR, 2 * n))
        o = o * jnp.transpose(inv)
        return jnp.where(first, o[:n], o[n:])

    def row_block(r, carry):
        r0 = jnp.clip(r - NA_WIN_H // 2, 0, ROWS - NA_WIN_H)
        qrows = pl.ds(pl.multiple_of(r * GRID_W, GRID_W), GRID_W)
        krows = pl.ds(pl.multiple_of(r0 * GRID_W, GRID_W), NA_WIN_H * GRID_W)
        keys = jnp.concatenate([k_ref[krows, :], k_tail], axis=0)
        vals = jnp.concatenate([v_ref[krows, :], v_tail], axis=0)
        o = attend(q_ref[qrows, :], keys, vals, bias_ref[r - r0])
        o_ref[qrows, :] = o.astype(o_ref.dtype)
        return carry

    lax.fori_loop(0, ROWS, row_block, 0, unroll=64)
    tb = bias_ref[0, NA_WIN_H * GRID_W:, :]
    tail_bias = jnp.concatenate([jnp.broadcast_to(tb[:, 0:1], (TAIL, TAIL)),
                                 jnp.broadcast_to(tb[:, NA_DH:NA_DH + 1], (TAIL, TAIL))], axis=1)
    ot_ref[...] = attend(qt_ref[...], k_tail, v_tail, tail_bias).astype(ot_ref.dtype)


def _natten(qn, kn, vn, bias):
    tail_blk = N_REAL // TAIL
    real = pl.BlockSpec((SEQ, PAIR), lambda b, p: (b, p))
    tail = pl.BlockSpec((TAIL, PAIR), lambda b, p: (tail_blk + b, p))
    return pl.pallas_call(
        _natten_kernel,
        grid=(BATCH, NA_HEADS // 2),
        in_specs=[real, tail, real, tail, real, tail,
                  pl.BlockSpec((None, NA_WIN_H, NA_WIN_H * GRID_W + TAIL, PAIR), lambda b, p: (p, 0, 0, 0))],
        out_specs=[pl.BlockSpec((SEQ, PAIR), lambda b, p: (b, p)),
                   pl.BlockSpec((TAIL, PAIR), lambda b, p: (b, p))],
        out_shape=[jax.ShapeDtypeStruct((N_REAL, NA_INNER), BF16),
                   jax.ShapeDtypeStruct((BATCH * TAIL, NA_INNER), BF16)],
        compiler_params=_params("parallel", "parallel"),
        name="natten",
    )(qn, qn, kn, kn, vn, vn, bias)


def _natten_bias(rpb, meta_bias):
    qc = jnp.arange(GRID_W)
    kc = jnp.arange(GRID_W)
    win0 = jnp.clip(qc - NA_WIN_W // 2, 0, GRID_W - NA_WIN_W)
    ok = (kc[:, None] >= win0[None, :]) & (kc[:, None] < win0[None, :] + NA_WIN_W)
    dc = jnp.clip(kc[:, None] - qc[None, :], -(NA_WIN_W - 1), NA_WIN_W - 1) + NA_WIN_W - 1
    onehot = (dc[None] == jnp.arange(2 * NA_WIN_W - 1)[:, None, None]).astype(F32)
    t1 = jnp.einsum('hdj,jkq->dkhq', rpb.astype(F32), onehot, precision=lax.Precision.HIGHEST)
    t1 = jnp.where(ok[None, :, None, :], t1, MASKED) * LOG2E
    t1 = jnp.transpose(t1.reshape(2 * NA_WIN_H - 1, GRID_W, NA_HEADS // 2, PAIR), (2, 0, 1, 3))
    met = jnp.concatenate([jnp.full((NA_HEADS, NPAD), MASKED, F32), meta_bias.astype(F32)], axis=1) * LOG2E
    met = jnp.broadcast_to(met.reshape(NA_HEADS // 2, 2, TAIL, 1), (NA_HEADS // 2, 2, TAIL, NA_DH))
    met = jnp.transpose(met, (0, 2, 1, 3)).reshape(NA_HEADS // 2, TAIL, PAIR)
    rows = [jnp.concatenate([t1[:, NA_WIN_H - 1 - dl:2 * NA_WIN_H - 1 - dl].reshape(-1, NA_WIN_H * GRID_W, PAIR),
                             met], axis=1) for dl in range(NA_WIN_H)]
    return jnp.stack(rows, axis=1)


def _merge_kernel(hsr_ref, hst_ref, ybr_ref, ybt_ref, xc_ref, og_ref, gab_ref,
                  ng_ref, sk_ref, wa_ref, wb_ref, wo_ref, o_ref):
    def body(hs_ref, yb_ref):
        hs = hs_ref[...].astype(F32)
        parts = []
        for hd in range(M_HEADS):
            sl = hs[:, hd * M_DV:(hd + 1) * M_DV]
            parts.append(sl * lax.rsqrt(jnp.mean(sl * sl, axis=-1, keepdims=True) + EPS))
        hn = jnp.concatenate(parts, axis=1) * ng_ref[...]
        y_a = _sigmoid(og_ref[...].astype(F32)) * (hn + sk_ref[...] * xc_ref[...].astype(F32))
        gab = gab_ref[...].astype(F32)
        mix = (_sigmoid(gab[:, :D_MODEL]) * _dot(y_a.astype(BF16), wa_ref[...])
               + _sigmoid(gab[:, D_MODEL:]) * _dot(yb_ref[...], wb_ref[...]))
        o_ref[...] = _dot(mix.astype(BF16), wo_ref[...])

    _real_or_tail(body, hsr_ref, hst_ref, ybr_ref, ybt_ref)


def _merge(hs_real, hs_tail, yb_real, yb_tail, xc, og, gab, ng, sk, wa, wb, wo):
    return pl.pallas_call(
        _merge_kernel,
        grid=(N_TILES,),
        in_specs=[_real_spec(M_INNER), _tail_spec(M_INNER), _real_spec(NA_INNER), _tail_spec(NA_INNER),
                  _row_spec(M_INNER), _row_spec(M_INNER), _row_spec(2 * D_MODEL)]
                 + [_const_spec(a.shape) for a in (ng, sk, wa, wb, wo)],
        out_specs=_row_spec(D_MODEL),
        out_shape=jax.ShapeDtypeStruct((NT, D_MODEL), F32),
        compiler_params=_params("parallel"),
        name="merge",
    )(hs_real, hs_tail, yb_real, yb_tail, xc, og, gab, ng, sk, wa, wb, wo)


def _ffn_kernel(d_ref, xr_ref, xt_ref, g2_ref, w1_ref, w2_ref, or_ref, ot_ref):
    def body(x_ref, o_ref):
        h = x_ref[...] + d_ref[...]
        xn = (h * lax.rsqrt(jnp.mean(h * h, axis=-1, keepdims=True) + EPS) * g2_ref[...]).astype(BF16)
        z = jnp.maximum(_dot(xn, w1_ref[...]), 0.0)
        o_ref[...] = h + _dot((z * z).astype(BF16), w2_ref[...])

    _real_or_tail(body, xr_ref, xt_ref, or_ref, ot_ref)


def _ffn(delta, x2d, tailh, g2, w1, w2):
    return pl.pallas_call(
        _ffn_kernel,
        grid=(N_TILES,),
        in_specs=[_row_spec(D_MODEL), _real_spec(D_MODEL), _tail_spec(D_MODEL),
                  _const_spec(g2.shape), _const_spec(w1.shape), _const_spec(w2.shape)],
        out_specs=[_real_spec(D_MODEL), _tail_spec(D_MODEL)],
        out_shape=[jax.ShapeDtypeStruct((N_REAL, D_MODEL), F32),
                   jax.ShapeDtypeStruct((BATCH * TAIL, D_MODEL), F32)],
        compiler_params=_params("arbitrary"),
        name="ffn",
    )(delta, x2d, tailh, g2, w1, w2)


def kernel(x, meta_tokens, norm1_g, w_in, mlstm_conv_w, mlstm_conv_b, mlstm_wq, mlstm_wk, mlstm_gate_b, mlstm_norm_g, mlstm_skip, na_q_norm_g, na_k_norm_g, na_rpb, na_meta_bias, w_branch_a, w_branch_b, w_out, norm2_g, w_ff1, w_ff2):
    x2d = x.astype(F32).reshape(N_REAL, D_MODEL)
    tail = jnp.concatenate([jnp.zeros((NPAD, D_MODEL), F32), meta_tokens.astype(F32)], axis=0)
    tailh = jnp.tile(tail, (BATCH, 1))

    w_all = w_in.astype(BF16)
    pad_gates = lambda a: jnp.pad(
        jnp.swapaxes(a.reshape(a.shape[0], 4, M_HEADS), 1, 2),
        ((0, 0), (0, 0), (0, GATE_PAD - 4))).reshape(a.shape[0], M_HEADS * GATE_PAD)
    wg = jnp.pad(pad_gates(w_all[:, OFF_MG:OFF_Q]), ((0, 0), (0, LANES - M_HEADS * GATE_PAD)))
    gb = jnp.pad(pad_gates(mlstm_gate_b.astype(F32).reshape(1, 4 * M_HEADS)),
                 ((0, 0), (0, LANES - M_HEADS * GATE_PAD)))
    bd = jnp.kron(jnp.eye(NORM_GROUP // NA_DH, dtype=F32),
                  jnp.full((NA_DH, NA_DH), 1.0 / NA_DH, F32)).astype(BF16)
    qg = jnp.tile(na_q_norm_g.astype(F32), NA_HEADS)[None, :] * (NA_DH ** -0.5 * LOG2E)
    kg = jnp.tile(na_k_norm_g.astype(F32), NA_HEADS)[None, :]

    xb = x.astype(F32)
    xedge = jnp.concatenate([xb[:, SEQ - 1, :], xb[:, 0, :]], axis=0)
    wqk = jnp.concatenate([mlstm_wq, mlstm_wk], axis=-1).astype(BF16)
    vm, og, grow, qn, kn, vn, gab, xc, qm, km = _inproj(
        x2d, tailh, xedge, norm1_g.astype(F32)[None, :], w_all, wg, gb, qg, kg, bd,
        mlstm_conv_w.astype(F32).reshape(3, M_INNER), mlstm_conv_b.astype(F32)[None, :], wqk)

    hs_real, hs_tail = _mlstm(qm, km, vm, grow)
    yb_real, yb_tail = _natten(qn, kn, vn, _natten_bias(na_rpb, na_meta_bias))

    delta = _merge(hs_real, hs_tail, yb_real, yb_tail, xc, og, gab,
                   mlstm_norm_g.astype(F32).reshape(1, M_INNER), mlstm_skip.astype(F32)[None, :],
                   w_branch_a.astype(BF16), w_branch_b.astype(BF16), w_out.astype(BF16))
    out_real, _ = _ffn(delta, x2d, tailh, norm2_g.astype(F32)[None, :],
                       w_ff1.astype(BF16), w_ff2.astype(BF16))
    return out_real.reshape(BATCH, SEQ, D_MODEL)
```

```python
import jax
import jax.numpy as jnp
from jax import lax
from jax.experimental import pallas as pl
from jax.experimental.pallas import tpu as pltpu

D_MODEL = 1024
BATCH = 4
SEQ = 4096
N_META = 16
GRID_W = 64
ROWS = SEQ // GRID_W
M_HEADS = 4
M_DV = 256
M_DK = 128
M_INNER = M_HEADS * M_DV
NA_HEADS = 8
NA_DH = 64
NA_INNER = NA_HEADS * NA_DH
NA_WIN_H = 8
NA_WIN_W = 16
D_FF = 4 * D_MODEL
EPS = 1e-6
NEG_LOG_GATE = -1e9
MASKED = -1e30
LOG2E = 1.4426950408889634

TAIL = 128
NPAD = TAIL - N_META
N_REAL = BATCH * SEQ
NT = N_REAL + BATCH * TAIL
TM = 512
N_TILES = NT // TM
N_REAL_TILES = N_REAL // TM

LANES = 128
BF16_ROWS = 16
VMEM_LIMIT = 56 * 1024 * 1024

F32 = jnp.float32
BF16 = jnp.bfloat16


def _dot(a, b):
    return jnp.dot(a, b, preferred_element_type=F32)


def _dot_nt(a, b):
    return lax.dot_general(a, b, (((1,), (1,)), ((), ())), preferred_element_type=F32)


def _sigmoid(z):
    return 0.5 * jnp.tanh(0.5 * z) + 0.5


def _log_sigmoid(z):
    return jnp.minimum(z, 0.0) - jnp.log1p(jnp.exp(-jnp.abs(z)))


def _const_spec(shape):
    nd = len(shape)
    return pl.BlockSpec(shape, lambda *_: (0,) * nd, pipeline_mode=pl.Buffered(1))


def _params(*sem):
    return pltpu.CompilerParams(dimension_semantics=sem, vmem_limit_bytes=VMEM_LIMIT)


GATE_PAD = 8
OFF_MX = 0
OFF_MO = OFF_MX + M_INNER
OFF_MG = OFF_MO + M_INNER
OFF_Q = OFF_MG + 4 * M_HEADS
OFF_G = OFF_Q + 3 * NA_INNER
D_IN_PROJ = OFF_G + 2 * D_MODEL


HALO = 8
HALO_PAD = BF16_ROWS
TILES_PER_SEQ = SEQ // TM
NORM_GROUP = 256


def _inproj_kernel(x_ref, tail_ref, xprev_ref, xnext_ref, xedge_ref, g1_ref, w_ref, wg_ref,
                   gb_ref, qg_ref, kg_ref, bd_ref, cw_ref, cb_ref, wqk_ref,
                   vm_ref, og_ref, grow_ref, q_ref, k_ref, v_ref, gab_ref, xc_ref, qm_ref, km_ref,
                   wqkv_s, wgab_s):
    i = pl.program_id(0)

    @pl.when(i == 0)
    def _():
        wqkv_s[...] = w_ref[:, OFF_Q:OFF_G]
        wgab_s[...] = w_ref[:, OFF_G:D_IN_PROJ]

    def norm1(h):
        ms = jnp.mean(h * h, axis=-1, keepdims=True)
        return (h * lax.rsqrt(ms + EPS) * g1_ref[...]).astype(BF16)

    def body(h_ref, is_tail):
        xn = norm1(h_ref[...])
        if is_tail:
            halo = jnp.concatenate([xedge_ref[...], jnp.zeros((HALO_PAD - 2 * BATCH, D_MODEL), F32)], axis=0)
        else:
            first = i % TILES_PER_SEQ == 0
            last = i % TILES_PER_SEQ == TILES_PER_SEQ - 1
            prev = jnp.where(first, tail_ref[TAIL - 1:TAIL, :], xprev_ref[HALO - 1:HALO, :])
            nxt = jnp.where(last, tail_ref[0:1, :], xnext_ref[0:1, :])
            halo = jnp.concatenate([prev, nxt, jnp.zeros((HALO_PAD - 2, D_MODEL), F32)], axis=0)
        xm_ext = _dot(jnp.concatenate([xn, norm1(halo)], axis=0), w_ref[:, OFF_MX:OFF_MO])
        og = _dot(xn, w_ref[:, OFF_MO:OFF_MG])
        gates = _dot(xn, wg_ref[...]) + gb_ref[...]
        gab = _dot(xn, wgab_s[...])
        qkv = _dot(xn, wqkv_s[...])
        xm = xm_ext[:TM]
        rows = lax.broadcasted_iota(jnp.int32, (TM, M_INNER), 0)
        x_m1 = pltpu.roll(xm, 1, axis=0)
        x_p1 = pltpu.roll(xm, TM - 1, axis=0)
        if is_tail:
            for b in range(BATCH):
                x_m1 = jnp.where(rows == b * TAIL, xm_ext[TM + b:TM + b + 1], x_m1)
                x_p1 = jnp.where(rows == b * TAIL + TAIL - 1, xm_ext[TM + BATCH + b:TM + BATCH + b + 1], x_p1)
        else:
            x_m1 = jnp.where(rows == 0, xm_ext[TM:TM + 1], x_m1)
            x_p1 = jnp.where(rows == TM - 1, xm_ext[TM + 1:TM + 2], x_p1)
        vm_ref[...] = xm.astype(BF16)
        z = cw_ref[0:1, :] * x_m1 + cw_ref[1:2, :] * xm + cw_ref[2:3, :] * x_p1 + cb_ref[...]
        xc = z * _sigmoid(z)
        xc_ref[...] = xc.astype(BF16)
        for hd in range(M_HEADS):
            qk = _dot(xc[:, hd * M_DV:(hd + 1) * M_DV].astype(BF16), wqk_ref[hd])
            qm_ref[:, hd * M_DK:(hd + 1) * M_DK] = qk[:, :M_DK].astype(BF16)
            km_ref[:, hd * M_DK:(hd + 1) * M_DK] = (qk[:, M_DK:] * (M_DK ** -0.5)).astype(BF16)
        og_ref[...] = og.astype(BF16)
        grow_ref[...] = jnp.transpose(gates)[:M_HEADS * GATE_PAD, :]
        gab_ref[...] = gab.astype(BF16)
        uq = qkv[:, :NA_INNER]
        uk = qkv[:, NA_INNER:2 * NA_INNER]
        def head_mean(sq):
            return jnp.concatenate([_dot(sq[:, j:j + NORM_GROUP].astype(BF16), bd_ref[...])
                                    for j in range(0, NA_INNER, NORM_GROUP)], axis=1)

        msq = head_mean(uq * uq)
        msk = head_mean(uk * uk)
        q_ref[...] = (uq * lax.rsqrt(msq + EPS) * qg_ref[...]).astype(BF16)
        k_ref[...] = (uk * lax.rsqrt(msk + EPS) * kg_ref[...]).astype(BF16)
        v_ref[...] = qkv[:, 2 * NA_INNER:].astype(BF16)

    pl.when(i < N_REAL_TILES)(lambda: body(x_ref, False))
    pl.when(i == N_REAL_TILES)(lambda: body(tail_ref, True))


def _real_or_tail(body, *ref_pairs):
    i = pl.program_id(0)
    pl.when(i < N_REAL_TILES)(lambda: body(*ref_pairs[0::2]))
    pl.when(i == N_REAL_TILES)(lambda: body(*ref_pairs[1::2]))


def _real_spec(width):
    return pl.BlockSpec((TM, width), lambda i: (jnp.minimum(i, N_REAL_TILES - 1), 0))


def _tail_spec(width):
    return pl.BlockSpec((TM, width), lambda i: (0, 0))


def _row_spec(width):
    return pl.BlockSpec((TM, width), lambda i: (i, 0))


def _slab_spec(shape):
    return pl.BlockSpec((shape[0] // N_REAL_TILES, shape[1]), lambda i: (jnp.minimum(i, N_REAL_TILES - 1), 0))


def _inproj(x2d, tailh, xedge, g1, w_all, wg, gb, qg, kg, bd, cw, cb, wqk):
    halo_blocks = N_REAL // HALO
    per_tile = TM // HALO
    consts = (g1, w_all, wg, gb, qg, kg, bd, cw, cb, wqk)
    return pl.pallas_call(
        _inproj_kernel,
        grid=(N_TILES,),
        in_specs=[_real_spec(D_MODEL), _tail_spec(D_MODEL),
                  pl.BlockSpec((HALO, D_MODEL), lambda i: (jnp.clip(i * per_tile - 1, 0, halo_blocks - 1), 0)),
                  pl.BlockSpec((HALO, D_MODEL), lambda i: (jnp.clip((i + 1) * per_tile, 0, halo_blocks - 1), 0)),
                  _const_spec(xedge.shape)]
                 + [_const_spec(a.shape) for a in consts],
        out_specs=[_row_spec(M_INNER), _row_spec(M_INNER),
                   pl.BlockSpec((M_HEADS * GATE_PAD, TM), lambda i: (0, i)),
                   _row_spec(NA_INNER), _row_spec(NA_INNER), _row_spec(NA_INNER), _row_spec(2 * D_MODEL),
                   _row_spec(M_INNER), _row_spec(M_HEADS * M_DK), _row_spec(M_HEADS * M_DK)],
        out_shape=[jax.ShapeDtypeStruct((NT, M_INNER), BF16),
                   jax.ShapeDtypeStruct((NT, M_INNER), BF16),
                   jax.ShapeDtypeStruct((M_HEADS * GATE_PAD, NT), F32),
                   jax.ShapeDtypeStruct((NT, NA_INNER), BF16),
                   jax.ShapeDtypeStruct((NT, NA_INNER), BF16),
                   jax.ShapeDtypeStruct((NT, NA_INNER), BF16),
                   jax.ShapeDtypeStruct((NT, 2 * D_MODEL), BF16),
                   jax.ShapeDtypeStruct((NT, M_INNER), BF16),
                   jax.ShapeDtypeStruct((NT, M_HEADS * M_DK), BF16),
                   jax.ShapeDtypeStruct((NT, M_HEADS * M_DK), BF16)],
        scratch_shapes=[pltpu.VMEM((D_MODEL, OFF_G - OFF_Q), BF16),
                        pltpu.VMEM((D_MODEL, D_IN_PROJ - OFF_G), BF16)],
        compiler_params=_params("arbitrary"),
        name="inproj",
    )(x2d, tailh, x2d, x2d, xedge, *consts)


MT = 256
N_MCHUNK = SEQ // MT
STATE_ROWS = M_DV + BF16_ROWS
ROWS_PER_DIR = 8
LOCAL_GROUP = 16
STATE_GROUP = 8


def _split3(x):
    hi = x.astype(BF16)
    r1 = x - hi.astype(F32)
    mid = r1.astype(BF16)
    lo = (r1 - mid.astype(F32)).astype(BF16)
    return hi, mid, lo


def _mlstm_local_start(q, k, v, g8, is_tail):
    t = q.shape[0]
    li = [g8[0:1], g8[2:3]]
    lf = [_log_sigmoid(g8[1:2]), _log_sigmoid(g8[3:4])]
    if is_tail:
        pad = lax.broadcasted_iota(jnp.int32, (1, t), 1) < NPAD
        li = [jnp.where(pad, NEG_LOG_GATE, x) for x in li]
        lf = [jnp.where(pad, 0.0, x) for x in lf]
    si = lax.broadcasted_iota(jnp.int32, (t, t), 0)
    ti = lax.broadcasted_iota(jnp.int32, (t, t), 1)
    hi, mid, lo = _split3(jnp.concatenate(lf, axis=0))
    lhs = jnp.concatenate([hi, mid, lo, jnp.zeros((BF16_ROWS - 6, t), BF16)], axis=0)
    pref = _dot(lhs, (si <= ti).astype(BF16))
    pre_f = pref[0:1] + pref[2:3] + pref[4:5]
    pre_b = pref[1:2] + pref[3:4] + pref[5:6]
    b_end = [pre_f[:, t - 1:t], pre_b[:, t - 1:t]]
    b = [pre_f, b_end[1] - pre_b + lf[1]]
    s_t = _dot_nt(k, q)
    v_tb = jnp.transpose(v.astype(BF16))
    return li, b, b_end, s_t, v_tb, k


def _mlstm_local_finish(li, b, b_end, s_t, v_tb, k):
    t = s_t.shape[0]
    si = lax.broadcasted_iota(jnp.int32, (t, t), 0)
    ti = lax.broadcasted_iota(jnp.int32, (t, t), 1)
    kf = k.astype(F32)
    v_ext = jnp.concatenate([v_tb, jnp.ones((STATE_ROWS - M_DV, t), BF16)], axis=0)
    out = []
    for dirn in range(2):
        g2 = (li[dirn] - b[dirn]) * LOG2E
        g_col = jnp.transpose(jnp.broadcast_to(g2, (LANES, t)))
        g_st = jnp.where((si <= ti) if dirn == 0 else (si >= ti),
                         jnp.concatenate([g_col] * (t // LANES), axis=1), MASKED)
        g_max = jnp.max(g_st, axis=0, keepdims=True)
        p_t = s_t * jnp.exp2(g_st - g_max)
        den = jnp.sum(p_t, axis=0, keepdims=True)
        nl_t = _dot(v_tb, p_t.astype(BF16))
        m_loc = b[dirn] + g_max * (1.0 / LOG2E)
        top = jnp.max(g2, axis=1, keepdims=True)
        kw = (kf * jnp.exp2(g_col - top)).astype(BF16)
        u = _dot(v_ext, kw)
        a_max = b_end[dirn] + top * (1.0 / LOG2E)
        rows = jnp.concatenate([m_loc, den, b[dirn], jnp.broadcast_to(a_max, (1, t)),
                                jnp.broadcast_to(b_end[dirn], (1, t)),
                                jnp.zeros((ROWS_PER_DIR - 5, t), F32)], axis=0)
        out.append((nl_t, u, rows))
    return out


def _mlstm_state_start(dirn, q, u, rows, s_ref, m_ref):
    a_max, b_end = rows[3:4, 0:1], rows[4:5, 0:1]
    m_prev = m_ref[dirn]
    s_old = s_ref[dirn]
    inter = _dot_nt(s_old.astype(BF16), q)
    m_new = jnp.maximum(b_end + m_prev, a_max)
    s_ref[dirn] = jnp.exp(b_end + m_prev - m_new) * s_old + jnp.exp(a_max - m_new) * u
    m_ref[dirn] = m_new
    return inter, m_prev


def _mlstm_state_finish(inter, m_prev, nl_t, rows):
    m_loc, den_loc, b = rows[0:1], rows[1:2], rows[2:3]
    m_inter = b + m_prev
    m_t = jnp.maximum(m_inter, m_loc)
    w_inter = jnp.exp(m_inter - m_t)
    w_loc = jnp.exp(m_loc - m_t)
    den = w_inter * inter[M_DV:M_DV + 1] + w_loc * den_loc
    scale = 1.0 / jnp.maximum(jnp.abs(den), jnp.exp(-m_t))
    return (w_inter * scale) * inter[:M_DV] + (w_loc * scale) * nl_t


def _mlstm_kernel(q_ref, qt_ref, k_ref, kt_ref, v_ref, vt_ref, g_ref, gt_ref, o_ref, ot_ref,
                  nl_s, u_s, rows_s, ht_s, s_ref, m_ref):
    s_ref[...] = jnp.zeros_like(s_ref)
    m_ref[...] = jnp.zeros_like(m_ref)
    tail_cols = pl.ds(SEQ, TAIL)

    def chunk(c):
        r0 = pl.multiple_of(c * MT, MT)
        return pl.ds(r0, MT)

    def keep_local(c, cols, per_dir):
        for dirn, (nl_t, u, rows) in enumerate(per_dir):
            nl_s[dirn, :, cols] = nl_t
            u_s[dirn, c] = u
            rows_s[dirn, :, cols] = rows

    def local_pass(i, carry):
        group = [i * LOCAL_GROUP + j for j in range(LOCAL_GROUP)]
        started = [_mlstm_local_start(q_ref[chunk(c), :], k_ref[chunk(c), :], v_ref[chunk(c), :],
                                      g_ref[:, chunk(c)], False) for c in group]
        results = [_mlstm_local_finish(*st) for st in started]
        for c, per_dir in zip(group, results):
            keep_local(c, chunk(c), per_dir)
        return carry

    def state_steps(items):
        started = [_mlstm_state_start(dirn, q, u_s[dirn, c], rows_s[dirn, :, cols], s_ref, m_ref)
                   for dirn, q, c, cols in items]
        return [_mlstm_state_finish(inter, m_prev, nl_s[dirn, :, cols], rows_s[dirn, :, cols])
                for (inter, m_prev), (dirn, _, c, cols) in zip(started, items)]

    def visit_items(i):
        chunks = []
        for j in range(STATE_GROUP):
            step = i * STATE_GROUP + j
            chunks += [(0, step), (1, N_MCHUNK - 1 - step)]
        return [(dirn, q_ref[chunk(c), :], c, chunk(c)) for dirn, c in chunks]

    def first_visits(i, carry):
        items = visit_items(i)
        for (_, _, _, cols), h_t in zip(items, state_steps(items)):
            ht_s[:, cols] = h_t
        return carry

    def second_visits(i, carry):
        items = visit_items(i)
        for (_, _, _, cols), h_t in zip(items, state_steps(items)):
            o_ref[cols, :] = jnp.transpose((ht_s[:, cols] + h_t).astype(o_ref.dtype))
        return carry

    keep_local(N_MCHUNK, tail_cols, _mlstm_local_finish(
        *_mlstm_local_start(qt_ref[...], kt_ref[...], vt_ref[...], gt_ref[...], True)))
    lax.fori_loop(0, N_MCHUNK // LOCAL_GROUP, local_pass, 0)
    ht_s[:, tail_cols] = state_steps([(0, qt_ref[...], N_MCHUNK, tail_cols)])[0]
    half = N_MCHUNK // 2 // STATE_GROUP
    lax.fori_loop(0, half, first_visits, 0)
    lax.fori_loop(half, 2 * half, second_visits, 0)
    h_tail = ht_s[:, tail_cols] + state_steps([(1, qt_ref[...], N_MCHUNK, tail_cols)])[0]
    ot_ref[...] = jnp.transpose(h_tail.astype(ot_ref.dtype))


def _mlstm(q, k, xm, grow):
    tail_blk = N_REAL // TAIL
    real = lambda w: pl.BlockSpec((SEQ, w), lambda b, h: (b, h))
    tail = lambda w: pl.BlockSpec((TAIL, w), lambda b, h: (tail_blk + b, h))
    lp = SEQ + TAIL
    return pl.pallas_call(
        _mlstm_kernel,
        grid=(BATCH, M_HEADS),
        in_specs=[real(M_DK), tail(M_DK), real(M_DK), tail(M_DK), real(M_DV), tail(M_DV),
                  pl.BlockSpec((GATE_PAD, SEQ), lambda b, h: (h, b)),
                  pl.BlockSpec((GATE_PAD, TAIL), lambda b, h: (h, tail_blk + b))],
        out_specs=[pl.BlockSpec((SEQ, M_DV), lambda b, h: (b, h)),
                   pl.BlockSpec((TAIL, M_DV), lambda b, h: (b, h))],
        out_shape=[jax.ShapeDtypeStruct((N_REAL, M_INNER), BF16),
                   jax.ShapeDtypeStruct((BATCH * TAIL, M_INNER), BF16)],
        scratch_shapes=[pltpu.VMEM((2, M_DV, lp), F32),
                        pltpu.VMEM((2, N_MCHUNK + 1, STATE_ROWS, M_DK), F32),
                        pltpu.VMEM((2, ROWS_PER_DIR, lp), F32),
                        pltpu.VMEM((M_DV, lp), F32),
                        pltpu.VMEM((2, STATE_ROWS, M_DK), F32),
                        pltpu.VMEM((2, 1, 1), F32)],
        compiler_params=_params("parallel", "parallel"),
        name="mlstm",
    )(q, q, k, k, xm, xm, grow, grow)


PAIR = 2 * NA_DH


def _natten_kernel(q_ref, qt_ref, k_ref, kt_ref, v_ref, vt_ref, bias_ref, o_ref, ot_ref):
    lane = lax.broadcasted_iota(jnp.int32, (1, PAIR), 1)
    first = lane < NA_DH
    k_tail = kt_ref[...]
    v_tail = vt_ref[...]

    def attend(q, keys, vals, bias_t):
        n = q.shape[0]
        zero = jnp.zeros_like(q)
        qs = jnp.concatenate([jnp.where(first, q, zero), jnp.where(first, zero, q)], axis=0)
        s = _dot_nt(keys, qs) + bias_t
        e = jnp.exp2(s - jnp.max(s, axis=0, keepdims=True))
        o = _dot(jnp.transpose(e.astype(BF16)), vals)
        inv = jnp.broadcast_to(1.0 / jnp.sum(e, axis=0, keepdims=True), (PAIR, 2 * n))
        o = o * jnp.transpose(inv)
        return jnp.where(first, o[:n], o[n:])

    def row_block(r, carry):
        r0 = jnp.clip(r - NA_WIN_H // 2, 0, ROWS - NA_WIN_H)
        qrows = pl.ds(pl.multiple_of(r * GRID_W, GRID_W), GRID_W)
        krows = pl.ds(pl.multiple_of(r0 * GRID_W, GRID_W), NA_WIN_H * GRID_W)
        keys = jnp.concatenate([k_ref[krows, :], k_tail], axis=0)
        vals = jnp.concatenate([v_ref[krows, :], v_tail], axis=0)
        o = attend(q_ref[qrows, :], keys, vals, bias_ref[r - r0])
        o_ref[qrows, :] = o.astype(o_ref.dtype)
        return carry

    lax.fori_loop(0, ROWS, row_block, 0, unroll=64)
    tb = bias_ref[0, NA_WIN_H * GRID_W:, :]
    tail_bias = jnp.concatenate([jnp.broadcast_to(tb[:, 0:1], (TAIL, TAIL)),
                                 jnp.broadcast_to(tb[:, NA_DH:NA_DH + 1], (TAIL, TAIL))], axis=1)
    ot_ref[...] = attend(qt_ref[...], k_tail, v_tail, tail_bias).astype(ot_ref.dtype)


def _natten(qn, kn, vn, bias):
    tail_blk = N_REAL // TAIL
    real = pl.BlockSpec((SEQ, PAIR), lambda b, p: (b, p))
    tail = pl.BlockSpec((TAIL, PAIR), lambda b, p: (tail_blk + b, p))
    return pl.pallas_call(
        _natten_kernel,
        grid=(BATCH, NA_HEADS // 2),
        in_specs=[real, tail, real, tail, real, tail,
                  pl.BlockSpec((None, NA_WIN_H, NA_WIN_H * GRID_W + TAIL, PAIR), lambda b, p: (p, 0, 0, 0))],
        out_specs=[pl.BlockSpec((SEQ, PAIR), lambda b, p: (b, p)),
                   pl.BlockSpec((TAIL, PAIR), lambda b, p: (b, p))],
        out_shape=[jax.ShapeDtypeStruct((N_REAL, NA_INNER), BF16),
                   jax.ShapeDtypeStruct((BATCH * TAIL, NA_INNER), BF16)],
        compiler_params=_params("parallel", "parallel"),
        name="natten",
    )(qn, qn, kn, kn, vn, vn, bias)


def _natten_bias(rpb, meta_bias):
    qc = jnp.arange(GRID_W)
    kc = jnp.arange(GRID_W)
    win0 = jnp.clip(qc - NA_WIN_W // 2, 0, GRID_W - NA_WIN_W)
    ok = (kc[:, None] >= win0[None, :]) & (kc[:, None] < win0[None, :] + NA_WIN_W)
    dc = jnp.clip(kc[:, None] - qc[None, :], -(NA_WIN_W - 1), NA_WIN_W - 1) + NA_WIN_W - 1
    onehot = (dc[None] == jnp.arange(2 * NA_WIN_W - 1)[:, None, None]).astype(F32)
    t1 = jnp.einsum('hdj,jkq->dkhq', rpb.astype(F32), onehot, precision=lax.Precision.HIGHEST)
    t1 = jnp.where(ok[None, :, None, :], t1, MASKED) * LOG2E
    t1 = jnp.transpose(t1.reshape(2 * NA_WIN_H - 1, GRID_W, NA_HEADS // 2, PAIR), (2, 0, 1, 3))
    met = jnp.concatenate([jnp.full((NA_HEADS, NPAD), MASKED, F32), meta_bias.astype(F32)], axis=1) * LOG2E
    met = jnp.broadcast_to(met.reshape(NA_HEADS // 2, 2, TAIL, 1), (NA_HEADS // 2, 2, TAIL, NA_DH))
    met = jnp.transpose(met, (0, 2, 1, 3)).reshape(NA_HEADS // 2, TAIL, PAIR)
    rows = [jnp.concatenate([t1[:, NA_WIN_H - 1 - dl:2 * NA_WIN_H - 1 - dl].reshape(-1, NA_WIN_H * GRID_W, PAIR),
                             met], axis=1) for dl in range(NA_WIN_H)]
    return jnp.stack(rows, axis=1)


def _merge_kernel(hsr_ref, hst_ref, ybr_ref, ybt_ref, xc_ref, og_ref, gab_ref,
                  ng_ref, sk_ref, wa_ref, wb_ref, wo_ref, w1_ref, w2_ref, o_ref, w1o_ref, w2o_ref):
    w1o_ref[...] = w1_ref[...].astype(BF16)
    w2o_ref[...] = w2_ref[...].astype(BF16)

    def body(hs_ref, yb_ref):
        hs = hs_ref[...].astype(F32)
        parts = []
        for hd in range(M_HEADS):
            sl = hs[:, hd * M_DV:(hd + 1) * M_DV]
            parts.append(sl * lax.rsqrt(jnp.mean(sl * sl, axis=-1, keepdims=True) + EPS))
        hn = jnp.concatenate(parts, axis=1) * ng_ref[...]
        y_a = _sigmoid(og_ref[...].astype(F32)) * (hn + sk_ref[...] * xc_ref[...].astype(F32))
        gab = gab_ref[...].astype(F32)
        mix = (_sigmoid(gab[:, :D_MODEL]) * _dot(y_a.astype(BF16), wa_ref[...])
               + _sigmoid(gab[:, D_MODEL:]) * _dot(yb_ref[...], wb_ref[...]))
        o_ref[...] = _dot(mix.astype(BF16), wo_ref[...])

    _real_or_tail(body, hsr_ref, hst_ref, ybr_ref, ybt_ref)


def _merge(hs_real, hs_tail, yb_real, yb_tail, xc, og, gab, ng, sk, wa, wb, wo, w1, w2):
    return pl.pallas_call(
        _merge_kernel,
        grid=(N_TILES,),
        in_specs=[_real_spec(M_INNER), _tail_spec(M_INNER), _real_spec(NA_INNER), _tail_spec(NA_INNER),
                  _row_spec(M_INNER), _row_spec(M_INNER), _row_spec(2 * D_MODEL)]
                 + [_const_spec(a.shape) for a in (ng, sk, wa, wb, wo)]
                 + [_slab_spec(w1.shape), _slab_spec(w2.shape)],
        out_specs=[_row_spec(D_MODEL), _slab_spec(w1.shape), _slab_spec(w2.shape)],
        out_shape=[jax.ShapeDtypeStruct((NT, D_MODEL), F32),
                   jax.ShapeDtypeStruct(w1.shape, BF16), jax.ShapeDtypeStruct(w2.shape, BF16)],
        compiler_params=_params("arbitrary"),
        name="merge",
    )(hs_real, hs_tail, yb_real, yb_tail, xc, og, gab, ng, sk, wa, wb, wo, w1, w2)


def _ffn_kernel(d_ref, xr_ref, xt_ref, g2_ref, w1_ref, w2_ref, or_ref, ot_ref):
    def body(x_ref, o_ref):
        h = x_ref[...] + d_ref[...]
        xn = (h * lax.rsqrt(jnp.mean(h * h, axis=-1, keepdims=True) + EPS) * g2_ref[...]).astype(BF16)
        z = jnp.maximum(_dot(xn, w1_ref[...]), 0.0)
        o_ref[...] = h + _dot((z * z).astype(BF16), w2_ref[...])

    _real_or_tail(body, xr_ref, xt_ref, or_ref, ot_ref)


def _ffn(delta, x2d, tailh, g2, w1, w2):
    return pl.pallas_call(
        _ffn_kernel,
        grid=(N_TILES,),
        in_specs=[_row_spec(D_MODEL), _real_spec(D_MODEL), _tail_spec(D_MODEL),
                  _const_spec(g2.shape), _const_spec(w1.shape), _const_spec(w2.shape)],
        out_specs=[_real_spec(D_MODEL), _tail_spec(D_MODEL)],
        out_shape=[jax.ShapeDtypeStruct((N_REAL, D_MODEL), F32),
                   jax.ShapeDtypeStruct((BATCH * TAIL, D_MODEL), F32)],
        compiler_params=_params("arbitrary"),
        name="ffn",
    )(delta, x2d, tailh, g2, w1, w2)


def kernel(x, meta_tokens, norm1_g, w_in, mlstm_conv_w, mlstm_conv_b, mlstm_wq, mlstm_wk, mlstm_gate_b, mlstm_norm_g, mlstm_skip, na_q_norm_g, na_k_norm_g, na_rpb, na_meta_bias, w_branch_a, w_branch_b, w_out, norm2_g, w_ff1, w_ff2):
    x2d = x.astype(F32).reshape(N_REAL, D_MODEL)
    tail = jnp.concatenate([jnp.zeros((NPAD, D_MODEL), F32), meta_tokens.astype(F32)], axis=0)
    tailh = jnp.tile(tail, (BATCH, 1))

    w_all = w_in.astype(BF16)
    pad_gates = lambda a: jnp.pad(
        jnp.swapaxes(a.reshape(a.shape[0], 4, M_HEADS), 1, 2),
        ((0, 0), (0, 0), (0, GATE_PAD - 4))).reshape(a.shape[0], M_HEADS * GATE_PAD)
    wg = jnp.pad(pad_gates(w_all[:, OFF_MG:OFF_Q]), ((0, 0), (0, LANES - M_HEADS * GATE_PAD)))
    gb = jnp.pad(pad_gates(mlstm_gate_b.astype(F32).reshape(1, 4 * M_HEADS)),
                 ((0, 0), (0, LANES - M_HEADS * GATE_PAD)))
    bd = jnp.kron(jnp.eye(NORM_GROUP // NA_DH, dtype=F32),
                  jnp.full((NA_DH, NA_DH), 1.0 / NA_DH, F32)).astype(BF16)
    qg = jnp.tile(na_q_norm_g.astype(F32), NA_HEADS)[None, :] * (NA_DH ** -0.5 * LOG2E)
    kg = jnp.tile(na_k_norm_g.astype(F32), NA_HEADS)[None, :]

    xb = x.astype(F32)
    xedge = jnp.concatenate([xb[:, SEQ - 1, :], xb[:, 0, :]], axis=0)
    wqk = jnp.concatenate([mlstm_wq, mlstm_wk], axis=-1).astype(BF16)
    vm, og, grow, qn, kn, vn, gab, xc, qm, km = _inproj(
        x2d, tailh, xedge, norm1_g.astype(F32)[None, :], w_all, wg, gb, qg, kg, bd,
        mlstm_conv_w.astype(F32).reshape(3, M_INNER), mlstm_conv_b.astype(F32)[None, :], wqk)

    hs_real, hs_tail = _mlstm(qm, km, vm, grow)
    yb_real, yb_tail = _natten(qn, kn, vn, _natten_bias(na_rpb, na_meta_bias))

    delta, w1, w2 = _merge(hs_real, hs_tail, yb_real, yb_tail, xc, og, gab,
                           mlstm_norm_g.astype(F32).reshape(1, M_INNER), mlstm_skip.astype(F32)[None, :],
                           w_branch_a.astype(BF16), w_branch_b.astype(BF16), w_out.astype(BF16),
                           w_ff1.astype(F32), w_ff2.astype(F32))
    out_real, _ = _ffn(delta, x2d, tailh, norm2_g.astype(F32)[None, :], w1, w2)
    return out_real.reshape(BATCH, SEQ, D_MODEL)
```

```python
import jax
import jax.numpy as jnp
from jax import lax
from jax.experimental import pallas as pl
from jax.experimental.pallas import tpu as pltpu

D_MODEL = 1024
BATCH = 4
SEQ = 4096
N_META = 16
GRID_W = 64
ROWS = SEQ // GRID_W
M_HEADS = 4
M_DV = 256
M_DK = 128
M_INNER = M_HEADS * M_DV
NA_HEADS = 8
NA_DH = 64
NA_INNER = NA_HEADS * NA_DH
NA_WIN_H = 8
NA_WIN_W = 16
D_FF = 4 * D_MODEL
EPS = 1e-6
NEG_LOG_GATE = -1e9
MASKED = -1e30
LOG2E = 1.4426950408889634

TAIL = 128
NPAD = TAIL - N_META
N_REAL = BATCH * SEQ
NT = N_REAL + BATCH * TAIL
TM = 512
N_TILES = NT // TM
N_REAL_TILES = N_REAL // TM

LANES = 128
BF16_ROWS = 16
VMEM_LIMIT = 56 * 1024 * 1024

F32 = jnp.float32
BF16 = jnp.bfloat16


def _dot(a, b):
    return jnp.dot(a, b, preferred_element_type=F32)


def _dot_nt(a, b):
    return lax.dot_general(a, b, (((1,), (1,)), ((), ())), preferred_element_type=F32)


def _sigmoid(z):
    return 0.5 * jnp.tanh(0.5 * z) + 0.5


def _log_sigmoid(z):
    return jnp.minimum(z, 0.0) - jnp.log1p(jnp.exp(-jnp.abs(z)))


def _const_spec(shape):
    nd = len(shape)
    return pl.BlockSpec(shape, lambda *_: (0,) * nd, pipeline_mode=pl.Buffered(1))


def _params(*sem):
    return pltpu.CompilerParams(dimension_semantics=sem, vmem_limit_bytes=VMEM_LIMIT)


GATE_PAD = 8
OFF_MX = 0
OFF_MO = OFF_MX + M_INNER
OFF_MG = OFF_MO + M_INNER
OFF_Q = OFF_MG + 4 * M_HEADS
OFF_G = OFF_Q + 3 * NA_INNER
D_IN_PROJ = OFF_G + 2 * D_MODEL


HALO = 8
HALO_PAD = BF16_ROWS
TILES_PER_SEQ = SEQ // TM
NORM_GROUP = 256


def _inproj_kernel(x_ref, tail_ref, xprev_ref, xnext_ref, xedge_ref, g1_ref, w_ref, wg_ref,
                   gb_ref, qg_ref, kg_ref, bd_ref, cw_ref, cb_ref, wqk_ref,
                   vm_ref, og_ref, grow_ref, q_ref, k_ref, v_ref, gab_ref, xc_ref, qm_ref, km_ref,
                   wqkv_s, wgab_s):
    i = pl.program_id(0)

    @pl.when(i == 0)
    def _():
        wqkv_s[...] = w_ref[:, OFF_Q:OFF_G]
        wgab_s[...] = w_ref[:, OFF_G:D_IN_PROJ]

    def norm1(h):
        ms = jnp.mean(h * h, axis=-1, keepdims=True)
        return (h * lax.rsqrt(ms + EPS) * g1_ref[...]).astype(BF16)

    def body(h_ref, is_tail):
        xn = norm1(h_ref[...])
        if is_tail:
            halo = jnp.concatenate([xedge_ref[...], jnp.zeros((HALO_PAD - 2 * BATCH, D_MODEL), F32)], axis=0)
        else:
            first = i % TILES_PER_SEQ == 0
            last = i % TILES_PER_SEQ == TILES_PER_SEQ - 1
            prev = jnp.where(first, tail_ref[TAIL - 1:TAIL, :], xprev_ref[HALO - 1:HALO, :])
            nxt = jnp.where(last, tail_ref[0:1, :], xnext_ref[0:1, :])
            halo = jnp.concatenate([prev, nxt, jnp.zeros((HALO_PAD - 2, D_MODEL), F32)], axis=0)
        xm_ext = _dot(jnp.concatenate([xn, norm1(halo)], axis=0), w_ref[:, OFF_MX:OFF_MO])
        og = _dot(xn, w_ref[:, OFF_MO:OFF_MG])
        gates = _dot(xn, wg_ref[...]) + gb_ref[...]
        gab = _dot(xn, wgab_s[...])
        qkv = _dot(xn, wqkv_s[...])
        xm = xm_ext[:TM]
        rows = lax.broadcasted_iota(jnp.int32, (TM, M_INNER), 0)
        x_m1 = pltpu.roll(xm, 1, axis=0)
        x_p1 = pltpu.roll(xm, TM - 1, axis=0)
        if is_tail:
            for b in range(BATCH):
                x_m1 = jnp.where(rows == b * TAIL, xm_ext[TM + b:TM + b + 1], x_m1)
                x_p1 = jnp.where(rows == b * TAIL + TAIL - 1, xm_ext[TM + BATCH + b:TM + BATCH + b + 1], x_p1)
        else:
            x_m1 = jnp.where(rows == 0, xm_ext[TM:TM + 1], x_m1)
            x_p1 = jnp.where(rows == TM - 1, xm_ext[TM + 1:TM + 2], x_p1)
        vm_ref[...] = xm.astype(BF16)
        z = cw_ref[0:1, :] * x_m1 + cw_ref[1:2, :] * xm + cw_ref[2:3, :] * x_p1 + cb_ref[...]
        xc = z * _sigmoid(z)
        xc_ref[...] = xc.astype(BF16)
        for hd in range(M_HEADS):
            qk = _dot(xc[:, hd * M_DV:(hd + 1) * M_DV].astype(BF16), wqk_ref[hd])
            qm_ref[:, hd * M_DK:(hd + 1) * M_DK] = qk[:, :M_DK].astype(BF16)
            km_ref[:, hd * M_DK:(hd + 1) * M_DK] = (qk[:, M_DK:] * (M_DK ** -0.5)).astype(BF16)
        og_ref[...] = og.astype(BF16)
        grow_ref[...] = jnp.transpose(gates)[:M_HEADS * GATE_PAD, :]
        gab_ref[...] = gab.astype(BF16)
        uq = qkv[:, :NA_INNER]
        uk = qkv[:, NA_INNER:2 * NA_INNER]
        def head_mean(sq):
            return jnp.concatenate([_dot(sq[:, j:j + NORM_GROUP].astype(BF16), bd_ref[...])
                                    for j in range(0, NA_INNER, NORM_GROUP)], axis=1)

        msq = head_mean(uq * uq)
        msk = head_mean(uk * uk)
        q_ref[...] = (uq * lax.rsqrt(msq + EPS) * qg_ref[...]).astype(BF16)
        k_ref[...] = (uk * lax.rsqrt(msk + EPS) * kg_ref[...]).astype(BF16)
        v_ref[...] = qkv[:, 2 * NA_INNER:].astype(BF16)

    pl.when(i < N_REAL_TILES)(lambda: body(x_ref, False))
    pl.when(i == N_REAL_TILES)(lambda: body(tail_ref, True))


def _real_or_tail(body, *ref_pairs):
    i = pl.program_id(0)
    pl.when(i < N_REAL_TILES)(lambda: body(*ref_pairs[0::2]))
    pl.when(i == N_REAL_TILES)(lambda: body(*ref_pairs[1::2]))


def _real_spec(width):
    return pl.BlockSpec((TM, width), lambda i: (jnp.minimum(i, N_REAL_TILES - 1), 0))


def _tail_spec(width):
    return pl.BlockSpec((TM, width), lambda i: (0, 0))


def _row_spec(width):
    return pl.BlockSpec((TM, width), lambda i: (i, 0))


def _slab_spec(shape):
    return pl.BlockSpec((shape[0] // N_REAL_TILES, shape[1]), lambda i: (jnp.minimum(i, N_REAL_TILES - 1), 0))


def _inproj(x2d, tailh, xedge, g1, w_all, wg, gb, qg, kg, bd, cw, cb, wqk):
    halo_blocks = N_REAL // HALO
    per_tile = TM // HALO
    consts = (g1, w_all, wg, gb, qg, kg, bd, cw, cb, wqk)
    return pl.pallas_call(
        _inproj_kernel,
        grid=(N_TILES,),
        in_specs=[_real_spec(D_MODEL), _tail_spec(D_MODEL),
                  pl.BlockSpec((HALO, D_MODEL), lambda i: (jnp.clip(i * per_tile - 1, 0, halo_blocks - 1), 0)),
                  pl.BlockSpec((HALO, D_MODEL), lambda i: (jnp.clip((i + 1) * per_tile, 0, halo_blocks - 1), 0)),
                  _const_spec(xedge.shape)]
                 + [_const_spec(a.shape) for a in consts],
        out_specs=[_row_spec(M_INNER), _row_spec(M_INNER),
                   pl.BlockSpec((M_HEADS * GATE_PAD, TM), lambda i: (0, i)),
                   _row_spec(NA_INNER), _row_spec(NA_INNER), _row_spec(NA_INNER), _row_spec(2 * D_MODEL),
                   _row_spec(M_INNER), _row_spec(M_HEADS * M_DK), _row_spec(M_HEADS * M_DK)],
        out_shape=[jax.ShapeDtypeStruct((NT, M_INNER), BF16),
                   jax.ShapeDtypeStruct((NT, M_INNER), BF16),
                   jax.ShapeDtypeStruct((M_HEADS * GATE_PAD, NT), F32),
                   jax.ShapeDtypeStruct((NT, NA_INNER), BF16),
                   jax.ShapeDtypeStruct((NT, NA_INNER), BF16),
                   jax.ShapeDtypeStruct((NT, NA_INNER), BF16),
                   jax.ShapeDtypeStruct((NT, 2 * D_MODEL), BF16),
                   jax.ShapeDtypeStruct((NT, M_INNER), BF16),
                   jax.ShapeDtypeStruct((NT, M_HEADS * M_DK), BF16),
                   jax.ShapeDtypeStruct((NT, M_HEADS * M_DK), BF16)],
        scratch_shapes=[pltpu.VMEM((D_MODEL, OFF_G - OFF_Q), BF16),
                        pltpu.VMEM((D_MODEL, D_IN_PROJ - OFF_G), BF16)],
        compiler_params=_params("arbitrary"),
        name="inproj",
    )(x2d, tailh, x2d, x2d, xedge, *consts)


MT = 256
N_MCHUNK = SEQ // MT
STATE_ROWS = M_DV + BF16_ROWS
ROWS_PER_DIR = 8
STATE_GROUP = N_MCHUNK + 1


def _split3(x):
    hi = x.astype(BF16)
    r1 = x - hi.astype(F32)
    mid = r1.astype(BF16)
    lo = (r1 - mid.astype(F32)).astype(BF16)
    return hi, mid, lo


def _mlstm_local_start(q, k, v, g8, is_tail):
    t = q.shape[0]
    li = [g8[0:1], g8[2:3]]
    lf = [_log_sigmoid(g8[1:2]), _log_sigmoid(g8[3:4])]
    if is_tail:
        pad = lax.broadcasted_iota(jnp.int32, (1, t), 1) < NPAD
        li = [jnp.where(pad, NEG_LOG_GATE, x) for x in li]
        lf = [jnp.where(pad, 0.0, x) for x in lf]
    si = lax.broadcasted_iota(jnp.int32, (t, t), 0)
    ti = lax.broadcasted_iota(jnp.int32, (t, t), 1)
    hi, mid, lo = _split3(jnp.concatenate(lf, axis=0))
    lhs = jnp.concatenate([hi, mid, lo, jnp.zeros((BF16_ROWS - 6, t), BF16)], axis=0)
    pref = _dot(lhs, (si <= ti).astype(BF16))
    pre_f = pref[0:1] + pref[2:3] + pref[4:5]
    pre_b = pref[1:2] + pref[3:4] + pref[5:6]
    b_end = [pre_f[:, t - 1:t], pre_b[:, t - 1:t]]
    b = [pre_f, b_end[1] - pre_b + lf[1]]
    s_t = _dot_nt(k, q)
    v_tb = jnp.transpose(v.astype(BF16))
    return li, b, b_end, s_t, v_tb, k


def _mlstm_local_finish(li, b, b_end, s_t, v_tb, k):
    t = s_t.shape[0]
    si = lax.broadcasted_iota(jnp.int32, (t, t), 0)
    ti = lax.broadcasted_iota(jnp.int32, (t, t), 1)
    kf = k.astype(F32)
    v_ext = jnp.concatenate([v_tb, jnp.ones((STATE_ROWS - M_DV, t), BF16)], axis=0)
    out = []
    for dirn in range(2):
        g2 = (li[dirn] - b[dirn]) * LOG2E
        g_col = jnp.transpose(jnp.broadcast_to(g2, (LANES, t)))
        g_st = jnp.where((si <= ti) if dirn == 0 else (si >= ti),
                         jnp.concatenate([g_col] * (t // LANES), axis=1), MASKED)
        g_max = jnp.max(g_st, axis=0, keepdims=True)
        p_t = s_t * jnp.exp2(g_st - g_max)
        den = jnp.sum(p_t, axis=0, keepdims=True)
        nl_t = _dot(v_tb, p_t.astype(BF16))
        m_loc = b[dirn] + g_max * (1.0 / LOG2E)
        top = jnp.max(g2, axis=1, keepdims=True)
        kw = (kf * jnp.exp2(g_col - top)).astype(BF16)
        u = _dot(v_ext, kw)
        a_max = b_end[dirn] + top * (1.0 / LOG2E)
        rows = jnp.concatenate([m_loc, den, b[dirn], jnp.broadcast_to(a_max, (1, t)),
                                jnp.broadcast_to(b_end[dirn], (1, t)),
                                jnp.zeros((ROWS_PER_DIR - 5, t), F32)], axis=0)
        out.append((nl_t, u, rows))
    return out


def _mlstm_state_start(dirn, q, u, rows, s_ref, m_ref):
    a_max, b_end = rows[3:4, 0:1], rows[4:5, 0:1]
    m_prev = m_ref[dirn]
    s_old = s_ref[dirn]
    inter = _dot_nt(s_old.astype(BF16), q)
    m_new = jnp.maximum(b_end + m_prev, a_max)
    s_ref[dirn] = jnp.exp(b_end + m_prev - m_new) * s_old + jnp.exp(a_max - m_new) * u
    m_ref[dirn] = m_new
    return inter, m_prev


def _mlstm_state_finish(inter, m_prev, nl_t, rows):
    m_loc, den_loc, b = rows[0:1], rows[1:2], rows[2:3]
    m_inter = b + m_prev
    m_t = jnp.maximum(m_inter, m_loc)
    w_inter = jnp.exp(m_inter - m_t)
    w_loc = jnp.exp(m_loc - m_t)
    den = w_inter * inter[M_DV:M_DV + 1] + w_loc * den_loc
    scale = 1.0 / jnp.maximum(jnp.abs(den), jnp.exp(-m_t))
    return (w_inter * scale) * inter[:M_DV] + (w_loc * scale) * nl_t


def _mlstm_kernel(q_ref, qt_ref, k_ref, kt_ref, v_ref, vt_ref, g_ref, gt_ref, o_ref, ot_ref,
                  nl_s, u_s, rows_s, ht_s, s_ref, m_ref):
    s_ref[...] = jnp.zeros_like(s_ref)
    m_ref[...] = jnp.zeros_like(m_ref)
    tail = N_MCHUNK

    def cols(c):
        return pl.ds(SEQ, TAIL) if c == tail else pl.ds(c * MT, MT)

    def rows(ref, ref_tail, c):
        return ref_tail[...] if c == tail else ref[c * MT:(c + 1) * MT, :]

    def gates(c):
        return gt_ref[...] if c == tail else g_ref[:, c * MT:(c + 1) * MT]

    order = [tail] + list(range(N_MCHUNK))
    started = [_mlstm_local_start(rows(q_ref, qt_ref, c), rows(k_ref, kt_ref, c), rows(v_ref, vt_ref, c),
                                  gates(c), c == tail) for c in order]
    for c, st in zip(order, started):
        for dirn, (nl_t, u, row_vecs) in enumerate(_mlstm_local_finish(*st)):
            nl_s[dirn, :, cols(c)] = nl_t
            u_s[dirn, c] = u
            rows_s[dirn, :, cols(c)] = row_vecs

    fwd = [(0, c) for c in order]
    bwd = [(1, c) for c in reversed(order)]
    visits = [v for pair in zip(fwd, bwd) for v in pair]
    seen = set()
    for lo in range(0, len(visits), STATE_GROUP):
        group = visits[lo:lo + STATE_GROUP]
        started = [_mlstm_state_start(dirn, rows(q_ref, qt_ref, c), u_s[dirn, c], rows_s[dirn, :, cols(c)],
                                      s_ref, m_ref) for dirn, c in group]
        for (inter, m_prev), (dirn, c) in zip(started, group):
            h_t = _mlstm_state_finish(inter, m_prev, nl_s[dirn, :, cols(c)], rows_s[dirn, :, cols(c)])
            if c not in seen:
                seen.add(c)
                ht_s[:, cols(c)] = h_t
                continue
            out = jnp.transpose((ht_s[:, cols(c)] + h_t).astype(o_ref.dtype))
            if c == tail:
                ot_ref[...] = out
            else:
                o_ref[c * MT:(c + 1) * MT, :] = out


def _mlstm(q, k, xm, grow):
    tail_blk = N_REAL // TAIL
    real = lambda w: pl.BlockSpec((SEQ, w), lambda b, h: (b, h))
    tail = lambda w: pl.BlockSpec((TAIL, w), lambda b, h: (tail_blk + b, h))
    lp = SEQ + TAIL
    return pl.pallas_call(
        _mlstm_kernel,
        grid=(BATCH, M_HEADS),
        in_specs=[real(M_DK), tail(M_DK), real(M_DK), tail(M_DK), real(M_DV), tail(M_DV),
                  pl.BlockSpec((GATE_PAD, SEQ), lambda b, h: (h, b)),
                  pl.BlockSpec((GATE_PAD, TAIL), lambda b, h: (h, tail_blk + b))],
        out_specs=[pl.BlockSpec((SEQ, M_DV), lambda b, h: (b, h)),
                   pl.BlockSpec((TAIL, M_DV), lambda b, h: (b, h))],
        out_shape=[jax.ShapeDtypeStruct((N_REAL, M_INNER), BF16),
                   jax.ShapeDtypeStruct((BATCH * TAIL, M_INNER), BF16)],
        scratch_shapes=[pltpu.VMEM((2, M_DV, lp), F32),
                        pltpu.VMEM((2, N_MCHUNK + 1, STATE_ROWS, M_DK), F32),
                        pltpu.VMEM((2, ROWS_PER_DIR, lp), F32),
                        pltpu.VMEM((M_DV, lp), F32),
                        pltpu.VMEM((2, STATE_ROWS, M_DK), F32),
                        pltpu.VMEM((2, 1, 1), F32)],
        compiler_params=_params("parallel", "parallel"),
        name="mlstm",
    )(q, q, k, k, xm, xm, grow, grow)


PAIR = 2 * NA_DH


def _natten_kernel(q_ref, qt_ref, k_ref, kt_ref, v_ref, vt_ref, bias_ref, o_ref, ot_ref):
    lane = lax.broadcasted_iota(jnp.int32, (1, PAIR), 1)
    first = lane < NA_DH
    k_tail = kt_ref[...]
    v_tail = vt_ref[...]

    def attend(q, keys, vals, bias_t):
        n = q.shape[0]
        zero = jnp.zeros_like(q)
        qs = jnp.concatenate([jnp.where(first, q, zero), jnp.where(first, zero, q)], axis=0)
        s = _dot_nt(keys, qs) + bias_t
        e = jnp.exp2(s - jnp.max(s, axis=0, keepdims=True))
        o = _dot(jnp.transpose(e.astype(BF16)), vals)
        inv = jnp.broadcast_to(1.0 / jnp.sum(e, axis=0, keepdims=True), (PAIR, 2 * n))
        o = o * jnp.transpose(inv)
        return jnp.where(first, o[:n], o[n:])

    def row_block(r, carry):
        r0 = jnp.clip(r - NA_WIN_H // 2, 0, ROWS - NA_WIN_H)
        qrows = pl.ds(pl.multiple_of(r * GRID_W, GRID_W), GRID_W)
        krows = pl.ds(pl.multiple_of(r0 * GRID_W, GRID_W), NA_WIN_H * GRID_W)
        keys = jnp.concatenate([k_ref[krows, :], k_tail], axis=0)
        vals = jnp.concatenate([v_ref[krows, :], v_tail], axis=0)
        o = attend(q_ref[qrows, :], keys, vals, bias_ref[r - r0])
        o_ref[qrows, :] = o.astype(o_ref.dtype)
        return carry

    lax.fori_loop(0, ROWS, row_block, 0, unroll=64)
    tb = bias_ref[0, NA_WIN_H * GRID_W:, :]
    tail_bias = jnp.concatenate([jnp.broadcast_to(tb[:, 0:1], (TAIL, TAIL)),
                                 jnp.broadcast_to(tb[:, NA_DH:NA_DH + 1], (TAIL, TAIL))], axis=1)
    ot_ref[...] = attend(qt_ref[...], k_tail, v_tail, tail_bias).astype(ot_ref.dtype)


def _natten(qn, kn, vn, bias):
    tail_blk = N_REAL // TAIL
    real = pl.BlockSpec((SEQ, PAIR), lambda b, p: (b, p))
    tail = pl.BlockSpec((TAIL, PAIR), lambda b, p: (tail_blk + b, p))
    return pl.pallas_call(
        _natten_kernel,
        grid=(BATCH, NA_HEADS // 2),
        in_specs=[real, tail, real, tail, real, tail,
                  pl.BlockSpec((None, NA_WIN_H, NA_WIN_H * GRID_W + TAIL, PAIR), lambda b, p: (p, 0, 0, 0))],
        out_specs=[pl.BlockSpec((SEQ, PAIR), lambda b, p: (b, p)),
                   pl.BlockSpec((TAIL, PAIR), lambda b, p: (b, p))],
        out_shape=[jax.ShapeDtypeStruct((N_REAL, NA_INNER), BF16),
                   jax.ShapeDtypeStruct((BATCH * TAIL, NA_INNER), BF16)],
        compiler_params=_params("parallel", "parallel"),
        name="natten",
    )(qn, qn, kn, kn, vn, vn, bias)


def _natten_bias(rpb, meta_bias):
    qc = jnp.arange(GRID_W)
    kc = jnp.arange(GRID_W)
    win0 = jnp.clip(qc - NA_WIN_W // 2, 0, GRID_W - NA_WIN_W)
    ok = (kc[:, None] >= win0[None, :]) & (kc[:, None] < win0[None, :] + NA_WIN_W)
    dc = jnp.clip(kc[:, None] - qc[None, :], -(NA_WIN_W - 1), NA_WIN_W - 1) + NA_WIN_W - 1
    onehot = (dc[None] == jnp.arange(2 * NA_WIN_W - 1)[:, None, None]).astype(F32)
    t1 = jnp.einsum('hdj,jkq->dkhq', rpb.astype(F32), onehot, precision=lax.Precision.HIGHEST)
    t1 = jnp.where(ok[None, :, None, :], t1, MASKED) * LOG2E
    t1 = jnp.transpose(t1.reshape(2 * NA_WIN_H - 1, GRID_W, NA_HEADS // 2, PAIR), (2, 0, 1, 3))
    met = jnp.concatenate([jnp.full((NA_HEADS, NPAD), MASKED, F32), meta_bias.astype(F32)], axis=1) * LOG2E
    met = jnp.broadcast_to(met.reshape(NA_HEADS // 2, 2, TAIL, 1), (NA_HEADS // 2, 2, TAIL, NA_DH))
    met = jnp.transpose(met, (0, 2, 1, 3)).reshape(NA_HEADS // 2, TAIL, PAIR)
    rows = [jnp.concatenate([t1[:, NA_WIN_H - 1 - dl:2 * NA_WIN_H - 1 - dl].reshape(-1, NA_WIN_H * GRID_W, PAIR),
                             met], axis=1) for dl in range(NA_WIN_H)]
    return jnp.stack(rows, axis=1)


def _merge_kernel(hsr_ref, hst_ref, ybr_ref, ybt_ref, xc_ref, og_ref, gab_ref,
                  ng_ref, sk_ref, wa_ref, wb_ref, wo_ref, w1_ref, w2_ref, o_ref, w1o_ref, w2o_ref):
    w1o_ref[...] = w1_ref[...].astype(BF16)
    w2o_ref[...] = w2_ref[...].astype(BF16)

    def body(hs_ref, yb_ref):
        hs = hs_ref[...].astype(F32)
        parts = []
        for hd in range(M_HEADS):
            sl = hs[:, hd * M_DV:(hd + 1) * M_DV]
            parts.append(sl * lax.rsqrt(jnp.mean(sl * sl, axis=-1, keepdims=True) + EPS))
        hn = jnp.concatenate(parts, axis=1) * ng_ref[...]
        y_a = _sigmoid(og_ref[...].astype(F32)) * (hn + sk_ref[...] * xc_ref[...].astype(F32))
        gab = gab_ref[...].astype(F32)
        mix = (_sigmoid(gab[:, :D_MODEL]) * _dot(y_a.astype(BF16), wa_ref[...])
               + _sigmoid(gab[:, D_MODEL:]) * _dot(yb_ref[...], wb_ref[...]))
        o_ref[...] = _dot(mix.astype(BF16), wo_ref[...])

    _real_or_tail(body, hsr_ref, hst_ref, ybr_ref, ybt_ref)


def _merge(hs_real, hs_tail, yb_real, yb_tail, xc, og, gab, ng, sk, wa, wb, wo, w1, w2):
    return pl.pallas_call(
        _merge_kernel,
        grid=(N_TILES,),
        in_specs=[_real_spec(M_INNER), _tail_spec(M_INNER), _real_spec(NA_INNER), _tail_spec(NA_INNER),
                  _row_spec(M_INNER), _row_spec(M_INNER), _row_spec(2 * D_MODEL)]
                 + [_const_spec(a.shape) for a in (ng, sk, wa, wb, wo)]
                 + [_slab_spec(w1.shape), _slab_spec(w2.shape)],
        out_specs=[_row_spec(D_MODEL), _slab_spec(w1.shape), _slab_spec(w2.shape)],
        out_shape=[jax.ShapeDtypeStruct((NT, D_MODEL), F32),
                   jax.ShapeDtypeStruct(w1.shape, BF16), jax.ShapeDtypeStruct(w2.shape, BF16)],
        compiler_params=_params("arbitrary"),
        name="merge",
    )(hs_real, hs_tail, yb_real, yb_tail, xc, og, gab, ng, sk, wa, wb, wo, w1, w2)


def _ffn_kernel(d_ref, xr_ref, xt_ref, g2_ref, w1_ref, w2_ref, or_ref, ot_ref):
    def body(x_ref, o_ref):
        h = x_ref[...] + d_ref[...]
        xn = (h * lax.rsqrt(jnp.mean(h * h, axis=-1, keepdims=True) + EPS) * g2_ref[...]).astype(BF16)
        z = jnp.maximum(_dot(xn, w1_ref[...]), 0.0)
        o_ref[...] = h + _dot((z * z).astype(BF16), w2_ref[...])

    _real_or_tail(body, xr_ref, xt_ref, or_ref, ot_ref)


def _ffn(delta, x2d, tailh, g2, w1, w2):
    return pl.pallas_call(
        _ffn_kernel,
        grid=(N_TILES,),
        in_specs=[_row_spec(D_MODEL), _real_spec(D_MODEL), _tail_spec(D_MODEL),
                  _const_spec(g2.shape), _const_spec(w1.shape), _const_spec(w2.shape)],
        out_specs=[_real_spec(D_MODEL), _tail_spec(D_MODEL)],
        out_shape=[jax.ShapeDtypeStruct((N_REAL, D_MODEL), F32),
                   jax.ShapeDtypeStruct((BATCH * TAIL, D_MODEL), F32)],
        compiler_params=_params("arbitrary"),
        name="ffn",
    )(delta, x2d, tailh, g2, w1, w2)


def kernel(x, meta_tokens, norm1_g, w_in, mlstm_conv_w, mlstm_conv_b, mlstm_wq, mlstm_wk, mlstm_gate_b, mlstm_norm_g, mlstm_skip, na_q_norm_g, na_k_norm_g, na_rpb, na_meta_bias, w_branch_a, w_branch_b, w_out, norm2_g, w_ff1, w_ff2):
    x2d = x.astype(F32).reshape(N_REAL, D_MODEL)
    tail = jnp.concatenate([jnp.zeros((NPAD, D_MODEL), F32), meta_tokens.astype(F32)], axis=0)
    tailh = jnp.tile(tail, (BATCH, 1))

    w_all = w_in.astype(BF16)
    pad_gates = lambda a: jnp.pad(
        jnp.swapaxes(a.reshape(a.shape[0], 4, M_HEADS), 1, 2),
        ((0, 0), (0, 0), (0, GATE_PAD - 4))).reshape(a.shape[0], M_HEADS * GATE_PAD)
    wg = jnp.pad(pad_gates(w_all[:, OFF_MG:OFF_Q]), ((0, 0), (0, LANES - M_HEADS * GATE_PAD)))
    gb = jnp.pad(pad_gates(mlstm_gate_b.astype(F32).reshape(1, 4 * M_HEADS)),
                 ((0, 0), (0, LANES - M_HEADS * GATE_PAD)))
    bd = jnp.kron(jnp.eye(NORM_GROUP // NA_DH, dtype=F32),
                  jnp.full((NA_DH, NA_DH), 1.0 / NA_DH, F32)).astype(BF16)
    qg = jnp.tile(na_q_norm_g.astype(F32), NA_HEADS)[None, :] * (NA_DH ** -0.5 * LOG2E)
    kg = jnp.tile(na_k_norm_g.astype(F32), NA_HEADS)[None, :]

    xb = x.astype(F32)
    xedge = jnp.concatenate([xb[:, SEQ - 1, :], xb[:, 0, :]], axis=0)
    wqk = jnp.concatenate([mlstm_wq, mlstm_wk], axis=-1).astype(BF16)
    vm, og, grow, qn, kn, vn, gab, xc, qm, km = _inproj(
        x2d, tailh, xedge, norm1_g.astype(F32)[None, :], w_all, wg, gb, qg, kg, bd,
        mlstm_conv_w.astype(F32).reshape(3, M_INNER), mlstm_conv_b.astype(F32)[None, :], wqk)

    hs_real, hs_tail = _mlstm(qm, km, vm, grow)
    yb_real, yb_tail = _natten(qn, kn, vn, _natten_bias(na_rpb, na_meta_bias))

    delta, w1, w2 = _merge(hs_real, hs_tail, yb_real, yb_tail, xc, og, gab,
                           mlstm_norm_g.astype(F32).reshape(1, M_INNER), mlstm_skip.astype(F32)[None, :],
                           w_branch_a.astype(BF16), w_branch_b.astype(BF16), w_out.astype(BF16),
                           w_ff1.astype(F32), w_ff2.astype(F32))
    out_real, _ = _ffn(delta, x2d, tailh, norm2_g.astype(F32)[None, :], w1, w2)
    return out_real.reshape(BATCH, SEQ, D_MODEL)
```

```python
import jax
import jax.numpy as jnp
from jax import lax
from jax.experimental import pallas as pl
from jax.experimental.pallas import tpu as pltpu

D_MODEL = 1024
BATCH = 4
SEQ = 4096
N_META = 16
GRID_W = 64
ROWS = SEQ // GRID_W
M_HEADS = 4
M_DV = 256
M_DK = 128
M_INNER = M_HEADS * M_DV
NA_HEADS = 8
NA_DH = 64
NA_INNER = NA_HEADS * NA_DH
NA_WIN_H = 8
NA_WIN_W = 16
D_FF = 4 * D_MODEL
EPS = 1e-6
NEG_LOG_GATE = -1e9
MASKED = -1e30
LOG2E = 1.4426950408889634

TAIL = 128
NPAD = TAIL - N_META
N_REAL = BATCH * SEQ
NT = N_REAL + BATCH * TAIL
TM = 512
N_TILES = NT // TM
N_REAL_TILES = N_REAL // TM

LANES = 128
BF16_ROWS = 16
VMEM_LIMIT = 56 * 1024 * 1024

F32 = jnp.float32
BF16 = jnp.bfloat16


def _dot(a, b):
    return jnp.dot(a, b, preferred_element_type=F32)


def _dot_nt(a, b):
    return lax.dot_general(a, b, (((1,), (1,)), ((), ())), preferred_element_type=F32)


def _sigmoid(z):
    return 0.5 * jnp.tanh(0.5 * z) + 0.5


def _log_sigmoid(z):
    return jnp.minimum(z, 0.0) - jnp.log1p(jnp.exp(-jnp.abs(z)))


def _const_spec(shape):
    nd = len(shape)
    return pl.BlockSpec(shape, lambda *_: (0,) * nd, pipeline_mode=pl.Buffered(1))


def _params(*sem):
    return pltpu.CompilerParams(dimension_semantics=sem, vmem_limit_bytes=VMEM_LIMIT)


GATE_PAD = 8
OFF_MX = 0
OFF_MO = OFF_MX + M_INNER
OFF_MG = OFF_MO + M_INNER
OFF_Q = OFF_MG + 4 * M_HEADS
OFF_G = OFF_Q + 3 * NA_INNER
D_IN_PROJ = OFF_G + 2 * D_MODEL


HALO = 8
HALO_PAD = BF16_ROWS
TILES_PER_SEQ = SEQ // TM
NORM_GROUP = 256


def _inproj_kernel(x_ref, tail_ref, xprev_ref, xnext_ref, xedge_ref, g1_ref, w_ref, wg_ref,
                   gb_ref, qg_ref, kg_ref, bd_ref, cw_ref, cb_ref, wqk_ref,
                   vm_ref, og_ref, grow_ref, q_ref, k_ref, v_ref, gab_ref, xc_ref, qm_ref, km_ref,
                   wqkv_s, wgab_s):
    i = pl.program_id(0)

    @pl.when(i == 0)
    def _():
        wqkv_s[...] = w_ref[:, OFF_Q:OFF_G]
        wgab_s[...] = w_ref[:, OFF_G:D_IN_PROJ]

    def norm1(h):
        ms = jnp.mean(h * h, axis=-1, keepdims=True)
        return (h * lax.rsqrt(ms + EPS) * g1_ref[...]).astype(BF16)

    def body(h_ref, is_tail):
        xn = norm1(h_ref[...])
        if is_tail:
            halo = jnp.concatenate([xedge_ref[...], jnp.zeros((HALO_PAD - 2 * BATCH, D_MODEL), F32)], axis=0)
        else:
            first = i % TILES_PER_SEQ == 0
            last = i % TILES_PER_SEQ == TILES_PER_SEQ - 1
            prev = jnp.where(first, tail_ref[TAIL - 1:TAIL, :], xprev_ref[HALO - 1:HALO, :])
            nxt = jnp.where(last, tail_ref[0:1, :], xnext_ref[0:1, :])
            halo = jnp.concatenate([prev, nxt, jnp.zeros((HALO_PAD - 2, D_MODEL), F32)], axis=0)
        xm_ext = _dot(jnp.concatenate([xn, norm1(halo)], axis=0), w_ref[:, OFF_MX:OFF_MO])
        og = _dot(xn, w_ref[:, OFF_MO:OFF_MG])
        gates = _dot(xn, wg_ref[...]) + gb_ref[...]
        gab = _dot(xn, wgab_s[...])
        qkv = _dot(xn, wqkv_s[...])
        xm = xm_ext[:TM]
        rows = lax.broadcasted_iota(jnp.int32, (TM, M_INNER), 0)
        x_m1 = pltpu.roll(xm, 1, axis=0)
        x_p1 = pltpu.roll(xm, TM - 1, axis=0)
        if is_tail:
            for b in range(BATCH):
                x_m1 = jnp.where(rows == b * TAIL, xm_ext[TM + b:TM + b + 1], x_m1)
                x_p1 = jnp.where(rows == b * TAIL + TAIL - 1, xm_ext[TM + BATCH + b:TM + BATCH + b + 1], x_p1)
        else:
            x_m1 = jnp.where(rows == 0, xm_ext[TM:TM + 1], x_m1)
            x_p1 = jnp.where(rows == TM - 1, xm_ext[TM + 1:TM + 2], x_p1)
        vm_ref[...] = xm.astype(BF16)
        z = cw_ref[0:1, :] * x_m1 + cw_ref[1:2, :] * xm + cw_ref[2:3, :] * x_p1 + cb_ref[...]
        xc = z * _sigmoid(z)
        xc_ref[...] = xc.astype(BF16)
        for hd in range(M_HEADS):
            qk = _dot(xc[:, hd * M_DV:(hd + 1) * M_DV].astype(BF16), wqk_ref[hd])
            qm_ref[:, hd * M_DK:(hd + 1) * M_DK] = qk[:, :M_DK].astype(BF16)
            km_ref[:, hd * M_DK:(hd + 1) * M_DK] = (qk[:, M_DK:] * (M_DK ** -0.5)).astype(BF16)
        og_ref[...] = og.astype(BF16)
        grow_ref[...] = jnp.transpose(gates)[:M_HEADS * GATE_PAD, :]
        gab_ref[...] = gab.astype(BF16)
        uq = qkv[:, :NA_INNER]
        uk = qkv[:, NA_INNER:2 * NA_INNER]
        def head_mean(sq):
            return jnp.concatenate([_dot(sq[:, j:j + NORM_GROUP].astype(BF16), bd_ref[...])
                                    for j in range(0, NA_INNER, NORM_GROUP)], axis=1)

        msq = head_mean(uq * uq)
        msk = head_mean(uk * uk)
        q_ref[...] = (uq * lax.rsqrt(msq + EPS) * qg_ref[...]).astype(BF16)
        k_ref[...] = (uk * lax.rsqrt(msk + EPS) * kg_ref[...]).astype(BF16)
        v_ref[...] = qkv[:, 2 * NA_INNER:].astype(BF16)

    pl.when(i < N_REAL_TILES)(lambda: body(x_ref, False))
    pl.when(i == N_REAL_TILES)(lambda: body(tail_ref, True))


def _real_or_tail(body, *ref_pairs):
    i = pl.program_id(0)
    pl.when(i < N_REAL_TILES)(lambda: body(*ref_pairs[0::2]))
    pl.when(i == N_REAL_TILES)(lambda: body(*ref_pairs[1::2]))


def _real_spec(width):
    return pl.BlockSpec((TM, width), lambda i: (jnp.minimum(i, N_REAL_TILES - 1), 0))


def _tail_spec(width):
    return pl.BlockSpec((TM, width), lambda i: (0, 0))


def _row_spec(width):
    return pl.BlockSpec((TM, width), lambda i: (i, 0))


def _slab_spec(shape):
    return pl.BlockSpec((shape[0] // N_REAL_TILES, shape[1]), lambda i: (jnp.minimum(i, N_REAL_TILES - 1), 0))


def _inproj(x2d, tailh, xedge, g1, w_all, wg, gb, qg, kg, bd, cw, cb, wqk):
    halo_blocks = N_REAL // HALO
    per_tile = TM // HALO
    consts = (g1, w_all, wg, gb, qg, kg, bd, cw, cb, wqk)
    return pl.pallas_call(
        _inproj_kernel,
        grid=(N_TILES,),
        in_specs=[_real_spec(D_MODEL), _tail_spec(D_MODEL),
                  pl.BlockSpec((HALO, D_MODEL), lambda i: (jnp.clip(i * per_tile - 1, 0, halo_blocks - 1), 0)),
                  pl.BlockSpec((HALO, D_MODEL), lambda i: (jnp.clip((i + 1) * per_tile, 0, halo_blocks - 1), 0)),
                  _const_spec(xedge.shape)]
                 + [_const_spec(a.shape) for a in consts],
        out_specs=[_row_spec(M_INNER), _row_spec(M_INNER),
                   pl.BlockSpec((M_HEADS * GATE_PAD, TM), lambda i: (0, i)),
                   _row_spec(NA_INNER), _row_spec(NA_INNER), _row_spec(NA_INNER), _row_spec(2 * D_MODEL),
                   _row_spec(M_INNER), _row_spec(M_HEADS * M_DK), _row_spec(M_HEADS * M_DK)],
        out_shape=[jax.ShapeDtypeStruct((NT, M_INNER), BF16),
                   jax.ShapeDtypeStruct((NT, M_INNER), BF16),
                   jax.ShapeDtypeStruct((M_HEADS * GATE_PAD, NT), F32),
                   jax.ShapeDtypeStruct((NT, NA_INNER), BF16),
                   jax.ShapeDtypeStruct((NT, NA_INNER), BF16),
                   jax.ShapeDtypeStruct((NT, NA_INNER), BF16),
                   jax.ShapeDtypeStruct((NT, 2 * D_MODEL), BF16),
                   jax.ShapeDtypeStruct((NT, M_INNER), BF16),
                   jax.ShapeDtypeStruct((NT, M_HEADS * M_DK), BF16),
                   jax.ShapeDtypeStruct((NT, M_HEADS * M_DK), BF16)],
        scratch_shapes=[pltpu.VMEM((D_MODEL, OFF_G - OFF_Q), BF16),
                        pltpu.VMEM((D_MODEL, D_IN_PROJ - OFF_G), BF16)],
        compiler_params=_params("arbitrary"),
        name="inproj",
    )(x2d, tailh, x2d, x2d, xedge, *consts)


MT = 256
N_MCHUNK = SEQ // MT
STATE_ROWS = M_DV + BF16_ROWS
ROWS_PER_DIR = 8
STATE_GROUP = N_MCHUNK + 1


def _split3(x):
    hi = x.astype(BF16)
    r1 = x - hi.astype(F32)
    mid = r1.astype(BF16)
    lo = (r1 - mid.astype(F32)).astype(BF16)
    return hi, mid, lo


def _mlstm_local_start(q, k, v, g8, is_tail):
    t = q.shape[0]
    li = [g8[0:1], g8[2:3]]
    lf = [_log_sigmoid(g8[1:2]), _log_sigmoid(g8[3:4])]
    if is_tail:
        pad = lax.broadcasted_iota(jnp.int32, (1, t), 1) < NPAD
        li = [jnp.where(pad, NEG_LOG_GATE, x) for x in li]
        lf = [jnp.where(pad, 0.0, x) for x in lf]
    si = lax.broadcasted_iota(jnp.int32, (t, t), 0)
    ti = lax.broadcasted_iota(jnp.int32, (t, t), 1)
    hi, mid, lo = _split3(jnp.concatenate(lf, axis=0))
    lhs = jnp.concatenate([hi, mid, lo, jnp.zeros((BF16_ROWS - 6, t), BF16)], axis=0)
    pref = _dot(lhs, (si <= ti).astype(BF16))
    pre_f = pref[0:1] + pref[2:3] + pref[4:5]
    pre_b = pref[1:2] + pref[3:4] + pref[5:6]
    b_end = [pre_f[:, t - 1:t], pre_b[:, t - 1:t]]
    b = [pre_f, b_end[1] - pre_b + lf[1]]
    s_t = _dot_nt(k, q)
    v_tb = jnp.transpose(v.astype(BF16))
    return li, b, b_end, s_t, v_tb, k


def _mlstm_local_finish(li, b, b_end, s_t, v_tb, k):
    t = s_t.shape[0]
    si = lax.broadcasted_iota(jnp.int32, (t, t), 0)
    ti = lax.broadcasted_iota(jnp.int32, (t, t), 1)
    kf = k.astype(F32)
    v_ext = jnp.concatenate([v_tb, jnp.ones((STATE_ROWS - M_DV, t), BF16)], axis=0)
    out = []
    for dirn in range(2):
        g2 = (li[dirn] - b[dirn]) * LOG2E
        g_col = jnp.transpose(jnp.broadcast_to(g2, (LANES, t)))
        g_st = jnp.where((si <= ti) if dirn == 0 else (si >= ti),
                         jnp.concatenate([g_col] * (t // LANES), axis=1), MASKED)
        g_max = jnp.max(g_st, axis=0, keepdims=True)
        p_t = s_t * jnp.exp2(g_st - g_max)
        den = jnp.sum(p_t, axis=0, keepdims=True)
        nl_t = _dot(v_tb, p_t.astype(BF16))
        m_loc = b[dirn] + g_max * (1.0 / LOG2E)
        top = jnp.max(g2, axis=1, keepdims=True)
        kw = (kf * jnp.exp2(g_col - top)).astype(BF16)
        u = _dot(v_ext, kw)
        a_max = b_end[dirn] + top * (1.0 / LOG2E)
        rows = jnp.concatenate([m_loc, den, b[dirn], jnp.broadcast_to(a_max, (1, t)),
                                jnp.broadcast_to(b_end[dirn], (1, t)),
                                jnp.zeros((ROWS_PER_DIR - 5, t), F32)], axis=0)
        out.append((nl_t, u, rows))
    return out


def _mlstm_state_start(dirn, q, u, rows, s_ref, m_ref):
    a_max, b_end = rows[3:4, 0:1], rows[4:5, 0:1]
    m_prev = m_ref[dirn]
    s_old = s_ref[dirn]
    inter = _dot_nt(s_old.astype(BF16), q)
    m_new = jnp.maximum(b_end + m_prev, a_max)
    s_ref[dirn] = jnp.exp(b_end + m_prev - m_new) * s_old + jnp.exp(a_max - m_new) * u
    m_ref[dirn] = m_new
    return inter, m_prev


def _mlstm_state_finish(inter, m_prev, nl_t, rows):
    m_loc, den_loc, b = rows[0:1], rows[1:2], rows[2:3]
    m_inter = b + m_prev
    m_t = jnp.maximum(m_inter, m_loc)
    w_inter = jnp.exp(m_inter - m_t)
    w_loc = jnp.exp(m_loc - m_t)
    den = w_inter * inter[M_DV:M_DV + 1] + w_loc * den_loc
    scale = 1.0 / jnp.maximum(jnp.abs(den), jnp.exp(-m_t))
    return (w_inter * scale) * inter[:M_DV] + (w_loc * scale) * nl_t


def _mlstm_kernel(q_ref, qt_ref, k_ref, kt_ref, v_ref, vt_ref, g_ref, gt_ref, o_ref, ot_ref,
                  nl_s, u_s, rows_s, ht_s, s_ref, m_ref):
    s_ref[...] = jnp.zeros_like(s_ref)
    m_ref[...] = jnp.zeros_like(m_ref)
    tail = N_MCHUNK

    def cols(c):
        return pl.ds(SEQ, TAIL) if c == tail else pl.ds(c * MT, MT)

    def rows(ref, ref_tail, c):
        return ref_tail[...] if c == tail else ref[c * MT:(c + 1) * MT, :]

    def gates(c):
        return gt_ref[...] if c == tail else g_ref[:, c * MT:(c + 1) * MT]

    order = [tail] + list(range(N_MCHUNK))
    started = [_mlstm_local_start(rows(q_ref, qt_ref, c), rows(k_ref, kt_ref, c), rows(v_ref, vt_ref, c),
                                  gates(c), c == tail) for c in order]
    for c, st in zip(order, started):
        for dirn, (nl_t, u, row_vecs) in enumerate(_mlstm_local_finish(*st)):
            nl_s[dirn, :, cols(c)] = nl_t
            u_s[dirn, c] = u
            rows_s[dirn, :, cols(c)] = row_vecs

    fwd = [(0, c) for c in order]
    bwd = [(1, c) for c in reversed(order)]
    visits = [v for pair in zip(fwd, bwd) for v in pair]
    seen = set()
    for lo in range(0, len(visits), STATE_GROUP):
        group = visits[lo:lo + STATE_GROUP]
        started = [_mlstm_state_start(dirn, rows(q_ref, qt_ref, c), u_s[dirn, c], rows_s[dirn, :, cols(c)],
                                      s_ref, m_ref) for dirn, c in group]
        for (inter, m_prev), (dirn, c) in zip(started, group):
            h_t = _mlstm_state_finish(inter, m_prev, nl_s[dirn, :, cols(c)], rows_s[dirn, :, cols(c)])
            if c not in seen:
                seen.add(c)
                ht_s[:, cols(c)] = h_t
                continue
            out = jnp.transpose((ht_s[:, cols(c)] + h_t).astype(o_ref.dtype))
            if c == tail:
                ot_ref[...] = out
            else:
                o_ref[c * MT:(c + 1) * MT, :] = out


def _mlstm(q, k, xm, grow):
    tail_blk = N_REAL // TAIL
    real = lambda w: pl.BlockSpec((SEQ, w), lambda b, h: (b, h))
    tail = lambda w: pl.BlockSpec((TAIL, w), lambda b, h: (tail_blk + b, h))
    lp = SEQ + TAIL
    return pl.pallas_call(
        _mlstm_kernel,
        grid=(BATCH, M_HEADS),
        in_specs=[real(M_DK), tail(M_DK), real(M_DK), tail(M_DK), real(M_DV), tail(M_DV),
                  pl.BlockSpec((GATE_PAD, SEQ), lambda b, h: (h, b)),
                  pl.BlockSpec((GATE_PAD, TAIL), lambda b, h: (h, tail_blk + b))],
        out_specs=[pl.BlockSpec((SEQ, M_DV), lambda b, h: (b, h)),
                   pl.BlockSpec((TAIL, M_DV), lambda b, h: (b, h))],
        out_shape=[jax.ShapeDtypeStruct((N_REAL, M_INNER), BF16),
                   jax.ShapeDtypeStruct((BATCH * TAIL, M_INNER), BF16)],
        scratch_shapes=[pltpu.VMEM((2, M_DV, lp), F32),
                        pltpu.VMEM((2, N_MCHUNK + 1, STATE_ROWS, M_DK), F32),
                        pltpu.VMEM((2, ROWS_PER_DIR, lp), F32),
                        pltpu.VMEM((M_DV, lp), F32),
                        pltpu.VMEM((2, STATE_ROWS, M_DK), F32),
                        pltpu.VMEM((2, 1, 1), F32)],
        compiler_params=_params("parallel", "parallel"),
        name="mlstm",
    )(q, q, k, k, xm, xm, grow, grow)


PAIR = 2 * NA_DH


def _natten_kernel(q_ref, qt_ref, k_ref, kt_ref, v_ref, vt_ref, loc_ref, met_ref, o_ref, ot_ref):
    lane = lax.broadcasted_iota(jnp.int32, (1, PAIR), 1)
    first = lane < NA_DH
    k_tail = kt_ref[...]
    v_tail = vt_ref[...]

    def attend(q, keys, vals, bias_t):
        n = q.shape[0]
        zero = jnp.zeros_like(q)
        qs = jnp.concatenate([jnp.where(first, q, zero), jnp.where(first, zero, q)], axis=0)
        s = _dot_nt(keys, qs) + bias_t
        e = jnp.exp2(s - jnp.max(s, axis=0, keepdims=True))
        o = _dot(jnp.transpose(e.astype(BF16)), vals)
        inv = jnp.broadcast_to(1.0 / jnp.sum(e, axis=0, keepdims=True), (PAIR, 2 * n))
        o = o * jnp.transpose(inv)
        return jnp.where(first, o[:n], o[n:])

    met = met_ref[...]
    for r in range(ROWS):
        r0 = min(max(r - NA_WIN_H // 2, 0), ROWS - NA_WIN_H)
        qrows = pl.ds(r * GRID_W, GRID_W)
        krows = pl.ds(r0 * GRID_W, NA_WIN_H * GRID_W)
        keys = jnp.concatenate([k_ref[krows, :], k_tail], axis=0)
        vals = jnp.concatenate([v_ref[krows, :], v_tail], axis=0)
        top = r0 - r + NA_WIN_H - 1
        loc = loc_ref[top:top + NA_WIN_H].reshape(NA_WIN_H * GRID_W, PAIR)
        o = attend(q_ref[qrows, :], keys, vals, jnp.concatenate([loc, met], axis=0))
        o_ref[qrows, :] = o.astype(o_ref.dtype)
    tb = met
    tail_bias = jnp.concatenate([jnp.broadcast_to(tb[:, 0:1], (TAIL, TAIL)),
                                 jnp.broadcast_to(tb[:, NA_DH:NA_DH + 1], (TAIL, TAIL))], axis=1)
    ot_ref[...] = attend(qt_ref[...], k_tail, v_tail, tail_bias).astype(ot_ref.dtype)


def _natten(qn, kn, vn, loc, met):
    tail_blk = N_REAL // TAIL
    real = pl.BlockSpec((SEQ, PAIR), lambda b, p: (b, p))
    tail = pl.BlockSpec((TAIL, PAIR), lambda b, p: (tail_blk + b, p))
    return pl.pallas_call(
        _natten_kernel,
        grid=(BATCH, NA_HEADS // 2),
        in_specs=[real, tail, real, tail, real, tail,
                  pl.BlockSpec((None, 2 * NA_WIN_H - 1, GRID_W, PAIR), lambda b, p: (p, 0, 0, 0)),
                  pl.BlockSpec((None, TAIL, PAIR), lambda b, p: (p, 0, 0))],
        out_specs=[pl.BlockSpec((SEQ, PAIR), lambda b, p: (b, p)),
                   pl.BlockSpec((TAIL, PAIR), lambda b, p: (b, p))],
        out_shape=[jax.ShapeDtypeStruct((N_REAL, NA_INNER), BF16),
                   jax.ShapeDtypeStruct((BATCH * TAIL, NA_INNER), BF16)],
        compiler_params=_params("parallel", "parallel"),
        name="natten",
    )(qn, qn, kn, kn, vn, vn, loc, met)


def _natten_bias(rpb, meta_bias):
    qc = jnp.arange(GRID_W)
    kc = jnp.arange(GRID_W)
    win0 = jnp.clip(qc - NA_WIN_W // 2, 0, GRID_W - NA_WIN_W)
    ok = (kc[:, None] >= win0[None, :]) & (kc[:, None] < win0[None, :] + NA_WIN_W)
    dc = jnp.clip(kc[:, None] - qc[None, :], -(NA_WIN_W - 1), NA_WIN_W - 1) + NA_WIN_W - 1
    onehot = (dc[None] == jnp.arange(2 * NA_WIN_W - 1)[:, None, None]).astype(F32)
    t1 = jnp.einsum('hdj,jkq->dkhq', rpb.astype(F32), onehot, precision=lax.Precision.HIGHEST)
    t1 = jnp.where(ok[None, :, None, :], t1, MASKED) * LOG2E
    t1 = jnp.transpose(t1.reshape(2 * NA_WIN_H - 1, GRID_W, NA_HEADS // 2, PAIR), (2, 0, 1, 3))
    met = jnp.concatenate([jnp.full((NA_HEADS, NPAD), MASKED, F32), meta_bias.astype(F32)], axis=1) * LOG2E
    met = jnp.broadcast_to(met.reshape(NA_HEADS // 2, 2, TAIL, 1), (NA_HEADS // 2, 2, TAIL, NA_DH))
    met = jnp.transpose(met, (0, 2, 1, 3)).reshape(NA_HEADS // 2, TAIL, PAIR)
    return t1, met


def _merge_kernel(hsr_ref, hst_ref, ybr_ref, ybt_ref, xc_ref, og_ref, gab_ref,
                  ng_ref, sk_ref, wa_ref, wb_ref, wo_ref, w1_ref, w2_ref, o_ref, w1o_ref, w2o_ref):
    w1o_ref[...] = w1_ref[...].astype(BF16)
    w2o_ref[...] = w2_ref[...].astype(BF16)

    def body(hs_ref, yb_ref):
        hs = hs_ref[...].astype(F32)
        parts = []
        for hd in range(M_HEADS):
            sl = hs[:, hd * M_DV:(hd + 1) * M_DV]
            parts.append(sl * lax.rsqrt(jnp.mean(sl * sl, axis=-1, keepdims=True) + EPS))
        hn = jnp.concatenate(parts, axis=1) * ng_ref[...]
        y_a = _sigmoid(og_ref[...].astype(F32)) * (hn + sk_ref[...] * xc_ref[...].astype(F32))
        gab = gab_ref[...].astype(F32)
        mix = (_sigmoid(gab[:, :D_MODEL]) * _dot(y_a.astype(BF16), wa_ref[...])
               + _sigmoid(gab[:, D_MODEL:]) * _dot(yb_ref[...], wb_ref[...]))
        o_ref[...] = _dot(mix.astype(BF16), wo_ref[...])

    _real_or_tail(body, hsr_ref, hst_ref, ybr_ref, ybt_ref)


def _merge(hs_real, hs_tail, yb_real, yb_tail, xc, og, gab, ng, sk, wa, wb, wo, w1, w2):
    return pl.pallas_call(
        _merge_kernel,
        grid=(N_TILES,),
        in_specs=[_real_spec(M_INNER), _tail_spec(M_INNER), _real_spec(NA_INNER), _tail_spec(NA_INNER),
                  _row_spec(M_INNER), _row_spec(M_INNER), _row_spec(2 * D_MODEL)]
                 + [_const_spec(a.shape) for a in (ng, sk, wa, wb, wo)]
                 + [_slab_spec(w1.shape), _slab_spec(w2.shape)],
        out_specs=[_row_spec(D_MODEL), _slab_spec(w1.shape), _slab_spec(w2.shape)],
        out_shape=[jax.ShapeDtypeStruct((NT, D_MODEL), F32),
                   jax.ShapeDtypeStruct(w1.shape, BF16), jax.ShapeDtypeStruct(w2.shape, BF16)],
        compiler_params=_params("arbitrary"),
        name="merge",
    )(hs_real, hs_tail, yb_real, yb_tail, xc, og, gab, ng, sk, wa, wb, wo, w1, w2)


def _ffn_kernel(d_ref, xr_ref, xt_ref, g2_ref, w1_ref, w2_ref, or_ref, ot_ref):
    def body(x_ref, o_ref):
        h = x_ref[...] + d_ref[...]
        xn = (h * lax.rsqrt(jnp.mean(h * h, axis=-1, keepdims=True) + EPS) * g2_ref[...]).astype(BF16)
        z = jnp.maximum(_dot(xn, w1_ref[...]), 0.0)
        o_ref[...] = h + _dot((z * z).astype(BF16), w2_ref[...])

    _real_or_tail(body, xr_ref, xt_ref, or_ref, ot_ref)


def _ffn(delta, x2d, tailh, g2, w1, w2):
    return pl.pallas_call(
        _ffn_kernel,
        grid=(N_TILES,),
        in_specs=[_row_spec(D_MODEL), _real_spec(D_MODEL), _tail_spec(D_MODEL),
                  _const_spec(g2.shape), _const_spec(w1.shape), _const_spec(w2.shape)],
        out_specs=[_real_spec(D_MODEL), _tail_spec(D_MODEL)],
        out_shape=[jax.ShapeDtypeStruct((N_REAL, D_MODEL), F32),
                   jax.ShapeDtypeStruct((BATCH * TAIL, D_MODEL), F32)],
        compiler_params=_params("arbitrary"),
        name="ffn",
    )(delta, x2d, tailh, g2, w1, w2)


def kernel(x, meta_tokens, norm1_g, w_in, mlstm_conv_w, mlstm_conv_b, mlstm_wq, mlstm_wk, mlstm_gate_b, mlstm_norm_g, mlstm_skip, na_q_norm_g, na_k_norm_g, na_rpb, na_meta_bias, w_branch_a, w_branch_b, w_out, norm2_g, w_ff1, w_ff2):
    x2d = x.astype(F32).reshape(N_REAL, D_MODEL)
    tail = jnp.concatenate([jnp.zeros((NPAD, D_MODEL), F32), meta_tokens.astype(F32)], axis=0)
    tailh = jnp.tile(tail, (BATCH, 1))

    w_all = w_in.astype(BF16)
    pad_gates = lambda a: jnp.pad(
        jnp.swapaxes(a.reshape(a.shape[0], 4, M_HEADS), 1, 2),
        ((0, 0), (0, 0), (0, GATE_PAD - 4))).reshape(a.shape[0], M_HEADS * GATE_PAD)
    wg = jnp.pad(pad_gates(w_all[:, OFF_MG:OFF_Q]), ((0, 0), (0, LANES - M_HEADS * GATE_PAD)))
    gb = jnp.pad(pad_gates(mlstm_gate_b.astype(F32).reshape(1, 4 * M_HEADS)),
                 ((0, 0), (0, LANES - M_HEADS * GATE_PAD)))
    bd = jnp.kron(jnp.eye(NORM_GROUP // NA_DH, dtype=F32),
                  jnp.full((NA_DH, NA_DH), 1.0 / NA_DH, F32)).astype(BF16)
    qg = jnp.tile(na_q_norm_g.astype(F32), NA_HEADS)[None, :] * (NA_DH ** -0.5 * LOG2E)
    kg = jnp.tile(na_k_norm_g.astype(F32), NA_HEADS)[None, :]

    xb = x.astype(F32)
    xedge = jnp.concatenate([xb[:, SEQ - 1, :], xb[:, 0, :]], axis=0)
    wqk = jnp.concatenate([mlstm_wq, mlstm_wk], axis=-1).astype(BF16)
    vm, og, grow, qn, kn, vn, gab, xc, qm, km = _inproj(
        x2d, tailh, xedge, norm1_g.astype(F32)[None, :], w_all, wg, gb, qg, kg, bd,
        mlstm_conv_w.astype(F32).reshape(3, M_INNER), mlstm_conv_b.astype(F32)[None, :], wqk)

    hs_real, hs_tail = _mlstm(qm, km, vm, grow)
    yb_real, yb_tail = _natten(qn, kn, vn, *_natten_bias(na_rpb, na_meta_bias))

    delta, w1, w2 = _merge(hs_real, hs_tail, yb_real, yb_tail, xc, og, gab,
                           mlstm_norm_g.astype(F32).reshape(1, M_INNER), mlstm_skip.astype(F32)[None, :],
                           w_branch_a.astype(BF16), w_branch_b.astype(BF16), w_out.astype(BF16),
                           w_ff1.astype(F32), w_ff2.astype(F32))
    out_real, _ = _ffn(delta, x2d, tailh, norm2_g.astype(F32)[None, :], w1, w2)
    return out_real.reshape(BATCH, SEQ, D_MODEL)
```

```python
import jax
import jax.numpy as jnp
from jax import lax
from jax.experimental import pallas as pl
from jax.experimental.pallas import tpu as pltpu

D_MODEL = 1024
BATCH = 4
SEQ = 4096
N_META = 16
GRID_W = 64
ROWS = SEQ // GRID_W
M_HEADS = 4
M_DV = 256
M_DK = 128
M_INNER = M_HEADS * M_DV
NA_HEADS = 8
NA_DH = 64
NA_INNER = NA_HEADS * NA_DH
NA_WIN_H = 8
NA_WIN_W = 16
D_FF = 4 * D_MODEL
EPS = 1e-6
NEG_LOG_GATE = -1e9
MASKED = -1e30
LOG2E = 1.4426950408889634

TAIL = 128
NPAD = TAIL - N_META
N_REAL = BATCH * SEQ
NT = N_REAL + BATCH * TAIL
TM = 512
N_TILES = NT // TM
N_REAL_TILES = N_REAL // TM

LANES = 128
BF16_ROWS = 16
VMEM_LIMIT = 56 * 1024 * 1024

F32 = jnp.float32
BF16 = jnp.bfloat16


def _dot(a, b):
    return jnp.dot(a, b, preferred_element_type=F32)


def _dot_nt(a, b):
    return lax.dot_general(a, b, (((1,), (1,)), ((), ())), preferred_element_type=F32)


def _sigmoid(z):
    return 0.5 * jnp.tanh(0.5 * z) + 0.5


def _log_sigmoid(z):
    return jnp.minimum(z, 0.0) - jnp.log1p(jnp.exp(-jnp.abs(z)))


def _const_spec(shape):
    nd = len(shape)
    return pl.BlockSpec(shape, lambda *_: (0,) * nd, pipeline_mode=pl.Buffered(1))


def _params(*sem):
    return pltpu.CompilerParams(dimension_semantics=sem, vmem_limit_bytes=VMEM_LIMIT)


GATE_PAD = 8
OFF_MX = 0
OFF_MO = OFF_MX + M_INNER
OFF_MG = OFF_MO + M_INNER
OFF_Q = OFF_MG + 4 * M_HEADS
OFF_G = OFF_Q + 3 * NA_INNER
D_IN_PROJ = OFF_G + 2 * D_MODEL


HALO = 8
HALO_PAD = BF16_ROWS
TILES_PER_SEQ = SEQ // TM
WCAST_ROWS = 128
NORM_GROUP = 256


def _wcast_kernel(w_ref, wxo_ref, wqkv_ref, wgab_ref):
    wxo_ref[...] = w_ref[:, OFF_MX:OFF_MG].astype(BF16)
    wqkv_ref[...] = w_ref[:, OFF_Q:OFF_G].astype(BF16)
    wgab_ref[...] = w_ref[:, OFF_G:D_IN_PROJ].astype(BF16)


def _wcast(w_in):
    slab = lambda width: pl.BlockSpec((WCAST_ROWS, width), lambda i: (i, 0))
    widths = (OFF_MG - OFF_MX, OFF_G - OFF_Q, D_IN_PROJ - OFF_G)
    return pl.pallas_call(
        _wcast_kernel,
        grid=(D_MODEL // WCAST_ROWS,),
        in_specs=[slab(D_IN_PROJ)],
        out_specs=[slab(w) for w in widths],
        out_shape=[jax.ShapeDtypeStruct((D_MODEL, w), BF16) for w in widths],
        compiler_params=_params("parallel"),
        name="wcast",
    )(w_in)


def _inproj_kernel(x_ref, tail_ref, xprev_ref, xnext_ref, xedge_ref, g1_ref, wxo_ref, wg_ref, wqkv_ref, wgab_ref,
                   gb_ref, qg_ref, kg_ref, bd_ref, cw_ref, cb_ref, wqk_ref,
                   vm_ref, og_ref, grow_ref, q_ref, k_ref, v_ref, gab_ref, xc_ref, qm_ref, km_ref):
    i = pl.program_id(0)

    def norm1(h):
        ms = jnp.mean(h * h, axis=-1, keepdims=True)
        return (h * lax.rsqrt(ms + EPS) * g1_ref[...]).astype(BF16)

    def body(h_ref, is_tail):
        xn = norm1(h_ref[...])
        if is_tail:
            halo = jnp.concatenate([xedge_ref[...], jnp.zeros((HALO_PAD - 2 * BATCH, D_MODEL), F32)], axis=0)
        else:
            first = i % TILES_PER_SEQ == 0
            last = i % TILES_PER_SEQ == TILES_PER_SEQ - 1
            prev = jnp.where(first, tail_ref[TAIL - 1:TAIL, :], xprev_ref[HALO - 1:HALO, :])
            nxt = jnp.where(last, tail_ref[0:1, :], xnext_ref[0:1, :])
            halo = jnp.concatenate([prev, nxt, jnp.zeros((HALO_PAD - 2, D_MODEL), F32)], axis=0)
        xm_ext = _dot(jnp.concatenate([xn, norm1(halo)], axis=0), wxo_ref[:, OFF_MX:OFF_MO])
        og = _dot(xn, wxo_ref[:, OFF_MO:OFF_MG])
        gates = _dot(xn, wg_ref[...]) + gb_ref[...]
        gab = _dot(xn, wgab_ref[...])
        qkv = _dot(xn, wqkv_ref[...])
        xm = xm_ext[:TM]
        rows = lax.broadcasted_iota(jnp.int32, (TM, M_INNER), 0)
        x_m1 = pltpu.roll(xm, 1, axis=0)
        x_p1 = pltpu.roll(xm, TM - 1, axis=0)
        if is_tail:
            for b in range(BATCH):
                x_m1 = jnp.where(rows == b * TAIL, xm_ext[TM + b:TM + b + 1], x_m1)
                x_p1 = jnp.where(rows == b * TAIL + TAIL - 1, xm_ext[TM + BATCH + b:TM + BATCH + b + 1], x_p1)
        else:
            x_m1 = jnp.where(rows == 0, xm_ext[TM:TM + 1], x_m1)
            x_p1 = jnp.where(rows == TM - 1, xm_ext[TM + 1:TM + 2], x_p1)
        vm_ref[...] = xm.astype(BF16)
        z = cw_ref[0:1, :] * x_m1 + cw_ref[1:2, :] * xm + cw_ref[2:3, :] * x_p1 + cb_ref[...]
        xc = z * _sigmoid(z)
        xc_ref[...] = xc.astype(BF16)
        for hd in range(M_HEADS):
            qk = _dot(xc[:, hd * M_DV:(hd + 1) * M_DV].astype(BF16), wqk_ref[hd])
            qm_ref[:, hd * M_DK:(hd + 1) * M_DK] = qk[:, :M_DK].astype(BF16)
            km_ref[:, hd * M_DK:(hd + 1) * M_DK] = (qk[:, M_DK:] * (M_DK ** -0.5)).astype(BF16)
        og_ref[...] = og.astype(BF16)
        grow_ref[...] = jnp.transpose(gates)[:M_HEADS * GATE_PAD, :]
        gab_ref[...] = gab.astype(BF16)
        uq = qkv[:, :NA_INNER]
        uk = qkv[:, NA_INNER:2 * NA_INNER]
        def head_mean(sq):
            return jnp.concatenate([_dot(sq[:, j:j + NORM_GROUP].astype(BF16), bd_ref[...])
                                    for j in range(0, NA_INNER, NORM_GROUP)], axis=1)

        msq = head_mean(uq * uq)
        msk = head_mean(uk * uk)
        q_ref[...] = (uq * lax.rsqrt(msq + EPS) * qg_ref[...]).astype(BF16)
        k_ref[...] = (uk * lax.rsqrt(msk + EPS) * kg_ref[...]).astype(BF16)
        v_ref[...] = qkv[:, 2 * NA_INNER:].astype(BF16)

    pl.when(i < N_REAL_TILES)(lambda: body(x_ref, False))
    pl.when(i == N_REAL_TILES)(lambda: body(tail_ref, True))


def _real_or_tail(body, *ref_pairs):
    i = pl.program_id(0)
    pl.when(i < N_REAL_TILES)(lambda: body(*ref_pairs[0::2]))
    pl.when(i == N_REAL_TILES)(lambda: body(*ref_pairs[1::2]))


def _real_spec(width):
    return pl.BlockSpec((TM, width), lambda i: (jnp.minimum(i, N_REAL_TILES - 1), 0))


def _tail_spec(width):
    return pl.BlockSpec((TM, width), lambda i: (0, 0))


def _row_spec(width):
    return pl.BlockSpec((TM, width), lambda i: (i, 0))


def _slab_spec(shape):
    return pl.BlockSpec((shape[0] // N_REAL_TILES, shape[1]), lambda i: (jnp.minimum(i, N_REAL_TILES - 1), 0))


def _inproj(x2d, tailh, xedge, g1, wxo, wg, wqkv, wgab, gb, qg, kg, bd, cw, cb, wqk):
    halo_blocks = N_REAL // HALO
    per_tile = TM // HALO
    consts = (g1, wxo, wg, wqkv, wgab, gb, qg, kg, bd, cw, cb, wqk)
    return pl.pallas_call(
        _inproj_kernel,
        grid=(N_TILES,),
        in_specs=[_real_spec(D_MODEL), _tail_spec(D_MODEL),
                  pl.BlockSpec((HALO, D_MODEL), lambda i: (jnp.clip(i * per_tile - 1, 0, halo_blocks - 1), 0)),
                  pl.BlockSpec((HALO, D_MODEL), lambda i: (jnp.clip((i + 1) * per_tile, 0, halo_blocks - 1), 0)),
                  _const_spec(xedge.shape)]
                 + [_const_spec(a.shape) for a in consts],
        out_specs=[_row_spec(M_INNER), _row_spec(M_INNER),
                   pl.BlockSpec((M_HEADS * GATE_PAD, TM), lambda i: (0, i)),
                   _row_spec(NA_INNER), _row_spec(NA_INNER), _row_spec(NA_INNER), _row_spec(2 * D_MODEL),
                   _row_spec(M_INNER), _row_spec(M_HEADS * M_DK), _row_spec(M_HEADS * M_DK)],
        out_shape=[jax.ShapeDtypeStruct((NT, M_INNER), BF16),
                   jax.ShapeDtypeStruct((NT, M_INNER), BF16),
                   jax.ShapeDtypeStruct((M_HEADS * GATE_PAD, NT), F32),
                   jax.ShapeDtypeStruct((NT, NA_INNER), BF16),
                   jax.ShapeDtypeStruct((NT, NA_INNER), BF16),
                   jax.ShapeDtypeStruct((NT, NA_INNER), BF16),
                   jax.ShapeDtypeStruct((NT, 2 * D_MODEL), BF16),
                   jax.ShapeDtypeStruct((NT, M_INNER), BF16),
                   jax.ShapeDtypeStruct((NT, M_HEADS * M_DK), BF16),
                   jax.ShapeDtypeStruct((NT, M_HEADS * M_DK), BF16)],
        compiler_params=_params("parallel"),
        name="inproj",
    )(x2d, tailh, x2d, x2d, xedge, *consts)


MT = 256
N_MCHUNK = SEQ // MT
STATE_ROWS = M_DV + BF16_ROWS
ROWS_PER_DIR = 8
STATE_GROUP = N_MCHUNK + 1


def _split3(x):
    hi = x.astype(BF16)
    r1 = x - hi.astype(F32)
    mid = r1.astype(BF16)
    lo = (r1 - mid.astype(F32)).astype(BF16)
    return hi, mid, lo


def _mlstm_local_start(q, k, v, g8, is_tail):
    t = q.shape[0]
    li = [g8[0:1], g8[2:3]]
    lf = [_log_sigmoid(g8[1:2]), _log_sigmoid(g8[3:4])]
    if is_tail:
        pad = lax.broadcasted_iota(jnp.int32, (1, t), 1) < NPAD
        li = [jnp.where(pad, NEG_LOG_GATE, x) for x in li]
        lf = [jnp.where(pad, 0.0, x) for x in lf]
    si = lax.broadcasted_iota(jnp.int32, (t, t), 0)
    ti = lax.broadcasted_iota(jnp.int32, (t, t), 1)
    hi, mid, lo = _split3(jnp.concatenate(lf, axis=0))
    lhs = jnp.concatenate([hi, mid, lo, jnp.zeros((BF16_ROWS - 6, t), BF16)], axis=0)
    pref = _dot(lhs, (si <= ti).astype(BF16))
    pre_f = pref[0:1] + pref[2:3] + pref[4:5]
    pre_b = pref[1:2] + pref[3:4] + pref[5:6]
    b_end = [pre_f[:, t - 1:t], pre_b[:, t - 1:t]]
    b = [pre_f, b_end[1] - pre_b + lf[1]]
    s_t = _dot_nt(k, q)
    v_tb = jnp.transpose(v.astype(BF16))
    return li, b, b_end, s_t, v_tb, k


def _mlstm_local_finish(li, b, b_end, s_t, v_tb, k):
    t = s_t.shape[0]
    si = lax.broadcasted_iota(jnp.int32, (t, t), 0)
    ti = lax.broadcasted_iota(jnp.int32, (t, t), 1)
    kf = k.astype(F32)
    v_ext = jnp.concatenate([v_tb, jnp.ones((STATE_ROWS - M_DV, t), BF16)], axis=0)
    out = []
    for dirn in range(2):
        g2 = (li[dirn] - b[dirn]) * LOG2E
        g_col = jnp.transpose(jnp.broadcast_to(g2, (LANES, t)))
        g_st = jnp.where((si <= ti) if dirn == 0 else (si >= ti),
                         jnp.concatenate([g_col] * (t // LANES), axis=1), MASKED)
        g_max = jnp.max(g_st, axis=0, keepdims=True)
        p_t = s_t * jnp.exp2(g_st - g_max)
        den = jnp.sum(p_t, axis=0, keepdims=True)
        nl_t = _dot(v_tb, p_t.astype(BF16))
        m_loc = b[dirn] + g_max * (1.0 / LOG2E)
        top = jnp.max(g2, axis=1, keepdims=True)
        kw = (kf * jnp.exp2(g_col - top)).astype(BF16)
        u = _dot(v_ext, kw)
        a_max = b_end[dirn] + top * (1.0 / LOG2E)
        rows = jnp.concatenate([m_loc, den, b[dirn], jnp.broadcast_to(a_max, (1, t)),
                                jnp.broadcast_to(b_end[dirn], (1, t)),
                                jnp.zeros((ROWS_PER_DIR - 5, t), F32)], axis=0)
        out.append((nl_t, u, rows))
    return out


def _mlstm_state_start(dirn, q, u, rows, s_ref, m_ref):
    a_max, b_end = rows[3:4, 0:1], rows[4:5, 0:1]
    m_prev = m_ref[dirn]
    s_old = s_ref[dirn]
    inter = _dot_nt(s_old.astype(BF16), q)
    m_new = jnp.maximum(b_end + m_prev, a_max)
    s_ref[dirn] = jnp.exp(b_end + m_prev - m_new) * s_old + jnp.exp(a_max - m_new) * u
    m_ref[dirn] = m_new
    return inter, m_prev


def _mlstm_state_finish(inter, m_prev, nl_t, rows):
    m_loc, den_loc, b = rows[0:1], rows[1:2], rows[2:3]
    m_inter = b + m_prev
    m_t = jnp.maximum(m_inter, m_loc)
    w_inter = jnp.exp(m_inter - m_t)
    w_loc = jnp.exp(m_loc - m_t)
    den = w_inter * inter[M_DV:M_DV + 1] + w_loc * den_loc
    scale = 1.0 / jnp.maximum(jnp.abs(den), jnp.exp(-m_t))
    return (w_inter * scale) * inter[:M_DV] + (w_loc * scale) * nl_t


def _mlstm_kernel(q_ref, qt_ref, k_ref, kt_ref, v_ref, vt_ref, g_ref, gt_ref, o_ref, ot_ref,
                  nl_s, u_s, rows_s, ht_s, s_ref, m_ref):
    s_ref[...] = jnp.zeros_like(s_ref)
    m_ref[...] = jnp.zeros_like(m_ref)
    tail = N_MCHUNK

    def cols(c):
        return pl.ds(SEQ, TAIL) if c == tail else pl.ds(c * MT, MT)

    def rows(ref, ref_tail, c):
        return ref_tail[...] if c == tail else ref[c * MT:(c + 1) * MT, :]

    def gates(c):
        return gt_ref[...] if c == tail else g_ref[:, c * MT:(c + 1) * MT]

    order = [tail] + list(range(N_MCHUNK))
    started = [_mlstm_local_start(rows(q_ref, qt_ref, c), rows(k_ref, kt_ref, c), rows(v_ref, vt_ref, c),
                                  gates(c), c == tail) for c in order]
    for c, st in zip(order, started):
        for dirn, (nl_t, u, row_vecs) in enumerate(_mlstm_local_finish(*st)):
            nl_s[dirn, :, cols(c)] = nl_t
            u_s[dirn, c] = u
            rows_s[dirn, :, cols(c)] = row_vecs

    fwd = [(0, c) for c in order]
    bwd = [(1, c) for c in reversed(order)]
    visits = [v for pair in zip(fwd, bwd) for v in pair]
    seen = set()
    for lo in range(0, len(visits), STATE_GROUP):
        group = visits[lo:lo + STATE_GROUP]
        started = [_mlstm_state_start(dirn, rows(q_ref, qt_ref, c), u_s[dirn, c], rows_s[dirn, :, cols(c)],
                                      s_ref, m_ref) for dirn, c in group]
        for (inter, m_prev), (dirn, c) in zip(started, group):
            h_t = _mlstm_state_finish(inter, m_prev, nl_s[dirn, :, cols(c)], rows_s[dirn, :, cols(c)])
            if c not in seen:
                seen.add(c)
                ht_s[:, cols(c)] = h_t
                continue
            out = jnp.transpose((ht_s[:, cols(c)] + h_t).astype(o_ref.dtype))
            if c == tail:
                ot_ref[...] = out
            else:
                o_ref[c * MT:(c + 1) * MT, :] = out


def _mlstm(q, k, xm, grow):
    tail_blk = N_REAL // TAIL
    real = lambda w: pl.BlockSpec((SEQ, w), lambda b, h: (b, h))
    tail = lambda w: pl.BlockSpec((TAIL, w), lambda b, h: (tail_blk + b, h))
    lp = SEQ + TAIL
    return pl.pallas_call(
        _mlstm_kernel,
        grid=(BATCH, M_HEADS),
        in_specs=[real(M_DK), tail(M_DK), real(M_DK), tail(M_DK), real(M_DV), tail(M_DV),
                  pl.BlockSpec((GATE_PAD, SEQ), lambda b, h: (h, b)),
                  pl.BlockSpec((GATE_PAD, TAIL), lambda b, h: (h, tail_blk + b))],
        out_specs=[pl.BlockSpec((SEQ, M_DV), lambda b, h: (b, h)),
                   pl.BlockSpec((TAIL, M_DV), lambda b, h: (b, h))],
        out_shape=[jax.ShapeDtypeStruct((N_REAL, M_INNER), BF16),
                   jax.ShapeDtypeStruct((BATCH * TAIL, M_INNER), BF16)],
        scratch_shapes=[pltpu.VMEM((2, M_DV, lp), F32),
                        pltpu.VMEM((2, N_MCHUNK + 1, STATE_ROWS, M_DK), F32),
                        pltpu.VMEM((2, ROWS_PER_DIR, lp), F32),
                        pltpu.VMEM((M_DV, lp), F32),
                        pltpu.VMEM((2, STATE_ROWS, M_DK), F32),
                        pltpu.VMEM((2, 1, 1), F32)],
        compiler_params=_params("parallel", "parallel"),
        name="mlstm",
    )(q, q, k, k, xm, xm, grow, grow)


PAIR = 2 * NA_DH


def _natten_kernel(q_ref, qt_ref, k_ref, kt_ref, v_ref, vt_ref, loc_ref, met_ref, o_ref, ot_ref):
    lane = lax.broadcasted_iota(jnp.int32, (1, PAIR), 1)
    first = lane < NA_DH
    k_tail = kt_ref[...]
    v_tail = vt_ref[...]

    def attend(q, keys, vals, bias_t):
        n = q.shape[0]
        zero = jnp.zeros_like(q)
        qs = jnp.concatenate([jnp.where(first, q, zero), jnp.where(first, zero, q)], axis=0)
        s = _dot_nt(keys, qs) + bias_t
        e = jnp.exp2(s - jnp.max(s, axis=0, keepdims=True))
        o = _dot(jnp.transpose(e.astype(BF16)), vals)
        inv = jnp.broadcast_to(1.0 / jnp.sum(e, axis=0, keepdims=True), (PAIR, 2 * n))
        o = o * jnp.transpose(inv)
        return jnp.where(first, o[:n], o[n:])

    met = met_ref[...]
    for r in range(ROWS):
        r0 = min(max(r - NA_WIN_H // 2, 0), ROWS - NA_WIN_H)
        qrows = pl.ds(r * GRID_W, GRID_W)
        krows = pl.ds(r0 * GRID_W, NA_WIN_H * GRID_W)
        keys = jnp.concatenate([k_ref[krows, :], k_tail], axis=0)
        vals = jnp.concatenate([v_ref[krows, :], v_tail], axis=0)
        top = r0 - r + NA_WIN_H - 1
        loc = loc_ref[top:top + NA_WIN_H].reshape(NA_WIN_H * GRID_W, PAIR)
        o = attend(q_ref[qrows, :], keys, vals, jnp.concatenate([loc, met], axis=0))
        o_ref[qrows, :] = o.astype(o_ref.dtype)
    tb = met
    tail_bias = jnp.concatenate([jnp.broadcast_to(tb[:, 0:1], (TAIL, TAIL)),
                                 jnp.broadcast_to(tb[:, NA_DH:NA_DH + 1], (TAIL, TAIL))], axis=1)
    ot_ref[...] = attend(qt_ref[...], k_tail, v_tail, tail_bias).astype(ot_ref.dtype)


def _natten(qn, kn, vn, loc, met):
    tail_blk = N_REAL // TAIL
    real = pl.BlockSpec((SEQ, PAIR), lambda b, p: (b, p))
    tail = pl.BlockSpec((TAIL, PAIR), lambda b, p: (tail_blk + b, p))
    return pl.pallas_call(
        _natten_kernel,
        grid=(BATCH, NA_HEADS // 2),
        in_specs=[real, tail, real, tail, real, tail,
                  pl.BlockSpec((None, 2 * NA_WIN_H - 1, GRID_W, PAIR), lambda b, p: (p, 0, 0, 0)),
                  pl.BlockSpec((None, TAIL, PAIR), lambda b, p: (p, 0, 0))],
        out_specs=[pl.BlockSpec((SEQ, PAIR), lambda b, p: (b, p)),
                   pl.BlockSpec((TAIL, PAIR), lambda b, p: (b, p))],
        out_shape=[jax.ShapeDtypeStruct((N_REAL, NA_INNER), BF16),
                   jax.ShapeDtypeStruct((BATCH * TAIL, NA_INNER), BF16)],
        compiler_params=_params("parallel", "parallel"),
        name="natten",
    )(qn, qn, kn, kn, vn, vn, loc, met)


def _natten_bias(rpb, meta_bias):
    qc = jnp.arange(GRID_W)
    kc = jnp.arange(GRID_W)
    win0 = jnp.clip(qc - NA_WIN_W // 2, 0, GRID_W - NA_WIN_W)
    ok = (kc[:, None] >= win0[None, :]) & (kc[:, None] < win0[None, :] + NA_WIN_W)
    dc = jnp.clip(kc[:, None] - qc[None, :], -(NA_WIN_W - 1), NA_WIN_W - 1) + NA_WIN_W - 1
    onehot = (dc[None] == jnp.arange(2 * NA_WIN_W - 1)[:, None, None]).astype(F32)
    t1 = jnp.einsum('hdj,jkq->dkhq', rpb.astype(F32), onehot, precision=lax.Precision.HIGHEST)
    t1 = jnp.where(ok[None, :, None, :], t1, MASKED) * LOG2E
    t1 = jnp.transpose(t1.reshape(2 * NA_WIN_H - 1, GRID_W, NA_HEADS // 2, PAIR), (2, 0, 1, 3))
    met = jnp.concatenate([jnp.full((NA_HEADS, NPAD), MASKED, F32), meta_bias.astype(F32)], axis=1) * LOG2E
    met = jnp.broadcast_to(met.reshape(NA_HEADS // 2, 2, TAIL, 1), (NA_HEADS // 2, 2, TAIL, NA_DH))
    met = jnp.transpose(met, (0, 2, 1, 3)).reshape(NA_HEADS // 2, TAIL, PAIR)
    return t1, met


def _merge_kernel(hsr_ref, hst_ref, ybr_ref, ybt_ref, xc_ref, og_ref, gab_ref,
                  ng_ref, sk_ref, wa_ref, wb_ref, wo_ref, w1_ref, w2_ref, o_ref, w1o_ref, w2o_ref):
    w1o_ref[...] = w1_ref[...].astype(BF16)
    w2o_ref[...] = w2_ref[...].astype(BF16)

    def body(hs_ref, yb_ref):
        hs = hs_ref[...].astype(F32)
        parts = []
        for hd in range(M_HEADS):
            sl = hs[:, hd * M_DV:(hd + 1) * M_DV]
            parts.append(sl * lax.rsqrt(jnp.mean(sl * sl, axis=-1, keepdims=True) + EPS))
        hn = jnp.concatenate(parts, axis=1) * ng_ref[...]
        y_a = _sigmoid(og_ref[...].astype(F32)) * (hn + sk_ref[...] * xc_ref[...].astype(F32))
        gab = gab_ref[...].astype(F32)
        mix = (_sigmoid(gab[:, :D_MODEL]) * _dot(y_a.astype(BF16), wa_ref[...])
               + _sigmoid(gab[:, D_MODEL:]) * _dot(yb_ref[...], wb_ref[...]))
        o_ref[...] = _dot(mix.astype(BF16), wo_ref[...])

    _real_or_tail(body, hsr_ref, hst_ref, ybr_ref, ybt_ref)


def _merge(hs_real, hs_tail, yb_real, yb_tail, xc, og, gab, ng, sk, wa, wb, wo, w1, w2):
    return pl.pallas_call(
        _merge_kernel,
        grid=(N_TILES,),
        in_specs=[_real_spec(M_INNER), _tail_spec(M_INNER), _real_spec(NA_INNER), _tail_spec(NA_INNER),
                  _row_spec(M_INNER), _row_spec(M_INNER), _row_spec(2 * D_MODEL)]
                 + [_const_spec(a.shape) for a in (ng, sk, wa, wb, wo)]
                 + [_slab_spec(w1.shape), _slab_spec(w2.shape)],
        out_specs=[_row_spec(D_MODEL), _slab_spec(w1.shape), _slab_spec(w2.shape)],
        out_shape=[jax.ShapeDtypeStruct((NT, D_MODEL), F32),
                   jax.ShapeDtypeStruct(w1.shape, BF16), jax.ShapeDtypeStruct(w2.shape, BF16)],
        compiler_params=_params("arbitrary"),
        name="merge",
    )(hs_real, hs_tail, yb_real, yb_tail, xc, og, gab, ng, sk, wa, wb, wo, w1, w2)


def _ffn_kernel(d_ref, xr_ref, xt_ref, g2_ref, w1_ref, w2_ref, or_ref, ot_ref):
    def body(x_ref, o_ref):
        h = x_ref[...] + d_ref[...]
        xn = (h * lax.rsqrt(jnp.mean(h * h, axis=-1, keepdims=True) + EPS) * g2_ref[...]).astype(BF16)
        z = jnp.maximum(_dot(xn, w1_ref[...]), 0.0)
        o_ref[...] = h + _dot((z * z).astype(BF16), w2_ref[...])

    _real_or_tail(body, xr_ref, xt_ref, or_ref, ot_ref)


def _ffn(delta, x2d, tailh, g2, w1, w2):
    return pl.pallas_call(
        _ffn_kernel,
        grid=(N_TILES,),
        in_specs=[_row_spec(D_MODEL), _real_spec(D_MODEL), _tail_spec(D_MODEL),
                  _const_spec(g2.shape), _const_spec(w1.shape), _const_spec(w2.shape)],
        out_specs=[_real_spec(D_MODEL), _tail_spec(D_MODEL)],
        out_shape=[jax.ShapeDtypeStruct((N_REAL, D_MODEL), F32),
                   jax.ShapeDtypeStruct((BATCH * TAIL, D_MODEL), F32)],
        compiler_params=_params("arbitrary"),
        name="ffn",
    )(delta, x2d, tailh, g2, w1, w2)


def kernel(x, meta_tokens, norm1_g, w_in, mlstm_conv_w, mlstm_conv_b, mlstm_wq, mlstm_wk, mlstm_gate_b, mlstm_norm_g, mlstm_skip, na_q_norm_g, na_k_norm_g, na_rpb, na_meta_bias, w_branch_a, w_branch_b, w_out, norm2_g, w_ff1, w_ff2):
    x2d = x.astype(F32).reshape(N_REAL, D_MODEL)
    tail = jnp.concatenate([jnp.zeros((NPAD, D_MODEL), F32), meta_tokens.astype(F32)], axis=0)
    tailh = jnp.tile(tail, (BATCH, 1))

    wxo, wqkv, wgab = _wcast(w_in.astype(F32))
    pad_gates = lambda a: jnp.pad(
        jnp.swapaxes(a.reshape(a.shape[0], 4, M_HEADS), 1, 2),
        ((0, 0), (0, 0), (0, GATE_PAD - 4))).reshape(a.shape[0], M_HEADS * GATE_PAD)
    wg = jnp.pad(pad_gates(w_in[:, OFF_MG:OFF_Q].astype(BF16)), ((0, 0), (0, LANES - M_HEADS * GATE_PAD)))
    gb = jnp.pad(pad_gates(mlstm_gate_b.astype(F32).reshape(1, 4 * M_HEADS)),
                 ((0, 0), (0, LANES - M_HEADS * GATE_PAD)))
    bd = jnp.kron(jnp.eye(NORM_GROUP // NA_DH, dtype=F32),
                  jnp.full((NA_DH, NA_DH), 1.0 / NA_DH, F32)).astype(BF16)
    qg = jnp.tile(na_q_norm_g.astype(F32), NA_HEADS)[None, :] * (NA_DH ** -0.5 * LOG2E)
    kg = jnp.tile(na_k_norm_g.astype(F32), NA_HEADS)[None, :]

    xb = x.astype(F32)
    xedge = jnp.concatenate([xb[:, SEQ - 1, :], xb[:, 0, :]], axis=0)
    wqk = jnp.concatenate([mlstm_wq, mlstm_wk], axis=-1).astype(BF16)
    vm, og, grow, qn, kn, vn, gab, xc, qm, km = _inproj(
        x2d, tailh, xedge, norm1_g.astype(F32)[None, :], wxo, wg, wqkv, wgab, gb, qg, kg, bd,
        mlstm_conv_w.astype(F32).reshape(3, M_INNER), mlstm_conv_b.astype(F32)[None, :], wqk)

    hs_real, hs_tail = _mlstm(qm, km, vm, grow)
    yb_real, yb_tail = _natten(qn, kn, vn, *_natten_bias(na_rpb, na_meta_bias))

    delta, w1, w2 = _merge(hs_real, hs_tail, yb_real, yb_tail, xc, og, gab,
                           mlstm_norm_g.astype(F32).reshape(1, M_INNER), mlstm_skip.astype(F32)[None, :],
                           w_branch_a.astype(BF16), w_branch_b.astype(BF16), w_out.astype(BF16),
                           w_ff1.astype(F32), w_ff2.astype(F32))
    out_real, _ = _ffn(delta, x2d, tailh, norm2_g.astype(F32)[None, :], w1, w2)
    return out_real.reshape(BATCH, SEQ, D_MODEL)
```

```python
import jax
import jax.numpy as jnp
from jax import lax
from jax.experimental import pallas as pl
from jax.experimental.pallas import tpu as pltpu

D_MODEL = 1024
BATCH = 4
SEQ = 4096
N_META = 16
GRID_W = 64
ROWS = SEQ // GRID_W
M_HEADS = 4
M_DV = 256
M_DK = 128
M_INNER = M_HEADS * M_DV
NA_HEADS = 8
NA_DH = 64
NA_INNER = NA_HEADS * NA_DH
NA_WIN_H = 8
NA_WIN_W = 16
D_FF = 4 * D_MODEL
EPS = 1e-6
NEG_LOG_GATE = -1e9
MASKED = -1e30
LOG2E = 1.4426950408889634

TAIL = 128
NPAD = TAIL - N_META
N_REAL = BATCH * SEQ
NT = N_REAL + BATCH * TAIL
TM = 512
N_TILES = NT // TM
N_REAL_TILES = N_REAL // TM

LANES = 128
BF16_ROWS = 16
VMEM_LIMIT = 56 * 1024 * 1024

F32 = jnp.float32
BF16 = jnp.bfloat16


def _dot(a, b):
    return jnp.dot(a, b, preferred_element_type=F32)


def _dot_nt(a, b):
    return lax.dot_general(a, b, (((1,), (1,)), ((), ())), preferred_element_type=F32)


def _sigmoid(z):
    return 0.5 * jnp.tanh(0.5 * z) + 0.5


def _log_sigmoid(z):
    return jnp.minimum(z, 0.0) - jnp.log1p(jnp.exp(-jnp.abs(z)))


def _const_spec(shape):
    nd = len(shape)
    return pl.BlockSpec(shape, lambda *_: (0,) * nd, pipeline_mode=pl.Buffered(1))


def _params(*sem):
    return pltpu.CompilerParams(dimension_semantics=sem, vmem_limit_bytes=VMEM_LIMIT)


GATE_PAD = 8
OFF_MX = 0
OFF_MO = OFF_MX + M_INNER
OFF_MG = OFF_MO + M_INNER
OFF_Q = OFF_MG + 4 * M_HEADS
OFF_G = OFF_Q + 3 * NA_INNER
D_IN_PROJ = OFF_G + 2 * D_MODEL


HALO = 8
HALO_PAD = BF16_ROWS
TILES_PER_SEQ = SEQ // TM
NORM_GROUP = 256
PAIR = 2 * NA_DH
NA_KEYS = NA_WIN_H * GRID_W + TAIL


def _inproj_kernel(x_ref, tail_ref, xprev_ref, xnext_ref, xedge_ref, g1_ref, w_ref, wg_ref,
                   gb_ref, qg_ref, kg_ref, bd_ref, cw_ref, cb_ref, wqk_ref, loc_ref, met_ref,
                   vm_ref, og_ref, grow_ref, q_ref, k_ref, v_ref, gab_ref, xc_ref, qm_ref, km_ref, bias_ref,
                   wqkv_s, wgab_s):
    i = pl.program_id(0)

    slab = jnp.minimum(i, N_REAL_TILES - 1)
    pair, offset = slab // NA_WIN_H, slab % NA_WIN_H
    loc = loc_ref[pair, pl.ds(NA_WIN_H - 1 - offset, NA_WIN_H)]
    bias_ref[...] = jnp.concatenate([loc.reshape(NA_WIN_H * GRID_W, PAIR), met_ref[pair]], axis=0)

    @pl.when(i == 0)
    def _():
        wqkv_s[...] = w_ref[:, OFF_Q:OFF_G]
        wgab_s[...] = w_ref[:, OFF_G:D_IN_PROJ]

    def norm1(h):
        ms = jnp.mean(h * h, axis=-1, keepdims=True)
        return (h * lax.rsqrt(ms + EPS) * g1_ref[...]).astype(BF16)

    def body(h_ref, is_tail):
        xn = norm1(h_ref[...])
        if is_tail:
            halo = jnp.concatenate([xedge_ref[...], jnp.zeros((HALO_PAD - 2 * BATCH, D_MODEL), F32)], axis=0)
        else:
            first = i % TILES_PER_SEQ == 0
            last = i % TILES_PER_SEQ == TILES_PER_SEQ - 1
            prev = jnp.where(first, tail_ref[TAIL - 1:TAIL, :], xprev_ref[HALO - 1:HALO, :])
            nxt = jnp.where(last, tail_ref[0:1, :], xnext_ref[0:1, :])
            halo = jnp.concatenate([prev, nxt, jnp.zeros((HALO_PAD - 2, D_MODEL), F32)], axis=0)
        xm_ext = _dot(jnp.concatenate([xn, norm1(halo)], axis=0), w_ref[:, OFF_MX:OFF_MO])
        og = _dot(xn, w_ref[:, OFF_MO:OFF_MG])
        gates = _dot(xn, wg_ref[...]) + gb_ref[...]
        gab = _dot(xn, wgab_s[...])
        qkv = _dot(xn, wqkv_s[...])
        xm = xm_ext[:TM]
        rows = lax.broadcasted_iota(jnp.int32, (TM, M_INNER), 0)
        x_m1 = pltpu.roll(xm, 1, axis=0)
        x_p1 = pltpu.roll(xm, TM - 1, axis=0)
        if is_tail:
            for b in range(BATCH):
                x_m1 = jnp.where(rows == b * TAIL, xm_ext[TM + b:TM + b + 1], x_m1)
                x_p1 = jnp.where(rows == b * TAIL + TAIL - 1, xm_ext[TM + BATCH + b:TM + BATCH + b + 1], x_p1)
        else:
            x_m1 = jnp.where(rows == 0, xm_ext[TM:TM + 1], x_m1)
            x_p1 = jnp.where(rows == TM - 1, xm_ext[TM + 1:TM + 2], x_p1)
        vm_ref[...] = xm.astype(BF16)
        z = cw_ref[0:1, :] * x_m1 + cw_ref[1:2, :] * xm + cw_ref[2:3, :] * x_p1 + cb_ref[...]
        xc = z * _sigmoid(z)
        xc_ref[...] = xc.astype(BF16)
        for hd in range(M_HEADS):
            qk = _dot(xc[:, hd * M_DV:(hd + 1) * M_DV].astype(BF16), wqk_ref[hd])
            qm_ref[:, hd * M_DK:(hd + 1) * M_DK] = qk[:, :M_DK].astype(BF16)
            km_ref[:, hd * M_DK:(hd + 1) * M_DK] = (qk[:, M_DK:] * (M_DK ** -0.5)).astype(BF16)
        og_ref[...] = og.astype(BF16)
        grow_ref[...] = jnp.transpose(gates)[:M_HEADS * GATE_PAD, :]
        gab_ref[...] = gab.astype(BF16)
        uq = qkv[:, :NA_INNER]
        uk = qkv[:, NA_INNER:2 * NA_INNER]
        def head_mean(sq):
            return jnp.concatenate([_dot(sq[:, j:j + NORM_GROUP].astype(BF16), bd_ref[...])
                                    for j in range(0, NA_INNER, NORM_GROUP)], axis=1)

        msq = head_mean(uq * uq)
        msk = head_mean(uk * uk)
        q_ref[...] = (uq * lax.rsqrt(msq + EPS) * qg_ref[...]).astype(BF16)
        k_ref[...] = (uk * lax.rsqrt(msk + EPS) * kg_ref[...]).astype(BF16)
        v_ref[...] = qkv[:, 2 * NA_INNER:].astype(BF16)

    pl.when(i < N_REAL_TILES)(lambda: body(x_ref, False))
    pl.when(i == N_REAL_TILES)(lambda: body(tail_ref, True))


def _real_or_tail(body, *ref_pairs):
    i = pl.program_id(0)
    pl.when(i < N_REAL_TILES)(lambda: body(*ref_pairs[0::2]))
    pl.when(i == N_REAL_TILES)(lambda: body(*ref_pairs[1::2]))


def _real_spec(width):
    return pl.BlockSpec((TM, width), lambda i: (jnp.minimum(i, N_REAL_TILES - 1), 0))


def _tail_spec(width):
    return pl.BlockSpec((TM, width), lambda i: (0, 0))


def _row_spec(width):
    return pl.BlockSpec((TM, width), lambda i: (i, 0))


def _slab_spec(shape):
    return pl.BlockSpec((shape[0] // N_REAL_TILES, shape[1]), lambda i: (jnp.minimum(i, N_REAL_TILES - 1), 0))


def _inproj(x2d, tailh, xedge, g1, w_all, wg, gb, qg, kg, bd, cw, cb, wqk, loc, met):
    assert (NA_HEADS // 2) * NA_WIN_H == N_REAL_TILES
    halo_blocks = N_REAL // HALO
    per_tile = TM // HALO
    consts = (g1, w_all, wg, gb, qg, kg, bd, cw, cb, wqk, loc, met)
    return pl.pallas_call(
        _inproj_kernel,
        grid=(N_TILES,),
        in_specs=[_real_spec(D_MODEL), _tail_spec(D_MODEL),
                  pl.BlockSpec((HALO, D_MODEL), lambda i: (jnp.clip(i * per_tile - 1, 0, halo_blocks - 1), 0)),
                  pl.BlockSpec((HALO, D_MODEL), lambda i: (jnp.clip((i + 1) * per_tile, 0, halo_blocks - 1), 0)),
                  _const_spec(xedge.shape)]
                 + [_const_spec(a.shape) for a in consts],
        out_specs=[_row_spec(M_INNER), _row_spec(M_INNER),
                   pl.BlockSpec((M_HEADS * GATE_PAD, TM), lambda i: (0, i)),
                   _row_spec(NA_INNER), _row_spec(NA_INNER), _row_spec(NA_INNER), _row_spec(2 * D_MODEL),
                   _row_spec(M_INNER), _row_spec(M_HEADS * M_DK), _row_spec(M_HEADS * M_DK),
                   pl.BlockSpec((None, None, NA_KEYS, PAIR),
                                lambda i: (jnp.minimum(i, N_REAL_TILES - 1) // NA_WIN_H,
                                           jnp.minimum(i, N_REAL_TILES - 1) % NA_WIN_H, 0, 0))],
        out_shape=[jax.ShapeDtypeStruct((NT, M_INNER), BF16),
                   jax.ShapeDtypeStruct((NT, M_INNER), BF16),
                   jax.ShapeDtypeStruct((M_HEADS * GATE_PAD, NT), F32),
                   jax.ShapeDtypeStruct((NT, NA_INNER), BF16),
                   jax.ShapeDtypeStruct((NT, NA_INNER), BF16),
                   jax.ShapeDtypeStruct((NT, NA_INNER), BF16),
                   jax.ShapeDtypeStruct((NT, 2 * D_MODEL), BF16),
                   jax.ShapeDtypeStruct((NT, M_INNER), BF16),
                   jax.ShapeDtypeStruct((NT, M_HEADS * M_DK), BF16),
                   jax.ShapeDtypeStruct((NT, M_HEADS * M_DK), BF16),
                   jax.ShapeDtypeStruct((NA_HEADS // 2, NA_WIN_H, NA_KEYS, PAIR), F32)],
        scratch_shapes=[pltpu.VMEM((D_MODEL, OFF_G - OFF_Q), BF16),
                        pltpu.VMEM((D_MODEL, D_IN_PROJ - OFF_G), BF16)],
        compiler_params=_params("arbitrary"),
        name="inproj",
    )(x2d, tailh, x2d, x2d, xedge, *consts)


MT = 256
N_MCHUNK = SEQ // MT
STATE_ROWS = M_DV + BF16_ROWS
ROWS_PER_DIR = 8
STATE_GROUP = N_MCHUNK + 1


def _split3(x):
    hi = x.astype(BF16)
    r1 = x - hi.astype(F32)
    mid = r1.astype(BF16)
    lo = (r1 - mid.astype(F32)).astype(BF16)
    return hi, mid, lo


def _mlstm_local_start(q, k, v, g8, is_tail):
    t = q.shape[0]
    li = [g8[0:1], g8[2:3]]
    lf = [_log_sigmoid(g8[1:2]), _log_sigmoid(g8[3:4])]
    if is_tail:
        pad = lax.broadcasted_iota(jnp.int32, (1, t), 1) < NPAD
        li = [jnp.where(pad, NEG_LOG_GATE, x) for x in li]
        lf = [jnp.where(pad, 0.0, x) for x in lf]
    si = lax.broadcasted_iota(jnp.int32, (t, t), 0)
    ti = lax.broadcasted_iota(jnp.int32, (t, t), 1)
    hi, mid, lo = _split3(jnp.concatenate(lf, axis=0))
    lhs = jnp.concatenate([hi, mid, lo, jnp.zeros((BF16_ROWS - 6, t), BF16)], axis=0)
    pref = _dot(lhs, (si <= ti).astype(BF16))
    pre_f = pref[0:1] + pref[2:3] + pref[4:5]
    pre_b = pref[1:2] + pref[3:4] + pref[5:6]
    b_end = [pre_f[:, t - 1:t], pre_b[:, t - 1:t]]
    b = [pre_f, b_end[1] - pre_b + lf[1]]
    s_t = _dot_nt(k, q)
    v_tb = jnp.transpose(v.astype(BF16))
    return li, b, b_end, s_t, v_tb, k


def _mlstm_local_finish(li, b, b_end, s_t, v_tb, k):
    t = s_t.shape[0]
    si = lax.broadcasted_iota(jnp.int32, (t, t), 0)
    ti = lax.broadcasted_iota(jnp.int32, (t, t), 1)
    kf = k.astype(F32)
    v_ext = jnp.concatenate([v_tb, jnp.ones((STATE_ROWS - M_DV, t), BF16)], axis=0)
    out = []
    for dirn in range(2):
        g2 = (li[dirn] - b[dirn]) * LOG2E
        g_col = jnp.transpose(jnp.broadcast_to(g2, (LANES, t)))
        g_st = jnp.where((si <= ti) if dirn == 0 else (si >= ti),
                         jnp.concatenate([g_col] * (t // LANES), axis=1), MASKED)
        g_max = jnp.max(g_st, axis=0, keepdims=True)
        p_t = s_t * jnp.exp2(g_st - g_max)
        den = jnp.sum(p_t, axis=0, keepdims=True)
        nl_t = _dot(v_tb, p_t.astype(BF16))
        m_loc = b[dirn] + g_max * (1.0 / LOG2E)
        top = jnp.max(g2, axis=1, keepdims=True)
        kw = (kf * jnp.exp2(g_col - top)).astype(BF16)
        u = _dot(v_ext, kw)
        a_max = b_end[dirn] + top * (1.0 / LOG2E)
        rows = jnp.concatenate([m_loc, den, b[dirn], jnp.broadcast_to(a_max, (1, t)),
                                jnp.broadcast_to(b_end[dirn], (1, t)),
                                jnp.zeros((ROWS_PER_DIR - 5, t), F32)], axis=0)
        out.append((nl_t, u, rows))
    return out


def _mlstm_state_start(dirn, q, u, rows, s_ref, m_ref):
    a_max, b_end = rows[3:4, 0:1], rows[4:5, 0:1]
    m_prev = m_ref[dirn]
    s_old = s_ref[dirn]
    inter = _dot_nt(s_old.astype(BF16), q)
    m_new = jnp.maximum(b_end + m_prev, a_max)
    s_ref[dirn] = jnp.exp(b_end + m_prev - m_new) * s_old + jnp.exp(a_max - m_new) * u
    m_ref[dirn] = m_new
    return inter, m_prev


def _mlstm_state_finish(inter, m_prev, nl_t, rows):
    m_loc, den_loc, b = rows[0:1], rows[1:2], rows[2:3]
    m_inter = b + m_prev
    m_t = jnp.maximum(m_inter, m_loc)
    w_inter = jnp.exp(m_inter - m_t)
    w_loc = jnp.exp(m_loc - m_t)
    den = w_inter * inter[M_DV:M_DV + 1] + w_loc * den_loc
    scale = 1.0 / jnp.maximum(jnp.abs(den), jnp.exp(-m_t))
    return (w_inter * scale) * inter[:M_DV] + (w_loc * scale) * nl_t


def _mlstm_kernel(q_ref, qt_ref, k_ref, kt_ref, v_ref, vt_ref, g_ref, gt_ref, o_ref, ot_ref,
                  nl_s, u_s, rows_s, ht_s, s_ref, m_ref):
    s_ref[...] = jnp.zeros_like(s_ref)
    m_ref[...] = jnp.zeros_like(m_ref)
    tail = N_MCHUNK

    def cols(c):
        return pl.ds(SEQ, TAIL) if c == tail else pl.ds(c * MT, MT)

    def rows(ref, ref_tail, c):
        return ref_tail[...] if c == tail else ref[c * MT:(c + 1) * MT, :]

    def gates(c):
        return gt_ref[...] if c == tail else g_ref[:, c * MT:(c + 1) * MT]

    order = [tail] + list(range(N_MCHUNK))
    started = [_mlstm_local_start(rows(q_ref, qt_ref, c), rows(k_ref, kt_ref, c), rows(v_ref, vt_ref, c),
                                  gates(c), c == tail) for c in order]
    for c, st in zip(order, started):
        for dirn, (nl_t, u, row_vecs) in enumerate(_mlstm_local_finish(*st)):
            nl_s[dirn, :, cols(c)] = nl_t
            u_s[dirn, c] = u
            rows_s[dirn, :, cols(c)] = row_vecs

    fwd = [(0, c) for c in order]
    bwd = [(1, c) for c in reversed(order)]
    visits = [v for pair in zip(fwd, bwd) for v in pair]
    seen = set()
    for lo in range(0, len(visits), STATE_GROUP):
        group = visits[lo:lo + STATE_GROUP]
        started = [_mlstm_state_start(dirn, rows(q_ref, qt_ref, c), u_s[dirn, c], rows_s[dirn, :, cols(c)],
                                      s_ref, m_ref) for dirn, c in group]
        for (inter, m_prev), (dirn, c) in zip(started, group):
            h_t = _mlstm_state_finish(inter, m_prev, nl_s[dirn, :, cols(c)], rows_s[dirn, :, cols(c)])
            if c not in seen:
                seen.add(c)
                ht_s[:, cols(c)] = h_t
                continue
            out = jnp.transpose((ht_s[:, cols(c)] + h_t).astype(o_ref.dtype))
            if c == tail:
                ot_ref[...] = out
            else:
                o_ref[c * MT:(c + 1) * MT, :] = out


def _mlstm(q, k, xm, grow):
    tail_blk = N_REAL // TAIL
    real = lambda w: pl.BlockSpec((SEQ, w), lambda b, h: (b, h))
    tail = lambda w: pl.BlockSpec((TAIL, w), lambda b, h: (tail_blk + b, h))
    lp = SEQ + TAIL
    return pl.pallas_call(
        _mlstm_kernel,
        grid=(BATCH, M_HEADS),
        in_specs=[real(M_DK), tail(M_DK), real(M_DK), tail(M_DK), real(M_DV), tail(M_DV),
                  pl.BlockSpec((GATE_PAD, SEQ), lambda b, h: (h, b)),
                  pl.BlockSpec((GATE_PAD, TAIL), lambda b, h: (h, tail_blk + b))],
        out_specs=[pl.BlockSpec((SEQ, M_DV), lambda b, h: (b, h)),
                   pl.BlockSpec((TAIL, M_DV), lambda b, h: (b, h))],
        out_shape=[jax.ShapeDtypeStruct((N_REAL, M_INNER), BF16),
                   jax.ShapeDtypeStruct((BATCH * TAIL, M_INNER), BF16)],
        scratch_shapes=[pltpu.VMEM((2, M_DV, lp), F32),
                        pltpu.VMEM((2, N_MCHUNK + 1, STATE_ROWS, M_DK), F32),
                        pltpu.VMEM((2, ROWS_PER_DIR, lp), F32),
                        pltpu.VMEM((M_DV, lp), F32),
                        pltpu.VMEM((2, STATE_ROWS, M_DK), F32),
                        pltpu.VMEM((2, 1, 1), F32)],
        compiler_params=_params("parallel", "parallel"),
        name="mlstm",
    )(q, q, k, k, xm, xm, grow, grow)


def _natten_kernel(q_ref, qt_ref, k_ref, kt_ref, v_ref, vt_ref, bias_ref, o_ref, ot_ref):
    lane = lax.broadcasted_iota(jnp.int32, (1, PAIR), 1)
    first = lane < NA_DH
    k_tail = kt_ref[...]
    v_tail = vt_ref[...]

    def attend(q, keys, vals, bias_t):
        n = q.shape[0]
        zero = jnp.zeros_like(q)
        qs = jnp.concatenate([jnp.where(first, q, zero), jnp.where(first, zero, q)], axis=0)
        s = _dot_nt(keys, qs) + bias_t
        e = jnp.exp2(s - jnp.max(s, axis=0, keepdims=True))
        o = _dot(jnp.transpose(e.astype(BF16)), vals)
        inv = jnp.broadcast_to(1.0 / jnp.sum(e, axis=0, keepdims=True), (PAIR, 2 * n))
        o = o * jnp.transpose(inv)
        return jnp.where(first, o[:n], o[n:])

    for r in range(ROWS):
        r0 = min(max(r - NA_WIN_H // 2, 0), ROWS - NA_WIN_H)
        qrows = pl.ds(r * GRID_W, GRID_W)
        krows = pl.ds(r0 * GRID_W, NA_WIN_H * GRID_W)
        keys = jnp.concatenate([k_ref[krows, :], k_tail], axis=0)
        vals = jnp.concatenate([v_ref[krows, :], v_tail], axis=0)
        o = attend(q_ref[qrows, :], keys, vals, bias_ref[r - r0])
        o_ref[qrows, :] = o.astype(o_ref.dtype)
    tb = bias_ref[0, NA_WIN_H * GRID_W:, :]
    tail_bias = jnp.concatenate([jnp.broadcast_to(tb[:, 0:1], (TAIL, TAIL)),
                                 jnp.broadcast_to(tb[:, NA_DH:NA_DH + 1], (TAIL, TAIL))], axis=1)
    ot_ref[...] = attend(qt_ref[...], k_tail, v_tail, tail_bias).astype(ot_ref.dtype)


def _natten(qn, kn, vn, bias):
    tail_blk = N_REAL // TAIL
    real = pl.BlockSpec((SEQ, PAIR), lambda b, p: (b, p))
    tail = pl.BlockSpec((TAIL, PAIR), lambda b, p: (tail_blk + b, p))
    return pl.pallas_call(
        _natten_kernel,
        grid=(BATCH, NA_HEADS // 2),
        in_specs=[real, tail, real, tail, real, tail,
                  pl.BlockSpec((None, NA_WIN_H, NA_KEYS, PAIR), lambda b, p: (p, 0, 0, 0))],
        out_specs=[pl.BlockSpec((SEQ, PAIR), lambda b, p: (b, p)),
                   pl.BlockSpec((TAIL, PAIR), lambda b, p: (b, p))],
        out_shape=[jax.ShapeDtypeStruct((N_REAL, NA_INNER), BF16),
                   jax.ShapeDtypeStruct((BATCH * TAIL, NA_INNER), BF16)],
        compiler_params=_params("parallel", "parallel"),
        name="natten",
    )(qn, qn, kn, kn, vn, vn, bias)


def _natten_bias(rpb, meta_bias):
    qc = jnp.arange(GRID_W)
    kc = jnp.arange(GRID_W)
    win0 = jnp.clip(qc - NA_WIN_W // 2, 0, GRID_W - NA_WIN_W)
    ok = (kc[:, None] >= win0[None, :]) & (kc[:, None] < win0[None, :] + NA_WIN_W)
    dc = jnp.clip(kc[:, None] - qc[None, :], -(NA_WIN_W - 1), NA_WIN_W - 1) + NA_WIN_W - 1
    onehot = (dc[None] == jnp.arange(2 * NA_WIN_W - 1)[:, None, None]).astype(F32)
    t1 = jnp.einsum('hdj,jkq->dkhq', rpb.astype(F32), onehot, precision=lax.Precision.HIGHEST)
    t1 = jnp.where(ok[None, :, None, :], t1, MASKED) * LOG2E
    t1 = jnp.transpose(t1.reshape(2 * NA_WIN_H - 1, GRID_W, NA_HEADS // 2, PAIR), (2, 0, 1, 3))
    met = jnp.concatenate([jnp.full((NA_HEADS, NPAD), MASKED, F32), meta_bias.astype(F32)], axis=1) * LOG2E
    met = jnp.broadcast_to(met.reshape(NA_HEADS // 2, 2, TAIL, 1), (NA_HEADS // 2, 2, TAIL, NA_DH))
    met = jnp.transpose(met, (0, 2, 1, 3)).reshape(NA_HEADS // 2, TAIL, PAIR)
    return t1, met


def _merge_kernel(hsr_ref, hst_ref, ybr_ref, ybt_ref, xc_ref, og_ref, gab_ref,
                  ng_ref, sk_ref, wa_ref, wb_ref, wo_ref, w1_ref, w2_ref, o_ref, w1o_ref, w2o_ref):
    w1o_ref[...] = w1_ref[...].astype(BF16)
    w2o_ref[...] = w2_ref[...].astype(BF16)

    def body(hs_ref, yb_ref):
        hs = hs_ref[...].astype(F32)
        parts = []
        for hd in range(M_HEADS):
            sl = hs[:, hd * M_DV:(hd + 1) * M_DV]
            parts.append(sl * lax.rsqrt(jnp.mean(sl * sl, axis=-1, keepdims=True) + EPS))
        hn = jnp.concatenate(parts, axis=1) * ng_ref[...]
        y_a = _sigmoid(og_ref[...].astype(F32)) * (hn + sk_ref[...] * xc_ref[...].astype(F32))
        gab = gab_ref[...].astype(F32)
        mix = (_sigmoid(gab[:, :D_MODEL]) * _dot(y_a.astype(BF16), wa_ref[...])
               + _sigmoid(gab[:, D_MODEL:]) * _dot(yb_ref[...], wb_ref[...]))
        o_ref[...] = _dot(mix.astype(BF16), wo_ref[...])

    _real_or_tail(body, hsr_ref, hst_ref, ybr_ref, ybt_ref)


def _merge(hs_real, hs_tail, yb_real, yb_tail, xc, og, gab, ng, sk, wa, wb, wo, w1, w2):
    return pl.pallas_call(
        _merge_kernel,
        grid=(N_TILES,),
        in_specs=[_real_spec(M_INNER), _tail_spec(M_INNER), _real_spec(NA_INNER), _tail_spec(NA_INNER),
                  _row_spec(M_INNER), _row_spec(M_INNER), _row_spec(2 * D_MODEL)]
                 + [_const_spec(a.shape) for a in (ng, sk, wa, wb, wo)]
                 + [_slab_spec(w1.shape), _slab_spec(w2.shape)],
        out_specs=[_row_spec(D_MODEL), _slab_spec(w1.shape), _slab_spec(w2.shape)],
        out_shape=[jax.ShapeDtypeStruct((NT, D_MODEL), F32),
                   jax.ShapeDtypeStruct(w1.shape, BF16), jax.ShapeDtypeStruct(w2.shape, BF16)],
        compiler_params=_params("arbitrary"),
        name="merge",
    )(hs_real, hs_tail, yb_real, yb_tail, xc, og, gab, ng, sk, wa, wb, wo, w1, w2)


def _ffn_kernel(d_ref, xr_ref, xt_ref, g2_ref, w1_ref, w2_ref, or_ref, ot_ref):
    def body(x_ref, o_ref):
        h = x_ref[...] + d_ref[...]
        xn = (h * lax.rsqrt(jnp.mean(h * h, axis=-1, keepdims=True) + EPS) * g2_ref[...]).astype(BF16)
        z = jnp.maximum(_dot(xn, w1_ref[...]), 0.0)
        o_ref[...] = h + _dot((z * z).astype(BF16), w2_ref[...])

    _real_or_tail(body, xr_ref, xt_ref, or_ref, ot_ref)


def _ffn(delta, x2d, tailh, g2, w1, w2):
    return pl.pallas_call(
        _ffn_kernel,
        grid=(N_TILES,),
        in_specs=[_row_spec(D_MODEL), _real_spec(D_MODEL), _tail_spec(D_MODEL),
                  _const_spec(g2.shape), _const_spec(w1.shape), _const_spec(w2.shape)],
        out_specs=[_real_spec(D_MODEL), _tail_spec(D_MODEL)],
        out_shape=[jax.ShapeDtypeStruct((N_REAL, D_MODEL), F32),
                   jax.ShapeDtypeStruct((BATCH * TAIL, D_MODEL), F32)],
        compiler_params=_params("arbitrary"),
        name="ffn",
    )(delta, x2d, tailh, g2, w1, w2)


def kernel(x, meta_tokens, norm1_g, w_in, mlstm_conv_w, mlstm_conv_b, mlstm_wq, mlstm_wk, mlstm_gate_b, mlstm_norm_g, mlstm_skip, na_q_norm_g, na_k_norm_g, na_rpb, na_meta_bias, w_branch_a, w_branch_b, w_out, norm2_g, w_ff1, w_ff2):
    x2d = x.astype(F32).reshape(N_REAL, D_MODEL)
    tail = jnp.concatenate([jnp.zeros((NPAD, D_MODEL), F32), meta_tokens.astype(F32)], axis=0)
    tailh = jnp.tile(tail, (BATCH, 1))

    w_all = w_in.astype(BF16)
    pad_gates = lambda a: jnp.pad(
        jnp.swapaxes(a.reshape(a.shape[0], 4, M_HEADS), 1, 2),
        ((0, 0), (0, 0), (0, GATE_PAD - 4))).reshape(a.shape[0], M_HEADS * GATE_PAD)
    wg = jnp.pad(pad_gates(w_all[:, OFF_MG:OFF_Q]), ((0, 0), (0, LANES - M_HEADS * GATE_PAD)))
    gb = jnp.pad(pad_gates(mlstm_gate_b.astype(F32).reshape(1, 4 * M_HEADS)),
                 ((0, 0), (0, LANES - M_HEADS * GATE_PAD)))
    bd = jnp.kron(jnp.eye(NORM_GROUP // NA_DH, dtype=F32),
                  jnp.full((NA_DH, NA_DH), 1.0 / NA_DH, F32)).astype(BF16)
    qg = jnp.tile(na_q_norm_g.astype(F32), NA_HEADS)[None, :] * (NA_DH ** -0.5 * LOG2E)
    kg = jnp.tile(na_k_norm_g.astype(F32), NA_HEADS)[None, :]

    xb = x.astype(F32)
    xedge = jnp.concatenate([xb[:, SEQ - 1, :], xb[:, 0, :]], axis=0)
    wqk = jnp.concatenate([mlstm_wq, mlstm_wk], axis=-1).astype(BF16)
    vm, og, grow, qn, kn, vn, gab, xc, qm, km, bias = _inproj(
        x2d, tailh, xedge, norm1_g.astype(F32)[None, :], w_all, wg, gb, qg, kg, bd,
        mlstm_conv_w.astype(F32).reshape(3, M_INNER), mlstm_conv_b.astype(F32)[None, :], wqk,
        *_natten_bias(na_rpb, na_meta_bias))

    hs_real, hs_tail = _mlstm(qm, km, vm, grow)
    yb_real, yb_tail = _natten(qn, kn, vn, bias)

    delta, w1, w2 = _merge(hs_real, hs_tail, yb_real, yb_tail, xc, og, gab,
                           mlstm_norm_g.astype(F32).reshape(1, M_INNER), mlstm_skip.astype(F32)[None, :],
                           w_branch_a.astype(BF16), w_branch_b.astype(BF16), w_out.astype(BF16),
                           w_ff1.astype(F32), w_ff2.astype(F32))
    out_real, _ = _ffn(delta, x2d, tailh, norm2_g.astype(F32)[None, :], w1, w2)
    return out_real.reshape(BATCH, SEQ, D_MODEL)
```

```python
import jax
import jax.numpy as jnp
from jax import lax
from jax.experimental import pallas as pl
from jax.experimental.pallas import tpu as pltpu

D_MODEL = 1024
BATCH = 4
SEQ = 4096
N_META = 16
GRID_W = 64
ROWS = SEQ // GRID_W
M_HEADS = 4
M_DV = 256
M_DK = 128
M_INNER = M_HEADS * M_DV
NA_HEADS = 8
NA_DH = 64
NA_INNER = NA_HEADS * NA_DH
NA_WIN_H = 8
NA_WIN_W = 16
D_FF = 4 * D_MODEL
EPS = 1e-6
NEG_LOG_GATE = -1e9
MASKED = -1e30
LOG2E = 1.4426950408889634

TAIL = 128
NPAD = TAIL - N_META
N_REAL = BATCH * SEQ
NT = N_REAL + BATCH * TAIL
TM = 512
N_TILES = NT // TM
N_REAL_TILES = N_REAL // TM

LANES = 128
BF16_ROWS = 16
VMEM_LIMIT = 56 * 1024 * 1024

F32 = jnp.float32
BF16 = jnp.bfloat16


def _dot(a, b):
    return jnp.dot(a, b, preferred_element_type=F32)


def _dot_nt(a, b):
    return lax.dot_general(a, b, (((1,), (1,)), ((), ())), preferred_element_type=F32)


def _sigmoid(z):
    return 0.5 * jnp.tanh(0.5 * z) + 0.5


def _log_sigmoid(z):
    return jnp.minimum(z, 0.0) - jnp.log1p(jnp.exp(-jnp.abs(z)))


def _const_spec(shape):
    nd = len(shape)
    return pl.BlockSpec(shape, lambda *_: (0,) * nd, pipeline_mode=pl.Buffered(1))


def _params(*sem):
    return pltpu.CompilerParams(dimension_semantics=sem, vmem_limit_bytes=VMEM_LIMIT)


GATE_PAD = 8
OFF_MX = 0
OFF_MO = OFF_MX + M_INNER
OFF_MG = OFF_MO + M_INNER
OFF_Q = OFF_MG + 4 * M_HEADS
OFF_G = OFF_Q + 3 * NA_INNER
D_IN_PROJ = OFF_G + 2 * D_MODEL


HALO = 8
HALO_PAD = BF16_ROWS
TILES_PER_SEQ = SEQ // TM
NORM_GROUP = 256
PAIR = 2 * NA_DH
NA_KEYS = NA_WIN_H * GRID_W + TAIL


def _inproj_kernel(x_ref, tail_ref, xprev_ref, xnext_ref, xedge_ref, g1_ref, w_ref, wg_ref,
                   gb_ref, qg_ref, kg_ref, bd_ref, cw_ref, cb_ref, wqk_ref, loc_ref, met_ref, wa_ref, wb_ref, wo_ref,
                   vm_ref, og_ref, grow_ref, q_ref, k_ref, v_ref, gab_ref, xc_ref, qm_ref, km_ref, bias_ref,
                   wao_ref, wbo_ref, woo_ref,
                   wqkv_s, wgab_s):
    i = pl.program_id(0)

    slab = jnp.minimum(i, N_REAL_TILES - 1)
    pair, offset = slab // NA_WIN_H, slab % NA_WIN_H
    loc = loc_ref[pair, pl.ds(NA_WIN_H - 1 - offset, NA_WIN_H)]
    bias_ref[...] = jnp.concatenate([loc.reshape(NA_WIN_H * GRID_W, PAIR), met_ref[pair]], axis=0)
    wao_ref[...] = wa_ref[...].astype(BF16)
    wbo_ref[...] = wb_ref[...].astype(BF16)
    woo_ref[...] = wo_ref[...].astype(BF16)

    @pl.when(i == 0)
    def _():
        wqkv_s[...] = w_ref[:, OFF_Q:OFF_G]
        wgab_s[...] = w_ref[:, OFF_G:D_IN_PROJ]

    def norm1(h):
        ms = jnp.mean(h * h, axis=-1, keepdims=True)
        return (h * lax.rsqrt(ms + EPS) * g1_ref[...]).astype(BF16)

    def body(h_ref, is_tail):
        xn = norm1(h_ref[...])
        if is_tail:
            halo = jnp.concatenate([xedge_ref[...], jnp.zeros((HALO_PAD - 2 * BATCH, D_MODEL), F32)], axis=0)
        else:
            first = i % TILES_PER_SEQ == 0
            last = i % TILES_PER_SEQ == TILES_PER_SEQ - 1
            prev = jnp.where(first, tail_ref[TAIL - 1:TAIL, :], xprev_ref[HALO - 1:HALO, :])
            nxt = jnp.where(last, tail_ref[0:1, :], xnext_ref[0:1, :])
            halo = jnp.concatenate([prev, nxt, jnp.zeros((HALO_PAD - 2, D_MODEL), F32)], axis=0)
        xm_ext = _dot(jnp.concatenate([xn, norm1(halo)], axis=0), w_ref[:, OFF_MX:OFF_MO])
        og = _dot(xn, w_ref[:, OFF_MO:OFF_MG])
        gates_t = _dot_nt(wg_ref[...], xn) + gb_ref[...]
        gab = _dot(xn, wgab_s[...])
        qkv = _dot(xn, wqkv_s[...])
        xm = xm_ext[:TM]
        rows = lax.broadcasted_iota(jnp.int32, (TM, M_INNER), 0)
        x_m1 = pltpu.roll(xm, 1, axis=0)
        x_p1 = pltpu.roll(xm, TM - 1, axis=0)
        if is_tail:
            for b in range(BATCH):
                x_m1 = jnp.where(rows == b * TAIL, xm_ext[TM + b:TM + b + 1], x_m1)
                x_p1 = jnp.where(rows == b * TAIL + TAIL - 1, xm_ext[TM + BATCH + b:TM + BATCH + b + 1], x_p1)
        else:
            x_m1 = jnp.where(rows == 0, xm_ext[TM:TM + 1], x_m1)
            x_p1 = jnp.where(rows == TM - 1, xm_ext[TM + 1:TM + 2], x_p1)
        vm_ref[...] = xm.astype(BF16)
        z = cw_ref[0:1, :] * x_m1 + cw_ref[1:2, :] * xm + cw_ref[2:3, :] * x_p1 + cb_ref[...]
        xc = z * _sigmoid(z)
        xc_ref[...] = xc.astype(BF16)
        for hd in range(M_HEADS):
            qk = _dot(xc[:, hd * M_DV:(hd + 1) * M_DV].astype(BF16), wqk_ref[hd])
            qm_ref[:, hd * M_DK:(hd + 1) * M_DK] = qk[:, :M_DK].astype(BF16)
            km_ref[:, hd * M_DK:(hd + 1) * M_DK] = (qk[:, M_DK:] * (M_DK ** -0.5)).astype(BF16)
        og_ref[...] = og.astype(BF16)
        grow_ref[...] = gates_t
        gab_ref[...] = gab.astype(BF16)
        uq = qkv[:, :NA_INNER]
        uk = qkv[:, NA_INNER:2 * NA_INNER]
        def head_mean(sq):
            return jnp.concatenate([_dot(sq[:, j:j + NORM_GROUP].astype(BF16), bd_ref[...])
                                    for j in range(0, NA_INNER, NORM_GROUP)], axis=1)

        msq = head_mean(uq * uq)
        msk = head_mean(uk * uk)
        q_ref[...] = (uq * lax.rsqrt(msq + EPS) * qg_ref[...]).astype(BF16)
        k_ref[...] = (uk * lax.rsqrt(msk + EPS) * kg_ref[...]).astype(BF16)
        v_ref[...] = qkv[:, 2 * NA_INNER:].astype(BF16)

    pl.when(i < N_REAL_TILES)(lambda: body(x_ref, False))
    pl.when(i == N_REAL_TILES)(lambda: body(tail_ref, True))


def _real_or_tail(body, *ref_pairs):
    i = pl.program_id(0)
    pl.when(i < N_REAL_TILES)(lambda: body(*ref_pairs[0::2]))
    pl.when(i == N_REAL_TILES)(lambda: body(*ref_pairs[1::2]))


def _real_spec(width):
    return pl.BlockSpec((TM, width), lambda i: (jnp.minimum(i, N_REAL_TILES - 1), 0))


def _tail_spec(width):
    return pl.BlockSpec((TM, width), lambda i: (0, 0))


def _row_spec(width):
    return pl.BlockSpec((TM, width), lambda i: (i, 0))


def _slab_spec(shape):
    return pl.BlockSpec((shape[0] // N_REAL_TILES, shape[1]), lambda i: (jnp.minimum(i, N_REAL_TILES - 1), 0))


def _inproj(x2d, tailh, xedge, g1, w_all, wg, gb, qg, kg, bd, cw, cb, wqk, loc, met, wa, wb, wo):
    assert (NA_HEADS // 2) * NA_WIN_H == N_REAL_TILES
    halo_blocks = N_REAL // HALO
    per_tile = TM // HALO
    consts = (g1, w_all, wg, gb, qg, kg, bd, cw, cb, wqk, loc, met)
    return pl.pallas_call(
        _inproj_kernel,
        grid=(N_TILES,),
        in_specs=[_real_spec(D_MODEL), _tail_spec(D_MODEL),
                  pl.BlockSpec((HALO, D_MODEL), lambda i: (jnp.clip(i * per_tile - 1, 0, halo_blocks - 1), 0)),
                  pl.BlockSpec((HALO, D_MODEL), lambda i: (jnp.clip((i + 1) * per_tile, 0, halo_blocks - 1), 0)),
                  _const_spec(xedge.shape)]
                 + [_const_spec(a.shape) for a in consts]
                 + [_slab_spec(w.shape) for w in (wa, wb, wo)],
        out_specs=[_row_spec(M_INNER), _row_spec(M_INNER),
                   pl.BlockSpec((M_HEADS * GATE_PAD, TM), lambda i: (0, i)),
                   _row_spec(NA_INNER), _row_spec(NA_INNER), _row_spec(NA_INNER), _row_spec(2 * D_MODEL),
                   _row_spec(M_INNER), _row_spec(M_HEADS * M_DK), _row_spec(M_HEADS * M_DK),
                   pl.BlockSpec((None, None, NA_KEYS, PAIR),
                                lambda i: (jnp.minimum(i, N_REAL_TILES - 1) // NA_WIN_H,
                                           jnp.minimum(i, N_REAL_TILES - 1) % NA_WIN_H, 0, 0))]
                  + [_slab_spec(w.shape) for w in (wa, wb, wo)],
        out_shape=[jax.ShapeDtypeStruct((NT, M_INNER), BF16),
                   jax.ShapeDtypeStruct((NT, M_INNER), BF16),
                   jax.ShapeDtypeStruct((M_HEADS * GATE_PAD, NT), F32),
                   jax.ShapeDtypeStruct((NT, NA_INNER), BF16),
                   jax.ShapeDtypeStruct((NT, NA_INNER), BF16),
                   jax.ShapeDtypeStruct((NT, NA_INNER), BF16),
                   jax.ShapeDtypeStruct((NT, 2 * D_MODEL), BF16),
                   jax.ShapeDtypeStruct((NT, M_INNER), BF16),
                   jax.ShapeDtypeStruct((NT, M_HEADS * M_DK), BF16),
                   jax.ShapeDtypeStruct((NT, M_HEADS * M_DK), BF16),
                   jax.ShapeDtypeStruct((NA_HEADS // 2, NA_WIN_H, NA_KEYS, PAIR), F32)]
                  + [jax.ShapeDtypeStruct(w.shape, BF16) for w in (wa, wb, wo)],
        scratch_shapes=[pltpu.VMEM((D_MODEL, OFF_G - OFF_Q), BF16),
                        pltpu.VMEM((D_MODEL, D_IN_PROJ - OFF_G), BF16)],
        compiler_params=_params("arbitrary"),
        name="inproj",
    )(x2d, tailh, x2d, x2d, xedge, *consts, wa, wb, wo)


MT = 256
N_MCHUNK = SEQ // MT
STATE_ROWS = M_DV + BF16_ROWS
ROWS_PER_DIR = 8
STATE_GROUP = N_MCHUNK + 1


def _split3(x):
    hi = x.astype(BF16)
    r1 = x - hi.astype(F32)
    mid = r1.astype(BF16)
    lo = (r1 - mid.astype(F32)).astype(BF16)
    return hi, mid, lo


def _mlstm_local_start(q, k, v, g8, is_tail):
    t = q.shape[0]
    li = [g8[0:1], g8[2:3]]
    lf = [_log_sigmoid(g8[1:2]), _log_sigmoid(g8[3:4])]
    if is_tail:
        pad = lax.broadcasted_iota(jnp.int32, (1, t), 1) < NPAD
        li = [jnp.where(pad, NEG_LOG_GATE, x) for x in li]
        lf = [jnp.where(pad, 0.0, x) for x in lf]
    si = lax.broadcasted_iota(jnp.int32, (t, t), 0)
    ti = lax.broadcasted_iota(jnp.int32, (t, t), 1)
    hi, mid, lo = _split3(jnp.concatenate(lf, axis=0))
    lhs = jnp.concatenate([hi, mid, lo, jnp.zeros((BF16_ROWS - 6, t), BF16)], axis=0)
    pref = _dot(lhs, (si <= ti).astype(BF16))
    pre_f = pref[0:1] + pref[2:3] + pref[4:5]
    pre_b = pref[1:2] + pref[3:4] + pref[5:6]
    b_end = [pre_f[:, t - 1:t], pre_b[:, t - 1:t]]
    b = [pre_f, b_end[1] - pre_b + lf[1]]
    s_t = _dot_nt(k, q)
    v_tb = jnp.transpose(v.astype(BF16))
    return li, b, b_end, s_t, v_tb, k


def _mlstm_local_finish(li, b, b_end, s_t, v_tb, k):
    t = s_t.shape[0]
    si = lax.broadcasted_iota(jnp.int32, (t, t), 0)
    ti = lax.broadcasted_iota(jnp.int32, (t, t), 1)
    kf = k.astype(F32)
    v_ext = jnp.concatenate([v_tb, jnp.ones((STATE_ROWS - M_DV, t), BF16)], axis=0)
    out = []
    for dirn in range(2):
        g2 = (li[dirn] - b[dirn]) * LOG2E
        g_col = jnp.transpose(jnp.broadcast_to(g2, (LANES, t)))
        g_st = jnp.where((si <= ti) if dirn == 0 else (si >= ti),
                         jnp.concatenate([g_col] * (t // LANES), axis=1), MASKED)
        g_max = jnp.max(g_st, axis=0, keepdims=True)
        p_t = s_t * jnp.exp2(g_st - g_max)
        den = jnp.sum(p_t, axis=0, keepdims=True)
        nl_t = _dot(v_tb, p_t.astype(BF16))
        m_loc = b[dirn] + g_max * (1.0 / LOG2E)
        top = jnp.max(g2, axis=1, keepdims=True)
        kw = (kf * jnp.exp2(g_col - top)).astype(BF16)
        u = _dot(v_ext, kw)
        a_max = b_end[dirn] + top * (1.0 / LOG2E)
        rows = jnp.concatenate([m_loc, den, b[dirn], jnp.broadcast_to(a_max, (1, t)),
                                jnp.broadcast_to(b_end[dirn], (1, t)),
                                jnp.zeros((ROWS_PER_DIR - 5, t), F32)], axis=0)
        out.append((nl_t, u, rows))
    return out


def _mlstm_state_start(dirn, q, u, rows, s_ref, m_ref):
    a_max, b_end = rows[3:4, 0:1], rows[4:5, 0:1]
    m_prev = m_ref[dirn]
    s_old = s_ref[dirn]
    inter = _dot_nt(s_old.astype(BF16), q)
    m_new = jnp.maximum(b_end + m_prev, a_max)
    s_ref[dirn] = jnp.exp(b_end + m_prev - m_new) * s_old + jnp.exp(a_max - m_new) * u
    m_ref[dirn] = m_new
    return inter, m_prev


def _mlstm_state_finish(inter, m_prev, nl_t, rows):
    m_loc, den_loc, b = rows[0:1], rows[1:2], rows[2:3]
    m_inter = b + m_prev
    m_t = jnp.maximum(m_inter, m_loc)
    w_inter = jnp.exp(m_inter - m_t)
    w_loc = jnp.exp(m_loc - m_t)
    den = w_inter * inter[M_DV:M_DV + 1] + w_loc * den_loc
    scale = 1.0 / jnp.maximum(jnp.abs(den), jnp.exp(-m_t))
    return (w_inter * scale) * inter[:M_DV] + (w_loc * scale) * nl_t


def _mlstm_kernel(q_ref, qt_ref, k_ref, kt_ref, v_ref, vt_ref, g_ref, gt_ref, o_ref, ot_ref,
                  nl_s, u_s, rows_s, ht_s, s_ref, m_ref):
    s_ref[...] = jnp.zeros_like(s_ref)
    m_ref[...] = jnp.zeros_like(m_ref)
    tail = N_MCHUNK

    def cols(c):
        return pl.ds(SEQ, TAIL) if c == tail else pl.ds(c * MT, MT)

    def rows(ref, ref_tail, c):
        return ref_tail[...] if c == tail else ref[c * MT:(c + 1) * MT, :]

    def gates(c):
        return gt_ref[...] if c == tail else g_ref[:, c * MT:(c + 1) * MT]

    order = [tail] + list(range(N_MCHUNK))
    started = [_mlstm_local_start(rows(q_ref, qt_ref, c), rows(k_ref, kt_ref, c), rows(v_ref, vt_ref, c),
                                  gates(c), c == tail) for c in order]
    for c, st in zip(order, started):
        for dirn, (nl_t, u, row_vecs) in enumerate(_mlstm_local_finish(*st)):
            nl_s[dirn, :, cols(c)] = nl_t
            u_s[dirn, c] = u
            rows_s[dirn, :, cols(c)] = row_vecs

    fwd = [(0, c) for c in order]
    bwd = [(1, c) for c in reversed(order)]
    visits = [v for pair in zip(fwd, bwd) for v in pair]
    seen = set()
    for lo in range(0, len(visits), STATE_GROUP):
        group = visits[lo:lo + STATE_GROUP]
        started = [_mlstm_state_start(dirn, rows(q_ref, qt_ref, c), u_s[dirn, c], rows_s[dirn, :, cols(c)],
                                      s_ref, m_ref) for dirn, c in group]
        for (inter, m_prev), (dirn, c) in zip(started, group):
            h_t = _mlstm_state_finish(inter, m_prev, nl_s[dirn, :, cols(c)], rows_s[dirn, :, cols(c)])
            if c not in seen:
                seen.add(c)
                ht_s[:, cols(c)] = h_t
                continue
            out = jnp.transpose((ht_s[:, cols(c)] + h_t).astype(o_ref.dtype))
            if c == tail:
                ot_ref[...] = out
            else:
                o_ref[c * MT:(c + 1) * MT, :] = out


def _mlstm(q, k, xm, grow):
    tail_blk = N_REAL // TAIL
    real = lambda w: pl.BlockSpec((SEQ, w), lambda b, h: (b, h))
    tail = lambda w: pl.BlockSpec((TAIL, w), lambda b, h: (tail_blk + b, h))
    lp = SEQ + TAIL
    return pl.pallas_call(
        _mlstm_kernel,
        grid=(BATCH, M_HEADS),
        in_specs=[real(M_DK), tail(M_DK), real(M_DK), tail(M_DK), real(M_DV), tail(M_DV),
                  pl.BlockSpec((GATE_PAD, SEQ), lambda b, h: (h, b)),
                  pl.BlockSpec((GATE_PAD, TAIL), lambda b, h: (h, tail_blk + b))],
        out_specs=[pl.BlockSpec((SEQ, M_DV), lambda b, h: (b, h)),
                   pl.BlockSpec((TAIL, M_DV), lambda b, h: (b, h))],
        out_shape=[jax.ShapeDtypeStruct((N_REAL, M_INNER), BF16),
                   jax.ShapeDtypeStruct((BATCH * TAIL, M_INNER), BF16)],
        scratch_shapes=[pltpu.VMEM((2, M_DV, lp), F32),
                        pltpu.VMEM((2, N_MCHUNK + 1, STATE_ROWS, M_DK), F32),
                        pltpu.VMEM((2, ROWS_PER_DIR, lp), F32),
                        pltpu.VMEM((M_DV, lp), F32),
                        pltpu.VMEM((2, STATE_ROWS, M_DK), F32),
                        pltpu.VMEM((2, 1, 1), F32)],
        compiler_params=_params("parallel", "parallel"),
        name="mlstm",
    )(q, q, k, k, xm, xm, grow, grow)


def _natten_kernel(q_ref, qt_ref, k_ref, kt_ref, v_ref, vt_ref, bias_ref, o_ref, ot_ref):
    lane = lax.broadcasted_iota(jnp.int32, (1, PAIR), 1)
    first = lane < NA_DH
    k_tail = kt_ref[...]
    v_tail = vt_ref[...]

    def attend(q, keys, vals, bias_t):
        n = q.shape[0]
        zero = jnp.zeros_like(q)
        qs = jnp.concatenate([jnp.where(first, q, zero), jnp.where(first, zero, q)], axis=0)
        s = _dot_nt(keys, qs) + bias_t
        e = jnp.exp2(s - jnp.max(s, axis=0, keepdims=True))
        o = _dot(jnp.transpose(e.astype(BF16)), vals)
        inv = jnp.broadcast_to(1.0 / jnp.sum(e, axis=0, keepdims=True), (PAIR, 2 * n))
        o = o * jnp.transpose(inv)
        return jnp.where(first, o[:n], o[n:])

    for r in range(ROWS):
        r0 = min(max(r - NA_WIN_H // 2, 0), ROWS - NA_WIN_H)
        qrows = pl.ds(r * GRID_W, GRID_W)
        krows = pl.ds(r0 * GRID_W, NA_WIN_H * GRID_W)
        keys = jnp.concatenate([k_ref[krows, :], k_tail], axis=0)
        vals = jnp.concatenate([v_ref[krows, :], v_tail], axis=0)
        o = attend(q_ref[qrows, :], keys, vals, bias_ref[r - r0])
        o_ref[qrows, :] = o.astype(o_ref.dtype)
    tb = bias_ref[0, NA_WIN_H * GRID_W:, :]
    tail_bias = jnp.concatenate([jnp.broadcast_to(tb[:, 0:1], (TAIL, TAIL)),
                                 jnp.broadcast_to(tb[:, NA_DH:NA_DH + 1], (TAIL, TAIL))], axis=1)
    ot_ref[...] = attend(qt_ref[...], k_tail, v_tail, tail_bias).astype(ot_ref.dtype)


def _natten(qn, kn, vn, bias):
    tail_blk = N_REAL // TAIL
    real = pl.BlockSpec((SEQ, PAIR), lambda b, p: (b, p))
    tail = pl.BlockSpec((TAIL, PAIR), lambda b, p: (tail_blk + b, p))
    return pl.pallas_call(
        _natten_kernel,
        grid=(BATCH, NA_HEADS // 2),
        in_specs=[real, tail, real, tail, real, tail,
                  pl.BlockSpec((None, NA_WIN_H, NA_KEYS, PAIR), lambda b, p: (p, 0, 0, 0))],
        out_specs=[pl.BlockSpec((SEQ, PAIR), lambda b, p: (b, p)),
                   pl.BlockSpec((TAIL, PAIR), lambda b, p: (b, p))],
        out_shape=[jax.ShapeDtypeStruct((N_REAL, NA_INNER), BF16),
                   jax.ShapeDtypeStruct((BATCH * TAIL, NA_INNER), BF16)],
        compiler_params=_params("parallel", "parallel"),
        name="natten",
    )(qn, qn, kn, kn, vn, vn, bias)


def _natten_bias(rpb, meta_bias):
    qc = jnp.arange(GRID_W)
    kc = jnp.arange(GRID_W)
    win0 = jnp.clip(qc - NA_WIN_W // 2, 0, GRID_W - NA_WIN_W)
    ok = (kc[:, None] >= win0[None, :]) & (kc[:, None] < win0[None, :] + NA_WIN_W)
    dc = jnp.clip(kc[:, None] - qc[None, :], -(NA_WIN_W - 1), NA_WIN_W - 1) + NA_WIN_W - 1
    onehot = (dc[None] == jnp.arange(2 * NA_WIN_W - 1)[:, None, None]).astype(F32)
    t1 = jnp.einsum('hdj,jkq->dkhq', rpb.astype(F32), onehot, precision=lax.Precision.HIGHEST)
    t1 = jnp.where(ok[None, :, None, :], t1, MASKED) * LOG2E
    t1 = jnp.transpose(t1.reshape(2 * NA_WIN_H - 1, GRID_W, NA_HEADS // 2, PAIR), (2, 0, 1, 3))
    met = jnp.concatenate([jnp.full((NA_HEADS, NPAD), MASKED, F32), meta_bias.astype(F32)], axis=1) * LOG2E
    met = jnp.broadcast_to(met.reshape(NA_HEADS // 2, 2, TAIL, 1), (NA_HEADS // 2, 2, TAIL, NA_DH))
    met = jnp.transpose(met, (0, 2, 1, 3)).reshape(NA_HEADS // 2, TAIL, PAIR)
    return t1, met


def _merge_kernel(hsr_ref, hst_ref, ybr_ref, ybt_ref, xc_ref, og_ref, gab_ref,
                  ng_ref, sk_ref, wa_ref, wb_ref, wo_ref, w1_ref, w2_ref, o_ref, w1o_ref, w2o_ref):
    w1o_ref[...] = w1_ref[...].astype(BF16)
    w2o_ref[...] = w2_ref[...].astype(BF16)

    def body(hs_ref, yb_ref):
        hs = hs_ref[...].astype(F32)
        parts = []
        for hd in range(M_HEADS):
            sl = hs[:, hd * M_DV:(hd + 1) * M_DV]
            parts.append(sl * lax.rsqrt(jnp.mean(sl * sl, axis=-1, keepdims=True) + EPS))
        hn = jnp.concatenate(parts, axis=1) * ng_ref[...]
        y_a = _sigmoid(og_ref[...].astype(F32)) * (hn + sk_ref[...] * xc_ref[...].astype(F32))
        gab = gab_ref[...].astype(F32)
        mix = (_sigmoid(gab[:, :D_MODEL]) * _dot(y_a.astype(BF16), wa_ref[...])
               + _sigmoid(gab[:, D_MODEL:]) * _dot(yb_ref[...], wb_ref[...]))
        o_ref[...] = _dot(mix.astype(BF16), wo_ref[...])

    _real_or_tail(body, hsr_ref, hst_ref, ybr_ref, ybt_ref)


def _merge(hs_real, hs_tail, yb_real, yb_tail, xc, og, gab, ng, sk, wa, wb, wo, w1, w2):
    return pl.pallas_call(
        _merge_kernel,
        grid=(N_TILES,),
        in_specs=[_real_spec(M_INNER), _tail_spec(M_INNER), _real_spec(NA_INNER), _tail_spec(NA_INNER),
                  _row_spec(M_INNER), _row_spec(M_INNER), _row_spec(2 * D_MODEL)]
                 + [_const_spec(a.shape) for a in (ng, sk, wa, wb, wo)]
                 + [_slab_spec(w1.shape), _slab_spec(w2.shape)],
        out_specs=[_row_spec(D_MODEL), _slab_spec(w1.shape), _slab_spec(w2.shape)],
        out_shape=[jax.ShapeDtypeStruct((NT, D_MODEL), F32),
                   jax.ShapeDtypeStruct(w1.shape, BF16), jax.ShapeDtypeStruct(w2.shape, BF16)],
        compiler_params=_params("arbitrary"),
        name="merge",
    )(hs_real, hs_tail, yb_real, yb_tail, xc, og, gab, ng, sk, wa, wb, wo, w1, w2)


def _ffn_kernel(d_ref, xr_ref, xt_ref, g2_ref, w1_ref, w2_ref, or_ref, ot_ref):
    def body(x_ref, o_ref):
        h = x_ref[...] + d_ref[...]
        xn = (h * lax.rsqrt(jnp.mean(h * h, axis=-1, keepdims=True) + EPS) * g2_ref[...]).astype(BF16)
        z = jnp.maximum(_dot(xn, w1_ref[...]), 0.0)
        o_ref[...] = h + _dot((z * z).astype(BF16), w2_ref[...])

    _real_or_tail(body, xr_ref, xt_ref, or_ref, ot_ref)


def _ffn(delta, x2d, tailh, g2, w1, w2):
    return pl.pallas_call(
        _ffn_kernel,
        grid=(N_TILES,),
        in_specs=[_row_spec(D_MODEL), _real_spec(D_MODEL), _tail_spec(D_MODEL),
                  _const_spec(g2.shape), _const_spec(w1.shape), _const_spec(w2.shape)],
        out_specs=[_real_spec(D_MODEL), _tail_spec(D_MODEL)],
        out_shape=[jax.ShapeDtypeStruct((N_REAL, D_MODEL), F32),
                   jax.ShapeDtypeStruct((BATCH * TAIL, D_MODEL), F32)],
        compiler_params=_params("arbitrary"),
        name="ffn",
    )(delta, x2d, tailh, g2, w1, w2)


def kernel(x, meta_tokens, norm1_g, w_in, mlstm_conv_w, mlstm_conv_b, mlstm_wq, mlstm_wk, mlstm_gate_b, mlstm_norm_g, mlstm_skip, na_q_norm_g, na_k_norm_g, na_rpb, na_meta_bias, w_branch_a, w_branch_b, w_out, norm2_g, w_ff1, w_ff2):
    x2d = x.astype(F32).reshape(N_REAL, D_MODEL)
    tail = jnp.concatenate([jnp.zeros((NPAD, D_MODEL), F32), meta_tokens.astype(F32)], axis=0)
    tailh = jnp.tile(tail, (BATCH, 1))

    w_all = w_in.astype(BF16)
    pad_gates = lambda a: jnp.pad(
        jnp.swapaxes(a.reshape(a.shape[0], 4, M_HEADS), 1, 2),
        ((0, 0), (0, 0), (0, GATE_PAD - 4))).reshape(a.shape[0], M_HEADS * GATE_PAD)
    wg = jnp.transpose(pad_gates(w_all[:, OFF_MG:OFF_Q]))
    gb = jnp.transpose(pad_gates(mlstm_gate_b.astype(F32).reshape(1, 4 * M_HEADS)))
    bd = jnp.kron(jnp.eye(NORM_GROUP // NA_DH, dtype=F32),
                  jnp.full((NA_DH, NA_DH), 1.0 / NA_DH, F32)).astype(BF16)
    qg = jnp.tile(na_q_norm_g.astype(F32), NA_HEADS)[None, :] * (NA_DH ** -0.5 * LOG2E)
    kg = jnp.tile(na_k_norm_g.astype(F32), NA_HEADS)[None, :]

    xb = x.astype(F32)
    xedge = jnp.concatenate([xb[:, SEQ - 1, :], xb[:, 0, :]], axis=0)
    wqk = jnp.concatenate([mlstm_wq, mlstm_wk], axis=-1).astype(BF16)
    vm, og, grow, qn, kn, vn, gab, xc, qm, km, bias, wa, wb, wo = _inproj(
        x2d, tailh, xedge, norm1_g.astype(F32)[None, :], w_all, wg, gb, qg, kg, bd,
        mlstm_conv_w.astype(F32).reshape(3, M_INNER), mlstm_conv_b.astype(F32)[None, :], wqk,
        *_natten_bias(na_rpb, na_meta_bias),
        w_branch_a.astype(F32), w_branch_b.astype(F32), w_out.astype(F32))

    hs_real, hs_tail = _mlstm(qm, km, vm, grow)
    yb_real, yb_tail = _natten(qn, kn, vn, bias)

    delta, w1, w2 = _merge(hs_real, hs_tail, yb_real, yb_tail, xc, og, gab,
                           mlstm_norm_g.astype(F32).reshape(1, M_INNER), mlstm_skip.astype(F32)[None, :],
                           wa, wb, wo,
                           w_ff1.astype(F32), w_ff2.astype(F32))
    out_real, _ = _ffn(delta, x2d, tailh, norm2_g.astype(F32)[None, :], w1, w2)
    return out_real.reshape(BATCH, SEQ, D_MODEL)
```

```python
import jax
import jax.numpy as jnp
from jax import lax
from jax.experimental import pallas as pl
from jax.experimental.pallas import tpu as pltpu

D_MODEL = 1024
BATCH = 4
SEQ = 4096
N_META = 16
GRID_W = 64
ROWS = SEQ // GRID_W
M_HEADS = 4
M_DV = 256
M_DK = 128
M_INNER = M_HEADS * M_DV
NA_HEADS = 8
NA_DH = 64
NA_INNER = NA_HEADS * NA_DH
NA_WIN_H = 8
NA_WIN_W = 16
D_FF = 4 * D_MODEL
EPS = 1e-6
NEG_LOG_GATE = -1e9
MASKED = -1e30
LOG2E = 1.4426950408889634

TAIL = 128
NPAD = TAIL - N_META
N_REAL = BATCH * SEQ
NT = N_REAL + BATCH * TAIL
TM = 512
N_TILES = NT // TM
N_REAL_TILES = N_REAL // TM

LANES = 128
BF16_ROWS = 16
VMEM_LIMIT = 56 * 1024 * 1024

F32 = jnp.float32
BF16 = jnp.bfloat16


def _dot(a, b):
    return jnp.dot(a, b, preferred_element_type=F32)


def _dot_nt(a, b):
    return lax.dot_general(a, b, (((1,), (1,)), ((), ())), preferred_element_type=F32)


def _sigmoid(z):
    return 0.5 * jnp.tanh(0.5 * z) + 0.5


def _log_sigmoid(z):
    return jnp.minimum(z, 0.0) - jnp.log1p(jnp.exp(-jnp.abs(z)))


def _const_spec(shape):
    nd = len(shape)
    return pl.BlockSpec(shape, lambda *_: (0,) * nd, pipeline_mode=pl.Buffered(1))


def _params(*sem, vmem_mib=None):
    limit = VMEM_LIMIT if vmem_mib is None else vmem_mib * 1024 * 1024
    return pltpu.CompilerParams(dimension_semantics=sem, vmem_limit_bytes=limit)


GATE_PAD = 8
OFF_MX = 0
OFF_MO = OFF_MX + M_INNER
OFF_MG = OFF_MO + M_INNER
OFF_Q = OFF_MG + 4 * M_HEADS
OFF_G = OFF_Q + 3 * NA_INNER
D_IN_PROJ = OFF_G + 2 * D_MODEL


HALO = 8
HALO_PAD = BF16_ROWS
TILES_PER_SEQ = SEQ // TM
NORM_GROUP = 256
PAIR = 2 * NA_DH
NA_KEYS = NA_WIN_H * GRID_W + TAIL


def _inproj_kernel(x_ref, tail_ref, xprev_ref, xnext_ref, xedge_ref, g1_ref, w_ref, wg_ref,
                   gb_ref, qg_ref, kg_ref, bd_ref, cw_ref, cb_ref, wqk_ref, loc_ref, met_ref,
                   vm_ref, og_ref, grow_ref, q_ref, k_ref, v_ref, gab_ref, xc_ref, qm_ref, km_ref, bias_ref,
                   wqkv_s, wgab_s):
    i = pl.program_id(0)

    slab = jnp.minimum(i, N_REAL_TILES - 1)
    pair, offset = slab // NA_WIN_H, slab % NA_WIN_H
    loc = loc_ref[pair, pl.ds(NA_WIN_H - 1 - offset, NA_WIN_H)]
    bias_ref[...] = jnp.concatenate([loc.reshape(NA_WIN_H * GRID_W, PAIR), met_ref[pair]], axis=0)

    @pl.when(i == 0)
    def _():
        wqkv_s[...] = w_ref[:, OFF_Q:OFF_G]
        wgab_s[...] = w_ref[:, OFF_G:D_IN_PROJ]

    def norm1(h):
        ms = jnp.mean(h * h, axis=-1, keepdims=True)
        return (h * lax.rsqrt(ms + EPS) * g1_ref[...]).astype(BF16)

    def body(h_ref, is_tail):
        xn = norm1(h_ref[...])
        if is_tail:
            halo = jnp.concatenate([xedge_ref[...], jnp.zeros((HALO_PAD - 2 * BATCH, D_MODEL), F32)], axis=0)
        else:
            first = i % TILES_PER_SEQ == 0
            last = i % TILES_PER_SEQ == TILES_PER_SEQ - 1
            prev = jnp.where(first, tail_ref[TAIL - 1:TAIL, :], xprev_ref[HALO - 1:HALO, :])
            nxt = jnp.where(last, tail_ref[0:1, :], xnext_ref[0:1, :])
            halo = jnp.concatenate([prev, nxt, jnp.zeros((HALO_PAD - 2, D_MODEL), F32)], axis=0)
        xm_ext = _dot(jnp.concatenate([xn, norm1(halo)], axis=0), w_ref[:, OFF_MX:OFF_MO])
        og = _dot(xn, w_ref[:, OFF_MO:OFF_MG])
        gates = _dot(xn, wg_ref[...]) + gb_ref[...]
        gab = _dot(xn, wgab_s[...])
        qkv = _dot(xn, wqkv_s[...])
        xm = xm_ext[:TM]
        rows = lax.broadcasted_iota(jnp.int32, (TM, M_INNER), 0)
        x_m1 = pltpu.roll(xm, 1, axis=0)
        x_p1 = pltpu.roll(xm, TM - 1, axis=0)
        if is_tail:
            for b in range(BATCH):
                x_m1 = jnp.where(rows == b * TAIL, xm_ext[TM + b:TM + b + 1], x_m1)
                x_p1 = jnp.where(rows == b * TAIL + TAIL - 1, xm_ext[TM + BATCH + b:TM + BATCH + b + 1], x_p1)
        else:
            x_m1 = jnp.where(rows == 0, xm_ext[TM:TM + 1], x_m1)
            x_p1 = jnp.where(rows == TM - 1, xm_ext[TM + 1:TM + 2], x_p1)
        vm_ref[...] = xm.astype(BF16)
        z = cw_ref[0:1, :] * x_m1 + cw_ref[1:2, :] * xm + cw_ref[2:3, :] * x_p1 + cb_ref[...]
        xc = z * _sigmoid(z)
        xc_ref[...] = xc.astype(BF16)
        for hd in range(M_HEADS):
            qk = _dot(xc[:, hd * M_DV:(hd + 1) * M_DV].astype(BF16), wqk_ref[hd])
            qm_ref[:, hd * M_DK:(hd + 1) * M_DK] = qk[:, :M_DK].astype(BF16)
            km_ref[:, hd * M_DK:(hd + 1) * M_DK] = (qk[:, M_DK:] * (M_DK ** -0.5)).astype(BF16)
        og_ref[...] = og.astype(BF16)
        grow_ref[...] = jnp.transpose(gates)[:M_HEADS * GATE_PAD, :]
        gab_ref[...] = gab.astype(BF16)
        uq = qkv[:, :NA_INNER]
        uk = qkv[:, NA_INNER:2 * NA_INNER]
        def head_mean(sq):
            return jnp.concatenate([_dot(sq[:, j:j + NORM_GROUP].astype(BF16), bd_ref[...])
                                    for j in range(0, NA_INNER, NORM_GROUP)], axis=1)

        msq = head_mean(uq * uq)
        msk = head_mean(uk * uk)
        q_ref[...] = (uq * lax.rsqrt(msq + EPS) * qg_ref[...]).astype(BF16)
        k_ref[...] = (uk * lax.rsqrt(msk + EPS) * kg_ref[...]).astype(BF16)
        v_ref[...] = qkv[:, 2 * NA_INNER:].astype(BF16)

    pl.when(i < N_REAL_TILES)(lambda: body(x_ref, False))
    pl.when(i == N_REAL_TILES)(lambda: body(tail_ref, True))


def _real_or_tail(body, *ref_pairs):
    i = pl.program_id(0)
    pl.when(i < N_REAL_TILES)(lambda: body(*ref_pairs[0::2]))
    pl.when(i == N_REAL_TILES)(lambda: body(*ref_pairs[1::2]))


def _real_spec(width):
    return pl.BlockSpec((TM, width), lambda i: (jnp.minimum(i, N_REAL_TILES - 1), 0))


def _tail_spec(width):
    return pl.BlockSpec((TM, width), lambda i: (0, 0))


def _row_spec(width):
    return pl.BlockSpec((TM, width), lambda i: (i, 0))


def _slab_spec(shape):
    return pl.BlockSpec((shape[0] // N_REAL_TILES, shape[1]), lambda i: (jnp.minimum(i, N_REAL_TILES - 1), 0))


def _inproj(x2d, tailh, xedge, g1, w_all, wg, gb, qg, kg, bd, cw, cb, wqk, loc, met):
    assert (NA_HEADS // 2) * NA_WIN_H == N_REAL_TILES
    halo_blocks = N_REAL // HALO
    per_tile = TM // HALO
    consts = (g1, w_all, wg, gb, qg, kg, bd, cw, cb, wqk, loc, met)
    return pl.pallas_call(
        _inproj_kernel,
        grid=(N_TILES,),
        in_specs=[_real_spec(D_MODEL), _tail_spec(D_MODEL),
                  pl.BlockSpec((HALO, D_MODEL), lambda i: (jnp.clip(i * per_tile - 1, 0, halo_blocks - 1), 0)),
                  pl.BlockSpec((HALO, D_MODEL), lambda i: (jnp.clip((i + 1) * per_tile, 0, halo_blocks - 1), 0)),
                  _const_spec(xedge.shape)]
                 + [_const_spec(a.shape) for a in consts],
        out_specs=[_row_spec(M_INNER), _row_spec(M_INNER),
                   pl.BlockSpec((M_HEADS * GATE_PAD, TM), lambda i: (0, i)),
                   _row_spec(NA_INNER), _row_spec(NA_INNER), _row_spec(NA_INNER), _row_spec(2 * D_MODEL),
                   _row_spec(M_INNER), _row_spec(M_HEADS * M_DK), _row_spec(M_HEADS * M_DK),
                   pl.BlockSpec((None, None, NA_KEYS, PAIR),
                                lambda i: (jnp.minimum(i, N_REAL_TILES - 1) // NA_WIN_H,
                                           jnp.minimum(i, N_REAL_TILES - 1) % NA_WIN_H, 0, 0))],
        out_shape=[jax.ShapeDtypeStruct((NT, M_INNER), BF16),
                   jax.ShapeDtypeStruct((NT, M_INNER), BF16),
                   jax.ShapeDtypeStruct((M_HEADS * GATE_PAD, NT), F32),
                   jax.ShapeDtypeStruct((NT, NA_INNER), BF16),
                   jax.ShapeDtypeStruct((NT, NA_INNER), BF16),
                   jax.ShapeDtypeStruct((NT, NA_INNER), BF16),
                   jax.ShapeDtypeStruct((NT, 2 * D_MODEL), BF16),
                   jax.ShapeDtypeStruct((NT, M_INNER), BF16),
                   jax.ShapeDtypeStruct((NT, M_HEADS * M_DK), BF16),
                   jax.ShapeDtypeStruct((NT, M_HEADS * M_DK), BF16),
                   jax.ShapeDtypeStruct((NA_HEADS // 2, NA_WIN_H, NA_KEYS, PAIR), F32)],
        scratch_shapes=[pltpu.VMEM((D_MODEL, OFF_G - OFF_Q), BF16),
                        pltpu.VMEM((D_MODEL, D_IN_PROJ - OFF_G), BF16)],
        compiler_params=_params("arbitrary", vmem_mib=48),
        name="inproj",
    )(x2d, tailh, x2d, x2d, xedge, *consts)


MT = 256
N_MCHUNK = SEQ // MT
STATE_ROWS = M_DV + BF16_ROWS
ROWS_PER_DIR = 8
STATE_GROUP = N_MCHUNK + 1


def _split3(x):
    hi = x.astype(BF16)
    r1 = x - hi.astype(F32)
    mid = r1.astype(BF16)
    lo = (r1 - mid.astype(F32)).astype(BF16)
    return hi, mid, lo


def _mlstm_local_start(q, k, v, g8, is_tail):
    t = q.shape[0]
    li = [g8[0:1], g8[2:3]]
    lf = [_log_sigmoid(g8[1:2]), _log_sigmoid(g8[3:4])]
    if is_tail:
        pad = lax.broadcasted_iota(jnp.int32, (1, t), 1) < NPAD
        li = [jnp.where(pad, NEG_LOG_GATE, x) for x in li]
        lf = [jnp.where(pad, 0.0, x) for x in lf]
    si = lax.broadcasted_iota(jnp.int32, (t, t), 0)
    ti = lax.broadcasted_iota(jnp.int32, (t, t), 1)
    hi, mid, lo = _split3(jnp.concatenate(lf, axis=0))
    lhs = jnp.concatenate([hi, mid, lo, jnp.zeros((BF16_ROWS - 6, t), BF16)], axis=0)
    pref = _dot(lhs, (si <= ti).astype(BF16))
    pre_f = pref[0:1] + pref[2:3] + pref[4:5]
    pre_b = pref[1:2] + pref[3:4] + pref[5:6]
    b_end = [pre_f[:, t - 1:t], pre_b[:, t - 1:t]]
    b = [pre_f, b_end[1] - pre_b + lf[1]]
    s_t = _dot_nt(k, q)
    v_tb = jnp.transpose(v.astype(BF16))
    return li, b, b_end, s_t, v_tb, k


def _mlstm_local_finish(li, b, b_end, s_t, v_tb, k):
    t = s_t.shape[0]
    si = lax.broadcasted_iota(jnp.int32, (t, t), 0)
    ti = lax.broadcasted_iota(jnp.int32, (t, t), 1)
    kf = k.astype(F32)
    v_ext = jnp.concatenate([v_tb, jnp.ones((STATE_ROWS - M_DV, t), BF16)], axis=0)
    out = []
    for dirn in range(2):
        g2 = (li[dirn] - b[dirn]) * LOG2E
        g_col = jnp.transpose(jnp.broadcast_to(g2, (LANES, t)))
        g_st = jnp.where((si <= ti) if dirn == 0 else (si >= ti),
                         jnp.concatenate([g_col] * (t // LANES), axis=1), MASKED)
        g_max = jnp.max(g_st, axis=0, keepdims=True)
        p_t = s_t * jnp.exp2(g_st - g_max)
        den = jnp.sum(p_t, axis=0, keepdims=True)
        nl_t = _dot(v_tb, p_t.astype(BF16))
        m_loc = b[dirn] + g_max * (1.0 / LOG2E)
        top = jnp.max(g2, axis=1, keepdims=True)
        kw = (kf * jnp.exp2(g_col - top)).astype(BF16)
        u = _dot(v_ext, kw)
        a_max = b_end[dirn] + top * (1.0 / LOG2E)
        rows = jnp.concatenate([m_loc, den, b[dirn], jnp.broadcast_to(a_max, (1, t)),
                                jnp.broadcast_to(b_end[dirn], (1, t)),
                                jnp.zeros((ROWS_PER_DIR - 5, t), F32)], axis=0)
        out.append((nl_t, u, rows))
    return out


def _mlstm_state_start(dirn, q, u, rows, s_ref, m_ref):
    a_max, b_end = rows[3:4, 0:1], rows[4:5, 0:1]
    m_prev = m_ref[dirn]
    s_old = s_ref[dirn]
    inter = _dot_nt(s_old.astype(BF16), q)
    m_new = jnp.maximum(b_end + m_prev, a_max)
    s_ref[dirn] = jnp.exp(b_end + m_prev - m_new) * s_old + jnp.exp(a_max - m_new) * u
    m_ref[dirn] = m_new
    return inter, m_prev


def _mlstm_state_finish(inter, m_prev, nl_t, rows):
    m_loc, den_loc, b = rows[0:1], rows[1:2], rows[2:3]
    m_inter = b + m_prev
    m_t = jnp.maximum(m_inter, m_loc)
    w_inter = jnp.exp(m_inter - m_t)
    w_loc = jnp.exp(m_loc - m_t)
    den = w_inter * inter[M_DV:M_DV + 1] + w_loc * den_loc
    scale = 1.0 / jnp.maximum(jnp.abs(den), jnp.exp(-m_t))
    return (w_inter * scale) * inter[:M_DV] + (w_loc * scale) * nl_t


def _mlstm_kernel(q_ref, qt_ref, k_ref, kt_ref, v_ref, vt_ref, g_ref, gt_ref, o_ref, ot_ref,
                  nl_s, u_s, rows_s, ht_s, s_ref, m_ref):
    s_ref[...] = jnp.zeros_like(s_ref)
    m_ref[...] = jnp.zeros_like(m_ref)
    tail = N_MCHUNK

    def cols(c):
        return pl.ds(SEQ, TAIL) if c == tail else pl.ds(c * MT, MT)

    def rows(ref, ref_tail, c):
        return ref_tail[...] if c == tail else ref[c * MT:(c + 1) * MT, :]

    def gates(c):
        return gt_ref[...] if c == tail else g_ref[:, c * MT:(c + 1) * MT]

    order = [tail] + list(range(N_MCHUNK))
    started = [_mlstm_local_start(rows(q_ref, qt_ref, c), rows(k_ref, kt_ref, c), rows(v_ref, vt_ref, c),
                                  gates(c), c == tail) for c in order]
    for c, st in zip(order, started):
        for dirn, (nl_t, u, row_vecs) in enumerate(_mlstm_local_finish(*st)):
            nl_s[dirn, :, cols(c)] = nl_t
            u_s[dirn, c] = u
            rows_s[dirn, :, cols(c)] = row_vecs

    fwd = [(0, c) for c in order]
    bwd = [(1, c) for c in reversed(order)]
    visits = [v for pair in zip(fwd, bwd) for v in pair]
    seen = set()
    for lo in range(0, len(visits), STATE_GROUP):
        group = visits[lo:lo + STATE_GROUP]
        started = [_mlstm_state_start(dirn, rows(q_ref, qt_ref, c), u_s[dirn, c], rows_s[dirn, :, cols(c)],
                                      s_ref, m_ref) for dirn, c in group]
        for (inter, m_prev), (dirn, c) in zip(started, group):
            h_t = _mlstm_state_finish(inter, m_prev, nl_s[dirn, :, cols(c)], rows_s[dirn, :, cols(c)])
            if c not in seen:
                seen.add(c)
                ht_s[:, cols(c)] = h_t
                continue
            out = jnp.transpose((ht_s[:, cols(c)] + h_t).astype(o_ref.dtype))
            if c == tail:
                ot_ref[...] = out
            else:
                o_ref[c * MT:(c + 1) * MT, :] = out


def _mlstm(q, k, xm, grow):
    tail_blk = N_REAL // TAIL
    real = lambda w: pl.BlockSpec((SEQ, w), lambda b, h: (b, h))
    tail = lambda w: pl.BlockSpec((TAIL, w), lambda b, h: (tail_blk + b, h))
    lp = SEQ + TAIL
    return pl.pallas_call(
        _mlstm_kernel,
        grid=(BATCH, M_HEADS),
        in_specs=[real(M_DK), tail(M_DK), real(M_DK), tail(M_DK), real(M_DV), tail(M_DV),
                  pl.BlockSpec((GATE_PAD, SEQ), lambda b, h: (h, b)),
                  pl.BlockSpec((GATE_PAD, TAIL), lambda b, h: (h, tail_blk + b))],
        out_specs=[pl.BlockSpec((SEQ, M_DV), lambda b, h: (b, h)),
                   pl.BlockSpec((TAIL, M_DV), lambda b, h: (b, h))],
        out_shape=[jax.ShapeDtypeStruct((N_REAL, M_INNER), BF16),
                   jax.ShapeDtypeStruct((BATCH * TAIL, M_INNER), BF16)],
        scratch_shapes=[pltpu.VMEM((2, M_DV, lp), F32),
                        pltpu.VMEM((2, N_MCHUNK + 1, STATE_ROWS, M_DK), F32),
                        pltpu.VMEM((2, ROWS_PER_DIR, lp), F32),
                        pltpu.VMEM((M_DV, lp), F32),
                        pltpu.VMEM((2, STATE_ROWS, M_DK), F32),
                        pltpu.VMEM((2, 1, 1), F32)],
        compiler_params=_params("parallel", "parallel", vmem_mib=40),
        name="mlstm",
    )(q, q, k, k, xm, xm, grow, grow)


def _natten_kernel(q_ref, qt_ref, k_ref, kt_ref, v_ref, vt_ref, bias_ref, o_ref, ot_ref):
    lane = lax.broadcasted_iota(jnp.int32, (1, PAIR), 1)
    first = lane < NA_DH
    k_tail = kt_ref[...]
    v_tail = vt_ref[...]

    def attend(q, keys, vals, bias_t):
        n = q.shape[0]
        zero = jnp.zeros_like(q)
        qs = jnp.concatenate([jnp.where(first, q, zero), jnp.where(first, zero, q)], axis=0)
        s = _dot_nt(keys, qs) + bias_t
        e = jnp.exp2(s - jnp.max(s, axis=0, keepdims=True))
        o = _dot(jnp.transpose(e.astype(BF16)), vals)
        inv = jnp.broadcast_to(1.0 / jnp.sum(e, axis=0, keepdims=True), (PAIR, 2 * n))
        o = o * jnp.transpose(inv)
        return jnp.where(first, o[:n], o[n:])

    for r in range(ROWS):
        r0 = min(max(r - NA_WIN_H // 2, 0), ROWS - NA_WIN_H)
        qrows = pl.ds(r * GRID_W, GRID_W)
        krows = pl.ds(r0 * GRID_W, NA_WIN_H * GRID_W)
        keys = jnp.concatenate([k_ref[krows, :], k_tail], axis=0)
        vals = jnp.concatenate([v_ref[krows, :], v_tail], axis=0)
        o = attend(q_ref[qrows, :], keys, vals, bias_ref[r - r0])
        o_ref[qrows, :] = o.astype(o_ref.dtype)
    tb = bias_ref[0, NA_WIN_H * GRID_W:, :]
    tail_bias = jnp.concatenate([jnp.broadcast_to(tb[:, 0:1], (TAIL, TAIL)),
                                 jnp.broadcast_to(tb[:, NA_DH:NA_DH + 1], (TAIL, TAIL))], axis=1)
    ot_ref[...] = attend(qt_ref[...], k_tail, v_tail, tail_bias).astype(ot_ref.dtype)


def _natten(qn, kn, vn, bias):
    tail_blk = N_REAL // TAIL
    real = pl.BlockSpec((SEQ, PAIR), lambda b, p: (b, p))
    tail = pl.BlockSpec((TAIL, PAIR), lambda b, p: (tail_blk + b, p))
    return pl.pallas_call(
        _natten_kernel,
        grid=(BATCH, NA_HEADS // 2),
        in_specs=[real, tail, real, tail, real, tail,
                  pl.BlockSpec((None, NA_WIN_H, NA_KEYS, PAIR), lambda b, p: (p, 0, 0, 0))],
        out_specs=[pl.BlockSpec((SEQ, PAIR), lambda b, p: (b, p)),
                   pl.BlockSpec((TAIL, PAIR), lambda b, p: (b, p))],
        out_shape=[jax.ShapeDtypeStruct((N_REAL, NA_INNER), BF16),
                   jax.ShapeDtypeStruct((BATCH * TAIL, NA_INNER), BF16)],
        compiler_params=_params("parallel", "parallel", vmem_mib=32),
        name="natten",
    )(qn, qn, kn, kn, vn, vn, bias)


def _natten_bias(rpb, meta_bias):
    qc = jnp.arange(GRID_W)
    kc = jnp.arange(GRID_W)
    win0 = jnp.clip(qc - NA_WIN_W // 2, 0, GRID_W - NA_WIN_W)
    ok = (kc[:, None] >= win0[None, :]) & (kc[:, None] < win0[None, :] + NA_WIN_W)
    dc = jnp.clip(kc[:, None] - qc[None, :], -(NA_WIN_W - 1), NA_WIN_W - 1) + NA_WIN_W - 1
    onehot = (dc[None] == jnp.arange(2 * NA_WIN_W - 1)[:, None, None]).astype(F32)
    t1 = jnp.einsum('hdj,jkq->dkhq', rpb.astype(F32), onehot, precision=lax.Precision.HIGHEST)
    t1 = jnp.where(ok[None, :, None, :], t1, MASKED) * LOG2E
    t1 = jnp.transpose(t1.reshape(2 * NA_WIN_H - 1, GRID_W, NA_HEADS // 2, PAIR), (2, 0, 1, 3))
    met = jnp.concatenate([jnp.full((NA_HEADS, NPAD), MASKED, F32), meta_bias.astype(F32)], axis=1) * LOG2E
    met = jnp.broadcast_to(met.reshape(NA_HEADS // 2, 2, TAIL, 1), (NA_HEADS // 2, 2, TAIL, NA_DH))
    met = jnp.transpose(met, (0, 2, 1, 3)).reshape(NA_HEADS // 2, TAIL, PAIR)
    return t1, met


def _merge_kernel(hsr_ref, hst_ref, ybr_ref, ybt_ref, xc_ref, og_ref, gab_ref,
                  ng_ref, sk_ref, wa_ref, wb_ref, wo_ref, w1_ref, w2_ref, o_ref, w1o_ref, w2o_ref):
    w1o_ref[...] = w1_ref[...].astype(BF16)
    w2o_ref[...] = w2_ref[...].astype(BF16)

    def body(hs_ref, yb_ref):
        hs = hs_ref[...].astype(F32)
        parts = []
        for hd in range(M_HEADS):
            sl = hs[:, hd * M_DV:(hd + 1) * M_DV]
            parts.append(sl * lax.rsqrt(jnp.mean(sl * sl, axis=-1, keepdims=True) + EPS))
        hn = jnp.concatenate(parts, axis=1) * ng_ref[...]
        y_a = _sigmoid(og_ref[...].astype(F32)) * (hn + sk_ref[...] * xc_ref[...].astype(F32))
        gab = gab_ref[...].astype(F32)
        mix = (_sigmoid(gab[:, :D_MODEL]) * _dot(y_a.astype(BF16), wa_ref[...])
               + _sigmoid(gab[:, D_MODEL:]) * _dot(yb_ref[...], wb_ref[...]))
        o_ref[...] = _dot(mix.astype(BF16), wo_ref[...])

    _real_or_tail(body, hsr_ref, hst_ref, ybr_ref, ybt_ref)


def _merge(hs_real, hs_tail, yb_real, yb_tail, xc, og, gab, ng, sk, wa, wb, wo, w1, w2):
    return pl.pallas_call(
        _merge_kernel,
        grid=(N_TILES,),
        in_specs=[_real_spec(M_INNER), _tail_spec(M_INNER), _real_spec(NA_INNER), _tail_spec(NA_INNER),
                  _row_spec(M_INNER), _row_spec(M_INNER), _row_spec(2 * D_MODEL)]
                 + [_const_spec(a.shape) for a in (ng, sk, wa, wb, wo)]
                 + [_slab_spec(w1.shape), _slab_spec(w2.shape)],
        out_specs=[_row_spec(D_MODEL), _slab_spec(w1.shape), _slab_spec(w2.shape)],
        out_shape=[jax.ShapeDtypeStruct((NT, D_MODEL), F32),
                   jax.ShapeDtypeStruct(w1.shape, BF16), jax.ShapeDtypeStruct(w2.shape, BF16)],
        compiler_params=_params("arbitrary", vmem_mib=40),
        name="merge",
    )(hs_real, hs_tail, yb_real, yb_tail, xc, og, gab, ng, sk, wa, wb, wo, w1, w2)


def _ffn_kernel(d_ref, xr_ref, xt_ref, g2_ref, w1_ref, w2_ref, or_ref, ot_ref):
    def body(x_ref, o_ref):
        h = x_ref[...] + d_ref[...]
        xn = (h * lax.rsqrt(jnp.mean(h * h, axis=-1, keepdims=True) + EPS) * g2_ref[...]).astype(BF16)
        z = jnp.maximum(_dot(xn, w1_ref[...]), 0.0)
        o_ref[...] = h + _dot((z * z).astype(BF16), w2_ref[...])

    _real_or_tail(body, xr_ref, xt_ref, or_ref, ot_ref)


def _ffn(delta, x2d, tailh, g2, w1, w2):
    return pl.pallas_call(
        _ffn_kernel,
        grid=(N_TILES,),
        in_specs=[_row_spec(D_MODEL), _real_spec(D_MODEL), _tail_spec(D_MODEL),
                  _const_spec(g2.shape), _const_spec(w1.shape), _const_spec(w2.shape)],
        out_specs=[_real_spec(D_MODEL), _tail_spec(D_MODEL)],
        out_shape=[jax.ShapeDtypeStruct((N_REAL, D_MODEL), F32),
                   jax.ShapeDtypeStruct((BATCH * TAIL, D_MODEL), F32)],
        compiler_params=_params("arbitrary", vmem_mib=48),
        name="ffn",
    )(delta, x2d, tailh, g2, w1, w2)


def kernel(x, meta_tokens, norm1_g, w_in, mlstm_conv_w, mlstm_conv_b, mlstm_wq, mlstm_wk, mlstm_gate_b, mlstm_norm_g, mlstm_skip, na_q_norm_g, na_k_norm_g, na_rpb, na_meta_bias, w_branch_a, w_branch_b, w_out, norm2_g, w_ff1, w_ff2):
    x2d = x.astype(F32).reshape(N_REAL, D_MODEL)
    tail = jnp.concatenate([jnp.zeros((NPAD, D_MODEL), F32), meta_tokens.astype(F32)], axis=0)
    tailh = jnp.tile(tail, (BATCH, 1))

    w_all = w_in.astype(BF16)
    pad_gates = lambda a: jnp.pad(
        jnp.swapaxes(a.reshape(a.shape[0], 4, M_HEADS), 1, 2),
        ((0, 0), (0, 0), (0, GATE_PAD - 4))).reshape(a.shape[0], M_HEADS * GATE_PAD)
    wg = jnp.pad(pad_gates(w_all[:, OFF_MG:OFF_Q]), ((0, 0), (0, LANES - M_HEADS * GATE_PAD)))
    gb = jnp.pad(pad_gates(mlstm_gate_b.astype(F32).reshape(1, 4 * M_HEADS)),
                 ((0, 0), (0, LANES - M_HEADS * GATE_PAD)))
    bd = jnp.kron(jnp.eye(NORM_GROUP // NA_DH, dtype=F32),
                  jnp.full((NA_DH, NA_DH), 1.0 / NA_DH, F32)).astype(BF16)
    qg = jnp.tile(na_q_norm_g.astype(F32), NA_HEADS)[None, :] * (NA_DH ** -0.5 * LOG2E)
    kg = jnp.tile(na_k_norm_g.astype(F32), NA_HEADS)[None, :]

    xb = x.astype(F32)
    xedge = jnp.concatenate([xb[:, SEQ - 1, :], xb[:, 0, :]], axis=0)
    wqk = jnp.concatenate([mlstm_wq, mlstm_wk], axis=-1).astype(BF16)
    vm, og, grow, qn, kn, vn, gab, xc, qm, km, bias = _inproj(
        x2d, tailh, xedge, norm1_g.astype(F32)[None, :], w_all, wg, gb, qg, kg, bd,
        mlstm_conv_w.astype(F32).reshape(3, M_INNER), mlstm_conv_b.astype(F32)[None, :], wqk,
        *_natten_bias(na_rpb, na_meta_bias))

    hs_real, hs_tail = _mlstm(qm, km, vm, grow)
    yb_real, yb_tail = _natten(qn, kn, vn, bias)

    delta, w1, w2 = _merge(hs_real, hs_tail, yb_real, yb_tail, xc, og, gab,
                           mlstm_norm_g.astype(F32).reshape(1, M_INNER), mlstm_skip.astype(F32)[None, :],
                           w_branch_a.astype(BF16), w_branch_b.astype(BF16), w_out.astype(BF16),
                           w_ff1.astype(F32), w_ff2.astype(F32))
    out_real, _ = _ffn(delta, x2d, tailh, norm2_g.astype(F32)[None, :], w1, w2)
    return out_real.reshape(BATCH, SEQ, D_MODEL)
```
